```python
import math
import jax
import jax.numpy as jnp
from jax import lax
import numpy as np

D_MODEL = 1024
BATCH = 8
SEQ = 4096
DEPTH = 2

CTX_LEN = 256
GRID_W = 64

HEAD_DIM = 64
N_MIXERS = 4
GROUP_WIDTH = D_MODEL // N_MIXERS
MIX_WIDTH = N_MIXERS * GROUP_WIDTH
N_MOD = 9
D_FF = 2816

SWA_HEADS = GROUP_WIDTH // HEAD_DIM
SWA_KV_HEADS = 2
SWA_WINDOW = 128
SWA_BLOCK = 128
SSD_HEADS = GROUP_WIDTH // HEAD_DIM
SSD_HEAD_DIM = HEAD_DIM
SSD_INNER = SSD_HEADS * SSD_HEAD_DIM
SSD_STATE = 128
SSD_BC_GROUPS = 2
SSD_CONV = 5
SSD_CHUNK = 128
SSD_CONV_CH = SSD_INNER + 2 * SSD_BC_GROUPS * SSD_STATE
DT_MIN = 0.001
DT_MAX = 0.1
GQA_HEADS = GROUP_WIDTH // HEAD_DIM
GQA_KV_HEADS = 2
GQA_BLOCK = 128
NA_HEADS = GROUP_WIDTH // HEAD_DIM
NA_ROWS = 8
NA_COLS = 16

ROPE_BASE = 10000.0
EPS = 1e-5
ALPHA = (2 * DEPTH) ** 0.25
BETA = (8 * DEPTH) ** -0.25

IN_SPLITS = (SWA_HEADS * HEAD_DIM, SWA_KV_HEADS * HEAD_DIM, SWA_KV_HEADS * HEAD_DIM,
             SSD_INNER, SSD_CONV_CH, 2 * SSD_HEADS,
             GQA_HEADS * HEAD_DIM, GQA_KV_HEADS * HEAD_DIM, GQA_KV_HEADS * HEAD_DIM,
             NA_HEADS * HEAD_DIM, NA_HEADS * HEAD_DIM, NA_HEADS * HEAD_DIM)
IN_WIDTH = sum(IN_SPLITS)

kernel_name = 'hybrid_parallel_group_diffusion_block'


def _heads(t):
    return t.reshape(t.shape[:-1] + (t.shape[-1] // HEAD_DIM, HEAD_DIM))


def _flip(t):
    return jnp.flip(t, axis=1)


def layer_norm(x, g, b):
    xf = x.astype(jnp.float32)
    mu = jnp.mean(xf, -1, keepdims=True)
    var = jnp.mean(jnp.square(xf - mu), -1, keepdims=True)
    return ((xf - mu) * lax.rsqrt(var + EPS) * g.astype(jnp.float32) + b.astype(jnp.float32)).astype(x.dtype)


def rms_norm(x, g):
    xf = x.astype(jnp.float32)
    return (xf * lax.rsqrt(jnp.mean(xf * xf, -1, keepdims=True) + EPS) * g.astype(jnp.float32)).astype(x.dtype)


def modulate(x, shift, scale):
    return x * (1 + scale) + shift


def post_norm(x, f, g, b):
    return layer_norm(ALPHA * x + f, g, b)


def swiglu(h, w_in, w_out):
    a, u = jnp.split(h @ w_in, 2, axis=-1)
    return (jax.nn.silu(a) * u) @ w_out


def rope_tables(S, dtype):
    t = jnp.arange(S)
    pos = jnp.stack([t // GRID_W, t % GRID_W], -1).astype(jnp.float32)
    quarter = HEAD_DIM // 4
    inv = ROPE_BASE ** (-jnp.arange(quarter, dtype=jnp.float32) / quarter)
    ang = pos[:, None, :, None] * inv
    return jnp.cos(ang).astype(dtype), jnp.sin(ang).astype(dtype)


def rope2d(x, cos, sin):
    xr = x.reshape(x.shape[:-1] + (2, 2, HEAD_DIM // 4))
    x1, x2 = xr[..., 0, :], xr[..., 1, :]
    out = jnp.stack([x1 * cos - x2 * sin, x1 * sin + x2 * cos], -2)
    return out.reshape(x.shape)


def dense_attn(q, k, v, sink=None):
    Bq, L, Hq, dh = q.shape
    g = k.shape[2]
    r = Hq // g
    qg = q.reshape(Bq, L, g, r, dh)
    s = jnp.einsum('blgrd,bmgd->bgrlm', qg, k).astype(jnp.float32) * (dh ** -0.5)
    if sink is not None:
        sk = jnp.broadcast_to(sink.astype(jnp.float32).reshape(1, g, r, 1, 1), s.shape[:-1] + (1,))
        p = jax.nn.softmax(jnp.concatenate([s, sk], -1), -1)[..., :-1]
    else:
        p = jax.nn.softmax(s, -1)
    o = jnp.einsum('bgrlm,bmgd->blgrd', p.astype(v.dtype), v)
    return o.reshape(Bq, L, Hq * dh)


def swa_latent(q, k, v, k_ctx, v_ctx, sink):
    Bq, S, Hq, dh = q.shape
    g = k.shape[2]
    r = Hq // g
    Q = SWA_BLOCK
    nb = S // Q
    M = k_ctx.shape[1]
    qb = q.reshape(Bq, nb, Q, g, r, dh)

    def band(t):
        tp = jnp.pad(t, ((0, 0), (Q, Q), (0, 0), (0, 0))).reshape(Bq, nb + 2, Q, g, dh)
        return jnp.concatenate([tp[:, :-2], tp[:, 1:-1], tp[:, 2:]], axis=2)

    kb, vb = band(k), band(v)
    qpos = jnp.arange(nb)[:, None] * Q + jnp.arange(Q)[None]
    kpos = (jnp.arange(nb)[:, None] - 1) * Q + jnp.arange(3 * Q)[None]
    off = kpos[:, None, :] - qpos[:, :, None]
    valid = (jnp.abs(off) <= SWA_WINDOW) & (kpos[:, None, :] >= 0) & (kpos[:, None, :] < S)
    scale = dh ** -0.5
    s_loc = jnp.einsum('bnqgrd,bnkgd->bngrqk', qb, kb).astype(jnp.float32) * scale
    s_loc = jnp.where(valid[None, :, None, None], s_loc, -jnp.inf)
    s_ctx = jnp.einsum('bnqgrd,bmgd->bngrqm', qb, k_ctx).astype(jnp.float32) * scale
    sk = jnp.broadcast_to(sink.astype(jnp.float32).reshape(1, 1, g, r, 1, 1), s_ctx.shape[:-1] + (1,))
    p = jax.nn.softmax(jnp.concatenate([s_loc, s_ctx, sk], -1), -1).astype(v.dtype)
    o = (jnp.einsum('bngrqk,bnkgd->bnqgrd', p[..., :3 * Q], vb)
         + jnp.einsum('bngrqm,bmgd->bnqgrd', p[..., 3 * Q:3 * Q + M], v_ctx))
    return o.reshape(Bq, S, Hq * dh)


def gqa_latent(q, k, v, k_ctx, v_ctx):
    Bq, S, Hq, dh = q.shape
    nb = S // GQA_BLOCK
    k_all = jnp.concatenate([k, k_ctx], axis=1)
    v_all = jnp.concatenate([v, v_ctx], axis=1)
    qb = jnp.moveaxis(q.reshape(Bq, nb, GQA_BLOCK, Hq, dh), 1, 0)
    o = lax.map(lambda qblk: dense_attn(qblk, k_all, v_all), qb)
    return jnp.moveaxis(o, 0, 1).reshape(Bq, S, Hq * dh)


def na_latent(q, k, v, k_ctx, v_ctx, rpb):
    Bq, S, H, dh = q.shape
    W = GRID_W
    rows = S // W
    KH = min(NA_ROWS, rows)
    KW = NA_COLS
    qg = q.reshape(Bq, rows, W, H, dh)
    kg = k.reshape(Bq, rows, W, H, dh)
    vg = v.reshape(Bq, rows, W, H, dh)
    cidx = jnp.arange(W)
    col_idx = jnp.clip(cidx - KW // 2, 0, W - KW)[:, None] + jnp.arange(KW)[None]
    dc = col_idx - cidx[:, None] + NA_COLS - 1
    rpb_c = rpb.astype(jnp.float32)[:, :, dc]
    scale = dh ** -0.5

    def one_row(args):
        r, q_row = args
        rs = jnp.clip(r - KH // 2, 0, rows - KH)
        k_nb = lax.dynamic_slice_in_dim(kg, rs, KH, axis=1)[:, :, col_idx]
        v_nb = lax.dynamic_slice_in_dim(vg, rs, KH, axis=1)[:, :, col_idx]
        dr = rs + jnp.arange(KH) - r + NA_ROWS - 1
        bias = jnp.transpose(jnp.take(rpb_c, dr, axis=1), (0, 2, 1, 3))
        s_nb = jnp.einsum('bwhd,bawkhd->bhwak', q_row, k_nb).astype(jnp.float32) * scale + bias
        s_ctx = jnp.einsum('bwhd,bmhd->bhwm', q_row, k_ctx).astype(jnp.float32) * scale
        p = jax.nn.softmax(jnp.concatenate([s_nb.reshape(Bq, H, W, KH * KW), s_ctx], -1), -1)
        p = p.astype(v.dtype)
        p_nb = p[..., :KH * KW].reshape(Bq, H, W, KH, KW)
        return (jnp.einsum('bhwak,bawkhd->bwhd', p_nb, v_nb)
                + jnp.einsum('bhwm,bmhd->bwhd', p[..., KH * KW:], v_ctx))

    o = lax.map(one_row, (jnp.arange(rows), jnp.moveaxis(qg, 1, 0)))
    return jnp.moveaxis(o, 0, 1).reshape(Bq, S, H * dh)


def dwconv(u, w, b):
    out = lax.conv_general_dilated(u, w.astype(u.dtype), window_strides=(1,), padding='SAME',
                                   dimension_numbers=('NWC', 'WIO', 'NWC'),
                                   feature_group_count=u.shape[-1])
    return jax.nn.silu(out + b)


def ssd_prep(xbc, dt_raw, conv_w, conv_b, dt_bias):
    u = dwconv(xbc, conv_w, conv_b)
    xs, Bm, Cm = jnp.split(u, [SSD_INNER, SSD_INNER + SSD_BC_GROUPS * SSD_STATE], axis=-1)
    lead = u.shape[:-1]
    rep = SSD_HEADS // SSD_BC_GROUPS
    xs = xs.reshape(lead + (SSD_HEADS, SSD_HEAD_DIM))
    Bh = jnp.repeat(Bm.reshape(lead + (SSD_BC_GROUPS, SSD_STATE)), rep, axis=-2)
    Ch = jnp.repeat(Cm.reshape(lead + (SSD_BC_GROUPS, SSD_STATE)), rep, axis=-2)
    dt = jax.nn.softplus(dt_raw.astype(jnp.float32).reshape(lead + (2, SSD_HEADS))
                         + dt_bias.astype(jnp.float32))
    return xs, Bh, Ch, dt


def ssd_chunked(x, dt, A, Bh, Ch, h0):
    Bsz, L, H, P = x.shape
    N = Bh.shape[-1]
    Q = SSD_CHUNK
    nc = L // Q
    xc = x.reshape(Bsz, nc, Q, H, P)
    Bc = Bh.reshape(Bsz, nc, Q, H, N)
    Cc = Ch.reshape(Bsz, nc, Q, H, N)
    dtc = dt.reshape(Bsz, nc, Q, H)
    cum = jnp.cumsum(dtc * A, axis=2)
    seg = cum[:, :, :, None, :] - cum[:, :, None, :, :]
    tril = jnp.tril(jnp.ones((Q, Q), bool))[None, None, :, :, None]
    lmat = jnp.exp(jnp.where(tril, seg, -jnp.inf))
    att = jnp.einsum('bcihn,bcjhn->bcijh', Cc, Bc) * lmat * dtc[:, :, None]
    y = jnp.einsum('bcijh,bcjhp->bcihp', att, xc)
    w_end = jnp.exp(cum[:, :, -1:] - cum) * dtc
    states = jnp.einsum('bcjhn,bcjh,bcjhp->bchpn', Bc, w_end, xc)
    chunk_decay = jnp.exp(cum[:, :, -1])

    def step(h, inp):
        st, dec = inp
        return h * dec[:, :, None, None] + st, h

    h_last, h_start = lax.scan(step, h0, (jnp.moveaxis(states, 1, 0), jnp.moveaxis(chunk_decay, 1, 0)))
    y = y + jnp.einsum('bcihn,bchpn,bcih->bcihp', Cc, jnp.moveaxis(h_start, 0, 1), jnp.exp(cum))
    return y.reshape(Bsz, L, H, P), h_last


def ssd_final_state(x, dt, A, Bh):
    cum = jnp.cumsum(dt * A, axis=1)
    w_end = jnp.exp(cum[:, -1:] - cum) * dt
    return jnp.einsum('blhn,blh,blhp->bhpn', Bh, w_end, x)


def ssd_bidir(xs, dt, A, Bh, Ch, h0f, h0b):
    yf, hf = ssd_chunked(xs, dt[..., 0, :], A[0], Bh, Ch, h0f)
    yb, hb = ssd_chunked(_flip(xs), _flip(dt[..., 1, :]), A[1], _flip(Bh), _flip(Ch), h0b)
    return yf + _flip(yb), hf, hb


def ssd_out(y, xs, z, d_skip):
    y = y + d_skip.astype(jnp.float32)[:, None] * xs
    return (y.reshape(z.shape) * jax.nn.silu(z)).astype(z.dtype)


def merge_groups(outs, g, w_out):
    y = jnp.concatenate(outs, -1)
    yf = y.astype(jnp.float32).reshape(y.shape[:-1] + (N_MIXERS, GROUP_WIDTH))
    yf = yf * lax.rsqrt(jnp.mean(yf * yf, -1, keepdims=True) + EPS)
    yf = yf * g.astype(jnp.float32).reshape(N_MIXERS, GROUP_WIDTH)
    return yf.reshape(y.shape).astype(y.dtype) @ w_out


def project(h, w_in):
    idx = np.cumsum(IN_SPLITS)[:-1].tolist()
    return jnp.split(h @ w_in, idx, axis=-1)


def hybrid_mixer(h, hc, cos, sin, w_in, w_out, norm_g, sink, conv_w, conv_b, dt_bias, a_log,
                 d_skip, q_norm, k_norm, rpb, last):
    aq, ak, av, bz, bxbc, bdt, cq, ck, cv, dq, dk, dv = project(h, w_in)
    aq_c, ak_c, av_c, bz_c, bxbc_c, bdt_c, cq_c, ck_c, cv_c, dq_c, dk_c, dv_c = project(hc, w_in)
    A = -jnp.exp(a_log.astype(jnp.float32))
    ka_c, va_c = _heads(ak_c), _heads(av_c)
    kc_c, vc_c = rms_norm(_heads(ck_c), k_norm), _heads(cv_c)
    kd_c, vd_c = _heads(dk_c), _heads(dv_c)
    xs_c, bh_c, ch_c, dt_c = ssd_prep(bxbc_c, bdt_c, conv_w, conv_b, dt_bias)
    if last:
        hf_c = ssd_final_state(xs_c, dt_c[..., 0, :], A[0], bh_c)
        hb_c = ssd_final_state(_flip(xs_c), _flip(dt_c[..., 1, :]), A[1], _flip(bh_c))
    else:
        zeros = jnp.zeros((xs_c.shape[0], SSD_HEADS, SSD_HEAD_DIM, SSD_STATE), jnp.float32)
        y_c, hf_c, hb_c = ssd_bidir(xs_c, dt_c, A, bh_c, ch_c, zeros, zeros)
    oa = swa_latent(rope2d(_heads(aq), cos, sin), rope2d(_heads(ak), cos, sin), _heads(av), ka_c, va_c, sink)
    xs, bh, ch, dt = ssd_prep(bxbc, bdt, conv_w, conv_b, dt_bias)
    y, _, _ = ssd_bidir(xs, dt, A, bh, ch, hf_c, hb_c)
    ob = ssd_out(y, xs, bz, d_skip)
    oc = gqa_latent(rope2d(rms_norm(_heads(cq), q_norm), cos, sin),
                    rope2d(rms_norm(_heads(ck), k_norm), cos, sin), _heads(cv), kc_c, vc_c)
    od = na_latent(_heads(dq), _heads(dk), _heads(dv), kd_c, vd_c, rpb)
    mix_x = merge_groups([oa, ob, oc, od], norm_g, w_out)
    if last:
        return mix_x, None
    oa_c = dense_attn(_heads(aq_c), ka_c, va_c, sink)
    ob_c = ssd_out(y_c, xs_c, bz_c, d_skip)
    oc_c = dense_attn(rms_norm(_heads(cq_c), q_norm), kc_c, vc_c)
    od_c = dense_attn(_heads(dq_c), kd_c, vd_c)
    return mix_x, merge_groups([oa_c, ob_c, oc_c, od_c], norm_g, w_out)


def setup_inputs(seed: int = 0) -> dict:
    key = jax.random.key(seed)
    ks = jax.random.split(key, 24)
    f32 = jnp.float32
    D = D_MODEL
    L = DEPTH

    def nrm(k, shape, std):
        return jax.random.normal(k, shape, f32) * std

    dt0 = jnp.exp(jax.random.uniform(ks[16], (L, 2, SSD_HEADS), f32)
                  * (math.log(DT_MAX) - math.log(DT_MIN)) + math.log(DT_MIN))
    return {
        'x': nrm(ks[0], (BATCH, SEQ, D), 1.0),
        'c': nrm(ks[1], (BATCH, D), 1.0),
        'ctx': nrm(ks[2], (BATCH, CTX_LEN, D), 1.0),
        'c_ctx': nrm(ks[3], (D,), 1.0),
        'ada_w': nrm(ks[4], (L, D, N_MOD * D), 0.5 * D ** -0.5),
        'ada_b': nrm(ks[5], (L, N_MOD * D), 0.01),
        'ln_g': 1.0 + nrm(ks[6], (L, 3, D), 0.05),
        'ln_b': nrm(ks[7], (L, 3, D), 0.01),
        'ffn1_w_in': nrm(ks[8], (L, D, 2 * D_FF), D ** -0.5),
        'ffn1_w_out': nrm(ks[9], (L, D_FF, D), BETA * D_FF ** -0.5),
        'mix_w_in': nrm(ks[10], (L, D, IN_WIDTH), D ** -0.5),
        'mix_w_out': nrm(ks[11], (L, MIX_WIDTH, D), BETA * MIX_WIDTH ** -0.5),
        'mix_norm_g': 1.0 + nrm(ks[12], (L, MIX_WIDTH), 0.05),
        'swa_sink': nrm(ks[13], (L, SWA_HEADS), 1.0),
        'ssd_conv_w': nrm(ks[14], (L, SSD_CONV, 1, SSD_CONV_CH), SSD_CONV ** -0.5),
        'ssd_conv_b': nrm(ks[15], (L, SSD_CONV_CH), 0.01),
        'ssd_dt_bias': dt0 + jnp.log(-jnp.expm1(-dt0)),
        'ssd_A_log': jnp.log(jax.random.uniform(ks[17], (L, 2, SSD_HEADS), f32, 1.0, 16.0)),
        'ssd_D': 1.0 + nrm(ks[18], (L, SSD_HEADS), 0.1),
        'gqa_q_norm': 1.0 + nrm(ks[19], (L, HEAD_DIM), 0.05),
        'gqa_k_norm': 1.0 + nrm(ks[20], (L, HEAD_DIM), 0.05),
        'na_rpb': nrm(ks[21], (L, NA_HEADS, 2 * NA_ROWS - 1, 2 * NA_COLS - 1), 0.2),
        'ffn2_w_in': nrm(ks[22], (L, D, 2 * D_FF), D ** -0.5),
        'ffn2_w_out': nrm(ks[23], (L, D_FF, D), BETA * D_FF ** -0.5),
    }


def reference(x, c, ctx, c_ctx, ada_w, ada_b, ln_g, ln_b, ffn1_w_in, ffn1_w_out, mix_w_in, mix_w_out,
              mix_norm_g, swa_sink, ssd_conv_w, ssd_conv_b, ssd_dt_bias, ssd_A_log, ssd_D,
              gqa_q_norm, gqa_k_norm, na_rpb, ffn2_w_in, ffn2_w_out):
    Bsz, S, _ = x.shape
    cos, sin = rope_tables(S, x.dtype)
    for l in range(DEPTH):
        last = l == DEPTH - 1
        m = (jax.nn.silu(c) @ ada_w[l] + ada_b[l]).reshape(Bsz, N_MOD, 1, D_MODEL)
        mc = (jax.nn.silu(c_ctx) @ ada_w[l] + ada_b[l]).reshape(N_MOD, 1, D_MODEL)
        x = post_norm(x, 0.5 * m[:, 2] * swiglu(modulate(x, m[:, 0], m[:, 1]), ffn1_w_in[l], ffn1_w_out[l]),
                      ln_g[l, 0], ln_b[l, 0])
        ctx = post_norm(ctx, 0.5 * mc[2] * swiglu(modulate(ctx, mc[0], mc[1]), ffn1_w_in[l], ffn1_w_out[l]),
                        ln_g[l, 0], ln_b[l, 0])
        mix_x, mix_c = hybrid_mixer(modulate(x, m[:, 3], m[:, 4]), modulate(ctx, mc[3], mc[4]), cos, sin,
                                    mix_w_in[l], mix_w_out[l], mix_norm_g[l], swa_sink[l],
                                    ssd_conv_w[l], ssd_conv_b[l], ssd_dt_bias[l], ssd_A_log[l], ssd_D[l],
                                    gqa_q_norm[l], gqa_k_norm[l], na_rpb[l], last)
        x = post_norm(x, m[:, 5] * mix_x, ln_g[l, 1], ln_b[l, 1])
        x = post_norm(x, 0.5 * m[:, 8] * swiglu(modulate(x, m[:, 6], m[:, 7]), ffn2_w_in[l], ffn2_w_out[l]),
                      ln_g[l, 2], ln_b[l, 2])
        if not last:
            ctx = post_norm(ctx, mc[5] * mix_c, ln_g[l, 1], ln_b[l, 1])
            ctx = post_norm(ctx, 0.5 * mc[8] * swiglu(modulate(ctx, mc[6], mc[7]), ffn2_w_in[l], ffn2_w_out[l]),
                            ln_g[l, 2], ln_b[l, 2])
    return x
```

```python
import functools

import numpy as np
import jax
import jax.numpy as jnp
from jax import lax
from jax.experimental import pallas as pl
from jax.experimental.pallas import tpu as pltpu

F32 = jnp.float32
BF16 = jnp.bfloat16

HEAD_DIM = 64
PAIR = 2 * HEAD_DIM
GROUP_WIDTH = 256
GRID_W = 64
N_MOD = 9
SWA_WINDOW = 128
SWA_BLOCK = 128
SSD_CHUNK = 128
SSD_STATE = 128
SSD_CONV = 5
SSD_HALO = 16
SSD_CONV_CH = 768
NA_ROWS = 8
NA_COLS = 16
NA_QROWS = 4
NA_KROWS = 12
ROPE_BASE = 10000.0
EPS = 1e-5
NEG = -1e30
Q_SCALE = HEAD_DIM ** -0.5
DT_PAD = 128
HEAD_PERM = (0, 2, 1, 3)

VMEM_LIMIT_BYTES = 56 * 1024 * 1024


def _cparams(*sem):
    return pltpu.CompilerParams(dimension_semantics=sem, vmem_limit_bytes=VMEM_LIMIT_BYTES)


def _dot(a, b):
    return jnp.dot(a, b, preferred_element_type=F32)


def _dot_nt(a, b):
    return lax.dot_general(a, b, (((1,), (1,)), ((), ())), preferred_element_type=F32)


def _sigmoid(x):
    return 1.0 / (1.0 + jnp.exp(-x))


def _silu(x):
    return x * _sigmoid(x)


def _softplus(x):
    return jnp.maximum(x, 0.0) + jnp.log(1.0 + jnp.exp(-jnp.abs(x)))


def _split3(a):
    hi = a.astype(BF16)
    r1 = a - hi.astype(F32)
    mid = r1.astype(BF16)
    lo = (r1 - mid.astype(F32)).astype(BF16)
    return hi, mid, lo


def _layer_norm(y, g, b):
    mu = jnp.mean(y, axis=-1, keepdims=True)
    d = y - mu
    var = jnp.mean(d * d, axis=-1, keepdims=True)
    return d * lax.rsqrt(var + EPS) * g + b


def _ada_kernel(c_ref, w_ref, b_ref, o_ref):
    s = _silu(c_ref[...])
    s_hi = s.astype(BF16)
    s_lo = (s - s_hi.astype(F32)).astype(BF16)
    w = w_ref[0]
    w_hi = w.astype(BF16)
    w_lo = (w - w_hi.astype(F32)).astype(BF16)
    o_ref[0] = _dot(s_hi, w_hi) + _dot(s_lo, w_hi) + _dot(s_hi, w_lo) + b_ref[0]


def _ada(cc, ada_w, ada_b):
    depth, d, n = ada_w.shape
    rows = cc.shape[0]
    tn = 1024
    return pl.pallas_call(
        _ada_kernel,
        grid=(depth, n // tn),
        in_specs=[
            pl.BlockSpec((rows, d), lambda l, j: (0, 0)),
            pl.BlockSpec((1, d, tn), lambda l, j: (l, 0, j)),
            pl.BlockSpec((1, 1, tn), lambda l, j: (l, 0, j)),
        ],
        out_specs=pl.BlockSpec((1, rows, tn), lambda l, j: (l, 0, j)),
        out_shape=jax.ShapeDtypeStruct((depth, rows, n), F32),
        compiler_params=_cparams("arbitrary", "arbitrary"),
        name="ada_mod",
    )(cc, ada_w, ada_b.reshape(depth, 1, n))


def _ffn_kernel(x_ref, mod_ref, wa_ref, wu_ref, wo_ref, g_ref, b_ref, o_ref, h_sc, acc_sc,
                *, rows, nf, alpha):
    j = pl.program_id(1)
    r_shift, r_scale, r_gate = rows

    @pl.when(j == 0)
    def _():
        x = x_ref[...]
        h = x * (1.0 + mod_ref[0, r_scale:r_scale + 1, :]) + mod_ref[0, r_shift:r_shift + 1, :]
        h_sc[...] = h.astype(BF16)
        acc_sc[...] = jnp.zeros_like(acc_sc)

    h = h_sc[...]
    a = _dot(h, wa_ref[...])
    u = _dot(h, wu_ref[...])
    gated = (_silu(a) * u).astype(BF16)
    acc_sc[...] += _dot(gated, wo_ref[...])

    @pl.when(j == nf - 1)
    def _():
        gate = mod_ref[0, r_gate:r_gate + 1, :]
        y = alpha * x_ref[...] + (0.5 * gate) * acc_sc[...]
        o_ref[...] = _layer_norm(y, g_ref[...], b_ref[...])


def _ffn(xt, mod, rows, w_in, w_out, ln_g, ln_b, *, tokens_per_mod, alpha, tm, tf):
    t, d = xt.shape
    dff = w_out.shape[0]
    nf = dff // tf
    tiles_per_mod = tokens_per_mod // tm
    return pl.pallas_call(
        functools.partial(_ffn_kernel, rows=rows, nf=nf, alpha=alpha),
        grid=(t // tm, nf),
        in_specs=[
            pl.BlockSpec((tm, d), lambda i, j: (i, 0)),
            pl.BlockSpec((1, N_MOD, d), lambda i, j: (i // tiles_per_mod, 0, 0)),
            pl.BlockSpec((d, tf), lambda i, j: (0, j)),
            pl.BlockSpec((d, tf), lambda i, j: (0, j + nf)),
            pl.BlockSpec((tf, d), lambda i, j: (j, 0)),
            pl.BlockSpec((1, d), lambda i, j: (0, 0)),
            pl.BlockSpec((1, d), lambda i, j: (0, 0)),
        ],
        out_specs=pl.BlockSpec((tm, d), lambda i, j: (i, 0)),
        out_shape=jax.ShapeDtypeStruct((t, d), F32),
        scratch_shapes=[pltpu.VMEM((tm, d), BF16), pltpu.VMEM((tm, d), F32)],
        compiler_params=_cparams("arbitrary", "arbitrary"),
        name="ffn",
    )(xt, mod, w_in, w_in, w_out, ln_g.reshape(1, d), ln_b.reshape(1, d))


_P_AQ, _P_AK, _P_AV, _P_Z, _P_XBC = 0, 256, 384, 512, 768
_P_CQ, _P_CK, _P_CV, _P_DQ, _P_DK, _P_DV, _P_DT, _P_END = 1536, 1792, 1920, 2048, 2304, 2560, 2816, 2944


def _rope_pair(t, cos, sin):
    lane = lax.broadcasted_iota(jnp.int32, t.shape, 1)
    up = pltpu.roll(t, HEAD_DIM // 4, 1)
    dn = pltpu.roll(t, PAIR - HEAD_DIM // 4, 1)
    partner = jnp.where((lane % 32) < 16, dn, up)
    return t * cos + partner * sin


def _rms_pair(t, g, blockdiag):
    sq = t * t
    hi = sq.astype(BF16)
    lo = (sq - hi.astype(F32)).astype(BF16)
    ss = _dot(hi, blockdiag) + _dot(lo, blockdiag)
    return t * lax.rsqrt(ss * (1.0 / HEAD_DIM) + EPS) * g


def _inproj_kernel(x_ref, mod_ref, w_ref, cos_ref, sin_ref, qg_ref, kg_ref, bd_ref,
                   aq_ref, ak_ref, av_ref, z_ref, xbc_ref, dt_ref,
                   cq_ref, ck_ref, cv_ref, dq_ref, dk_ref, dv_ref):
    x = x_ref[...]
    h = (x * (1.0 + mod_ref[0, 4:5, :]) + mod_ref[0, 3:4, :]).astype(BF16)
    y = _dot(h, w_ref[...])
    cos = cos_ref[...]
    sin = sin_ref[...]
    bd = bd_ref[...]

    def pairs(lo, n):
        return [y[:, lo + PAIR * p: lo + PAIR * (p + 1)] for p in range(n)]

    def put(ref, parts, scale=None):
        for p, part in enumerate(parts):
            if scale is not None:
                part = part * scale
            ref[:, PAIR * p: PAIR * (p + 1)] = part.astype(ref.dtype)

    put(aq_ref, [_rope_pair(t, cos, sin) for t in pairs(_P_AQ, 2)], Q_SCALE)
    put(ak_ref, [_rope_pair(t, cos, sin) for t in pairs(_P_AK, 1)])
    put(av_ref, pairs(_P_AV, 1))
    put(z_ref, pairs(_P_Z, 2))
    put(xbc_ref, pairs(_P_XBC, 6))
    put(dt_ref, pairs(_P_DT, 1))
    put(cq_ref, [_rope_pair(_rms_pair(t, qg_ref[...], bd), cos, sin) for t in pairs(_P_CQ, 2)], Q_SCALE)
    put(ck_ref, [_rope_pair(_rms_pair(t, kg_ref[...], bd), cos, sin) for t in pairs(_P_CK, 1)])
    put(cv_ref, pairs(_P_CV, 1))
    put(dq_ref, pairs(_P_DQ, 2), Q_SCALE)
    put(dk_ref, pairs(_P_DK, 2))
    put(dv_ref, pairs(_P_DV, 2))


def _inproj(xt, mod, w, cos, sin, qg, kg, blockdiag, *, tokens_per_mod, tm):
    t, d = xt.shape
    tiles_per_mod = tokens_per_mod // tm
    table_tiles = cos.shape[0] // tm
    widths = (256, 128, 128, 256, SSD_CONV_CH, DT_PAD, 256, 128, 128, 256, 256, 256)
    dtypes = (BF16,) * 5 + (F32,) + (BF16,) * 6
    row = lambda i: (i, 0)
    const = lambda i: (0, 0)
    return pl.pallas_call(
        _inproj_kernel,
        grid=(t // tm,),
        in_specs=[
            pl.BlockSpec((tm, d), row),
            pl.BlockSpec((1, N_MOD, d), lambda i: (i // tiles_per_mod, 0, 0)),
            pl.BlockSpec((d, _P_END), const),
            pl.BlockSpec((tm, PAIR), lambda i: (i % table_tiles, 0)),
            pl.BlockSpec((tm, PAIR), lambda i: (i % table_tiles, 0)),
            pl.BlockSpec((1, PAIR), const),
            pl.BlockSpec((1, PAIR), const),
            pl.BlockSpec((PAIR, PAIR), const),
        ],
        out_specs=[pl.BlockSpec((tm, wd), row) for wd in widths],
        out_shape=[jax.ShapeDtypeStruct((t, wd), dt) for wd, dt in zip(widths, dtypes)],
        compiler_params=_cparams("arbitrary"),
        name="inproj",
    )(xt, mod, w, cos, sin, qg, kg, blockdiag)


def _stack_heads(q_pairs):
    lane = lax.broadcasted_iota(jnp.int32, q_pairs[0].shape, 1)
    lo = lane < HEAD_DIM
    zero = jnp.zeros_like(q_pairs[0])
    blocks = []
    for qp in q_pairs:
        blocks.append(jnp.where(lo, qp, zero))
        blocks.append(jnp.where(lo, zero, qp))
    return jnp.concatenate(blocks, axis=0)


def _unstack_heads(o, n_pairs, tq):
    lane = lax.broadcasted_iota(jnp.int32, (tq, PAIR), 1)
    lo = lane < HEAD_DIM
    return [jnp.where(lo, o[2 * p * tq:(2 * p + 1) * tq], o[(2 * p + 1) * tq:(2 * p + 2) * tq])
            for p in range(n_pairs)]


def _sink_column(sink_ref, tq):
    blk = lax.broadcasted_iota(jnp.int32, (4 * tq, 1), 0) // tq
    col = jnp.full((4 * tq, 1), sink_ref[HEAD_PERM[3]], F32)
    for b in (2, 1, 0):
        col = jnp.where(blk == b, sink_ref[HEAD_PERM[b]], col)
    return col


def _swa_kernel(sink_ref, q_ref, k_ref, v_ref, kc_ref, vc_ref, o_ref, *, seq):
    n = pl.program_id(1)
    blk = SWA_BLOCK
    span = 3 * blk
    start = pl.multiple_of(jnp.clip((n - 1) * blk, 0, seq - span), blk)
    kl = k_ref[0, pl.ds(start, span), :]
    vl = v_ref[0, pl.ds(start, span), :]
    q = q_ref[0]
    qs = _stack_heads([q[:, :PAIR], q[:, PAIR:]])
    s_loc = _dot_nt(qs, kl)
    s_ctx = _dot_nt(qs, kc_ref[0])
    qpos = n * blk + lax.broadcasted_iota(jnp.int32, s_loc.shape, 0) % blk
    kpos = start + lax.broadcasted_iota(jnp.int32, s_loc.shape, 1)
    s_loc = jnp.where(jnp.abs(kpos - qpos) <= SWA_WINDOW, s_loc, NEG)
    sk = _sink_column(sink_ref, blk)
    m = jnp.maximum(jnp.maximum(jnp.max(s_loc, axis=-1, keepdims=True),
                                jnp.max(s_ctx, axis=-1, keepdims=True)), sk)
    p_loc = jnp.exp(s_loc - m)
    p_ctx = jnp.exp(s_ctx - m)
    denom = (jnp.sum(p_loc, axis=-1, keepdims=True) + jnp.sum(p_ctx, axis=-1, keepdims=True)
             + jnp.exp(sk - m))
    o = _dot(p_loc.astype(BF16), vl) + _dot(p_ctx.astype(BF16), vc_ref[0])
    o = o * (1.0 / denom)
    o_a, o_b = _unstack_heads(o, 2, blk)
    o_ref[0, :, :PAIR] = o_a.astype(o_ref.dtype)
    o_ref[0, :, PAIR:] = o_b.astype(o_ref.dtype)


def _swa(sink, q, k, v, kc, vc):
    b, s, _ = q.shape
    m = kc.shape[1]
    blk = SWA_BLOCK
    assert s % blk == 0 and s >= 3 * blk
    return pl.pallas_call(
        functools.partial(_swa_kernel, seq=s),
        grid=(b, s // blk),
        in_specs=[
            pl.BlockSpec(memory_space=pltpu.SMEM),
            pl.BlockSpec((1, blk, GROUP_WIDTH), lambda i, n: (i, n, 0)),
            pl.BlockSpec((1, s, PAIR), lambda i, n: (i, 0, 0)),
            pl.BlockSpec((1, s, PAIR), lambda i, n: (i, 0, 0)),
            pl.BlockSpec((1, m, PAIR), lambda i, n: (i, 0, 0)),
            pl.BlockSpec((1, m, PAIR), lambda i, n: (i, 0, 0)),
        ],
        out_specs=pl.BlockSpec((1, blk, GROUP_WIDTH), lambda i, n: (i, n, 0)),
        out_shape=jax.ShapeDtypeStruct((b, s, GROUP_WIDTH), BF16),
        compiler_params=_cparams("arbitrary", "arbitrary"),
        name="swa_attn",
    )(sink, q, k, v, kc, vc)


def _dense_attn_kernel(*refs, kv_pairs, n_src, has_sink, tq, tk):
    refs = list(refs)
    sink_ref = refs.pop(0) if has_sink else None
    q_ref = refs.pop(0)
    srcs = [(refs[2 * i], refs[2 * i + 1]) for i in range(n_src)]
    o_ref = refs[2 * n_src]
    q = q_ref[0]
    if kv_pairs == 1:
        units = [([q[:, :PAIR], q[:, PAIR:]], 0)]
    else:
        units = [([q[:, :PAIR]], 0), ([q[:, PAIR:]], 1)]
    outs = []
    for q_pairs, kv in units:
        qs = _stack_heads(q_pairs)
        nrow = qs.shape[0]
        if has_sink:
            m = _sink_column(sink_ref, tq)
            l = jnp.ones((nrow, 1), F32)
        else:
            m = jnp.full((nrow, 1), NEG, F32)
            l = jnp.zeros((nrow, 1), F32)
        acc = jnp.zeros((nrow, PAIR), F32)
        for k_ref, v_ref in srcs:
            nk = k_ref.shape[1]
            step = min(tk, nk)
            for c in range(nk // step):
                kch = k_ref[0, c * step:(c + 1) * step, kv * PAIR:(kv + 1) * PAIR]
                vch = v_ref[0, c * step:(c + 1) * step, kv * PAIR:(kv + 1) * PAIR]
                s = _dot_nt(qs, kch)
                m_new = jnp.maximum(m, jnp.max(s, axis=-1, keepdims=True))
                a = jnp.exp(m - m_new)
                p = jnp.exp(s - m_new)
                l = a * l + jnp.sum(p, axis=-1, keepdims=True)
                acc = a * acc + _dot(p.astype(BF16), vch)
                m = m_new
        o = acc * (1.0 / l)
        outs += _unstack_heads(o, len(q_pairs), tq)
    o_ref[0, :, :PAIR] = outs[0].astype(o_ref.dtype)
    o_ref[0, :, PAIR:] = outs[1].astype(o_ref.dtype)


def _dense_attn(q, srcs, *, kv_pairs, sink=None, tq=128, tk=512):
    b, s, _ = q.shape
    kvw = kv_pairs * PAIR
    has_sink = sink is not None
    in_specs, args = [], []
    if has_sink:
        in_specs.append(pl.BlockSpec(memory_space=pltpu.SMEM))
        args.append(sink)
    in_specs.append(pl.BlockSpec((1, tq, GROUP_WIDTH), lambda i, n: (i, n, 0)))
    args.append(q)
    for k, v in srcs:
        nk = k.shape[1]
        assert nk % min(tk, nk) == 0
        in_specs += [pl.BlockSpec((1, nk, kvw), lambda i, n: (i, 0, 0))] * 2
        args += [k, v]
    return pl.pallas_call(
        functools.partial(_dense_attn_kernel, kv_pairs=kv_pairs, n_src=len(srcs),
                          has_sink=has_sink, tq=tq, tk=tk),
        grid=(b, s // tq),
        in_specs=in_specs,
        out_specs=pl.BlockSpec((1, tq, GROUP_WIDTH), lambda i, n: (i, n, 0)),
        out_shape=jax.ShapeDtypeStruct((b, s, GROUP_WIDTH), BF16),
        compiler_params=_cparams("arbitrary", "arbitrary"),
        name="dense_attn",
    )(*args)


def _na_bias_tables(rpb, rows):
    w = GRID_W
    groups = rows // NA_QROWS
    assert rows % NA_QROWS == 0 and groups >= 4
    tables = []
    for r0, start in ((0, 0), (NA_QROWS, 0), (NA_QROWS * (groups - 1), rows - NA_KROWS)):
        r = r0 + np.arange(NA_QROWS)[:, None, None, None]
        c = np.arange(w)[None, :, None, None]
        krow = start + np.arange(NA_KROWS)[None, None, :, None]
        kcol = np.arange(w)[None, None, None, :]
        rs = np.clip(r - NA_ROWS // 2, 0, rows - NA_ROWS)
        cs = np.clip(c - NA_COLS // 2, 0, w - NA_COLS)
        valid = (krow >= rs) & (krow < rs + NA_ROWS) & (kcol >= cs) & (kcol < cs + NA_COLS)
        dr = np.clip(krow - r + NA_ROWS - 1, 0, 2 * NA_ROWS - 2)
        dc = np.clip(kcol - c + NA_COLS - 1, 0, 2 * NA_COLS - 2)
        shape = (NA_QROWS * w, NA_KROWS * w)
        valid = np.broadcast_to(valid, (NA_QROWS, w, NA_KROWS, w)).reshape(shape)
        dr = np.broadcast_to(dr, (NA_QROWS, w, NA_KROWS, w)).reshape(shape)
        dc = np.broadcast_to(dc, (NA_QROWS, w, NA_KROWS, w)).reshape(shape)
        tables.append(jnp.where(valid[None], rpb.astype(F32)[:, dr, dc], NEG))
    t = jnp.stack(tables)
    return t.reshape(3, 2, 2 * NA_QROWS * w, NA_KROWS * w)


def _na_kernel(q_ref, k_ref, v_ref, kc_ref, vc_ref, bias_ref, o_ref, *, rows):
    g = pl.program_id(1)
    tq = NA_QROWS * GRID_W
    span = NA_KROWS * GRID_W
    start = pl.multiple_of(jnp.clip(NA_QROWS * g - NA_ROWS // 2, 0, rows - NA_KROWS) * GRID_W, GRID_W)
    q = q_ref[0]
    for p in range(2):
        lanes = slice(p * PAIR, (p + 1) * PAIR)
        qs = _stack_heads([q[:, lanes]])
        kl = k_ref[0, pl.ds(start, span), lanes]
        vl = v_ref[0, pl.ds(start, span), lanes]
        s_nb = _dot_nt(qs, kl) + bias_ref[0, p]
        s_ctx = _dot_nt(qs, kc_ref[0, :, lanes])
        m = jnp.maximum(jnp.max(s_nb, axis=-1, keepdims=True), jnp.max(s_ctx, axis=-1, keepdims=True))
        p_nb = jnp.exp(s_nb - m)
        p_ctx = jnp.exp(s_ctx - m)
        denom = jnp.sum(p_nb, axis=-1, keepdims=True) + jnp.sum(p_ctx, axis=-1, keepdims=True)
        o = _dot(p_nb.astype(BF16), vl) + _dot(p_ctx.astype(BF16), vc_ref[0, :, lanes])
        o = o * (1.0 / denom)
        o_ref[0, :, lanes] = _unstack_heads(o, 1, tq)[0].astype(o_ref.dtype)


def _na(q, k, v, kc, vc, bias):
    b, s, _ = q.shape
    m = kc.shape[1]
    rows = s // GRID_W
    groups = rows // NA_QROWS
    tq = NA_QROWS * GRID_W

    def pattern(i, g):
        return (jnp.where(g == 0, 0, jnp.where(g == groups - 1, 2, 1)), 0, 0, 0)

    return pl.pallas_call(
        functools.partial(_na_kernel, rows=rows),
        grid=(b, groups),
        in_specs=[
            pl.BlockSpec((1, tq, GROUP_WIDTH), lambda i, g: (i, g, 0)),
            pl.BlockSpec((1, s, GROUP_WIDTH), lambda i, g: (i, 0, 0)),
            pl.BlockSpec((1, s, GROUP_WIDTH), lambda i, g: (i, 0, 0)),
            pl.BlockSpec((1, m, GROUP_WIDTH), lambda i, g: (i, 0, 0)),
            pl.BlockSpec((1, m, GROUP_WIDTH), lambda i, g: (i, 0, 0)),
            pl.BlockSpec((1, 2, 2 * tq, NA_KROWS * GRID_W), pattern),
        ],
        out_specs=pl.BlockSpec((1, tq, GROUP_WIDTH), lambda i, g: (i, g, 0)),
        out_shape=jax.ShapeDtypeStruct((b, s, GROUP_WIDTH), BF16),
        compiler_params=_cparams("arbitrary", "arbitrary"),
        name="na_attn",
    )(q, k, v, kc, vc, bias)


def _ssd_kernel(xf_ref, xfp_ref, xfn_ref, xb_ref, xbp_ref, xbn_ref, dtf_ref, dtb_ref,
                cw_ref, cb_ref, dtbias_ref, alog_ref, dskip_ref, h0f_ref, h0b_ref,
                yf_ref, yb_ref, hf_ref, hb_ref, hf_sc, hb_sc, *, nc):
    i = pl.program_id(1)
    q = SSD_CHUNK

    @pl.when(i == 0)
    def _():
        hf_sc[...] = h0f_ref[0]
        hb_sc[...] = h0b_ref[0]

    def conv(main_ref, prev_ref, next_ref, first, last):
        xm = main_ref[0].astype(F32)
        xp = jnp.where(first, 0.0, prev_ref[0].astype(F32))
        xn = jnp.where(last, 0.0, next_ref[0].astype(F32))
        ext = jnp.concatenate([xp, xm, xn], axis=0)
        acc = cb_ref[...]
        for k in range(SSD_CONV):
            lo = SSD_HALO - SSD_CONV // 2 + k
            acc = acc + cw_ref[k:k + 1, :] * ext[lo:lo + q, :]
        return _silu(acc)

    ii = lax.broadcasted_iota(jnp.int32, (q, q), 0)
    jj = lax.broadcasted_iota(jnp.int32, (q, q), 1)
    lo_lanes = jj < HEAD_DIM
    a_coef = -jnp.exp(alog_ref[...])

    def chunk(u, dt_raw, col0, reverse, h_sc):
        causal = (jj >= ii) if reverse else (jj <= ii)
        dt = _softplus(dt_raw + dtbias_ref[...])
        tri = jnp.where(causal, 1.0, 0.0).astype(BF16)
        a_hi, a_mid, a_lo = _split3(dt * a_coef)
        cum = _dot(tri, a_hi) + _dot(tri, a_mid) + _dot(tri, a_lo)
        cum_t = cum.T
        dt_t = dt.T
        end = cum[0:1, :] if reverse else cum[q - 1:q, :]
        xs = u[:, :2 * PAIR]
        ys = []
        for k in range(2):
            bk = u[:, 2 * PAIR + k * SSD_STATE: 2 * PAIR + (k + 1) * SSD_STATE]
            ck = u[:, 2 * PAIR + 2 * SSD_STATE + k * SSD_STATE: 2 * PAIR + 2 * SSD_STATE + (k + 1) * SSD_STATE]
            xk = xs[:, k * PAIR:(k + 1) * PAIR]
            ck_b = ck.astype(BF16)
            xk_b = xk.astype(BF16)
            cb = _dot_nt(ck_b, bk.astype(BF16))
            c0 = col0 + 2 * k
            cols = []
            for r in range(2):
                c = c0 + r
                seg = cum[:, c:c + 1] - cum_t[c:c + 1, :]
                lmat = jnp.exp(jnp.where(causal, seg, NEG))
                att = (cb * lmat * dt_t[c:c + 1, :]).astype(BF16)
                cols.append(_dot(att, xk_b))
            y_intra = jnp.where(lo_lanes, cols[0], cols[1])
            e_in = jnp.where(lo_lanes, jnp.exp(cum[:, c0:c0 + 1]), jnp.exp(cum[:, c0 + 1:c0 + 2]))
            h_t = h_sc[k]
            y_state = _dot(ck_b, h_t.astype(BF16)) * e_in
            w0 = jnp.exp(end[:, c0:c0 + 1] - cum[:, c0:c0 + 1]) * dt[:, c0:c0 + 1]
            w1 = jnp.exp(end[:, c0 + 1:c0 + 2] - cum[:, c0 + 1:c0 + 2]) * dt[:, c0 + 1:c0 + 2]
            xw = (xk * jnp.where(lo_lanes, w0, w1)).astype(BF16)
            st = _dot(bk.T.astype(BF16), xw)
            decay = jnp.where(lo_lanes[0:1], jnp.exp(end[:, c0:c0 + 1]), jnp.exp(end[:, c0 + 1:c0 + 2]))
            h_sc[k] = h_t * decay + st
            ys.append(y_intra + y_state)
        return ys, xs

    uf = conv(xf_ref, xfp_ref, xfn_ref, i == 0, i == nc - 1)
    ys, xs = chunk(uf, dtf_ref[0], 0, False, hf_sc)
    dskip = dskip_ref[...]
    for k in range(2):
        lanes = slice(k * PAIR, (k + 1) * PAIR)
        yf_ref[0, :, lanes] = ys[k] + dskip[:, lanes] * xs[:, lanes]

    ub = conv(xb_ref, xbp_ref, xbn_ref, i == nc - 1, i == 0)
    ys, _ = chunk(ub, dtb_ref[0], 4, True, hb_sc)
    for k in range(2):
        yb_ref[0, :, k * PAIR:(k + 1) * PAIR] = ys[k]

    @pl.when(i == nc - 1)
    def _():
        hf_ref[0] = hf_sc[...]
        hb_ref[0] = hb_sc[...]


def _ssd(xbc, dt, conv_w, conv_b, dt_bias, a_log, dskip, h0f, h0b):
    b, length, ch = xbc.shape
    q = SSD_CHUNK
    nc = length // q
    per = q // SSD_HALO
    last_halo = length // SSD_HALO - 1
    fwd = lambda i, c: (i, c, 0)
    bwd = lambda i, c: (i, nc - 1 - c, 0)
    fwd_prev = lambda i, c: (i, jnp.maximum(c * per - 1, 0), 0)
    fwd_next = lambda i, c: (i, jnp.minimum((c + 1) * per, last_halo), 0)
    bwd_prev = lambda i, c: (i, jnp.maximum((nc - 1 - c) * per - 1, 0), 0)
    bwd_next = lambda i, c: (i, jnp.minimum((nc - c) * per, last_halo), 0)
    const = lambda i, c: (0, 0)
    state = lambda i, c: (i, 0, 0, 0)
    state_shape = (b, 2, SSD_STATE, PAIR)
    return pl.pallas_call(
        functools.partial(_ssd_kernel, nc=nc),
        grid=(b, nc),
        in_specs=[
            pl.BlockSpec((1, q, ch), fwd),
            pl.BlockSpec((1, SSD_HALO, ch), fwd_prev),
            pl.BlockSpec((1, SSD_HALO, ch), fwd_next),
            pl.BlockSpec((1, q, ch), bwd),
            pl.BlockSpec((1, SSD_HALO, ch), bwd_prev),
            pl.BlockSpec((1, SSD_HALO, ch), bwd_next),
            pl.BlockSpec((1, q, DT_PAD), fwd),
            pl.BlockSpec((1, q, DT_PAD), bwd),
            pl.BlockSpec((8, ch), const),
            pl.BlockSpec((1, ch), const),
            pl.BlockSpec((1, DT_PAD), const),
            pl.BlockSpec((1, DT_PAD), const),
            pl.BlockSpec((1, GROUP_WIDTH), const),
            pl.BlockSpec((1, 2, SSD_STATE, PAIR), state),
            pl.BlockSpec((1, 2, SSD_STATE, PAIR), state),
        ],
        out_specs=[
            pl.BlockSpec((1, q, GROUP_WIDTH), fwd),
            pl.BlockSpec((1, q, GROUP_WIDTH), bwd),
            pl.BlockSpec((1, 2, SSD_STATE, PAIR), state),
            pl.BlockSpec((1, 2, SSD_STATE, PAIR), state),
        ],
        out_shape=[
            jax.ShapeDtypeStruct((b, length, GROUP_WIDTH), F32),
            jax.ShapeDtypeStruct((b, length, GROUP_WIDTH), F32),
            jax.ShapeDtypeStruct(state_shape, F32),
            jax.ShapeDtypeStruct(state_shape, F32),
        ],
        scratch_shapes=[pltpu.VMEM((2, SSD_STATE, PAIR), F32), pltpu.VMEM((2, SSD_STATE, PAIR), F32)],
        compiler_params=_cparams("arbitrary", "arbitrary"),
        name="ssd_scan",
    )(xbc, xbc, xbc, xbc, xbc, xbc, dt, dt, conv_w, conv_b, dt_bias, a_log, dskip, h0f, h0b)


def _merge_kernel(x_ref, mod_ref, oa_ref, yf_ref, yb_ref, z_ref, oc_ref, od_ref, ng_ref, w_ref,
                  g_ref, b_ref, o_ref, *, alpha):
    ob = (yf_ref[...] + yb_ref[...]) * _silu(z_ref[...].astype(F32))
    parts = (oa_ref[...].astype(F32), ob, oc_ref[...].astype(F32), od_ref[...].astype(F32))
    acc = None
    for gi, y in enumerate(parts):
        rows = slice(gi * GROUP_WIDTH, (gi + 1) * GROUP_WIDTH)
        ms = jnp.mean(y * y, axis=-1, keepdims=True)
        yn = (y * lax.rsqrt(ms + EPS) * ng_ref[:, rows]).astype(BF16)
        term = _dot(yn, w_ref[rows, :])
        acc = term if acc is None else acc + term
    y = alpha * x_ref[...] + mod_ref[0, 5:6, :] * acc
    o_ref[...] = _layer_norm(y, g_ref[...], b_ref[...])


def _merge(xt, mod, oa, yf, yb, z, oc, od, norm_g, w_out, ln_g, ln_b, *, tokens_per_mod, alpha, tm):
    t, d = xt.shape
    tiles_per_mod = tokens_per_mod // tm
    row = lambda i: (i, 0)
    const = lambda i: (0, 0)
    grp = pl.BlockSpec((tm, GROUP_WIDTH), row)
    return pl.pallas_call(
        functools.partial(_merge_kernel, alpha=alpha),
        grid=(t // tm,),
        in_specs=[
            pl.BlockSpec((tm, d), row),
            pl.BlockSpec((1, N_MOD, d), lambda i: (i // tiles_per_mod, 0, 0)),
            grp, grp, grp, grp, grp, grp,
            pl.BlockSpec((1, d), const),
            pl.BlockSpec((d, d), const),
            pl.BlockSpec((1, d), const),
            pl.BlockSpec((1, d), const),
        ],
        out_specs=pl.BlockSpec((tm, d), row),
        out_shape=jax.ShapeDtypeStruct((t, d), F32),
        compiler_params=_cparams("arbitrary"),
        name="merge_out",
    )(xt, mod, oa, yf, yb, z, oc, od, norm_g, w_out, ln_g.reshape(1, d), ln_b.reshape(1, d))


def _head_block_index(start, perm):
    return np.concatenate([np.arange(start + h * HEAD_DIM, start + (h + 1) * HEAD_DIM) for h in perm])


def _inproj_columns():
    cols = [
        _head_block_index(0, HEAD_PERM), np.arange(256, 1536),
        _head_block_index(1544, HEAD_PERM), np.arange(1800, 2824),
    ]
    return np.concatenate(cols), np.arange(1536, 1544)


def _outproj_rows():
    return np.concatenate([
        _head_block_index(0, HEAD_PERM), np.arange(256, 512),
        _head_block_index(512, HEAD_PERM), np.arange(768, 1024),
    ])


def _rope_tables(seq):
    t = np.arange(seq)
    quarter = HEAD_DIM // 4
    inv = ROPE_BASE ** (-jnp.arange(quarter, dtype=F32) / quarter)
    a_row = jnp.asarray(t // GRID_W, F32)[:, None] * inv
    a_col = jnp.asarray(t % GRID_W, F32)[:, None] * inv
    cos = jnp.concatenate([jnp.cos(a_row), jnp.cos(a_row), jnp.cos(a_col), jnp.cos(a_col)], axis=1)
    sin = jnp.concatenate([-jnp.sin(a_row), jnp.sin(a_row), -jnp.sin(a_col), jnp.sin(a_col)], axis=1)
    return jnp.tile(cos, (1, 2)), jnp.tile(sin, (1, 2))


def _pad_lanes(v, width):
    v = v.reshape(1, -1).astype(F32)
    return jnp.pad(v, ((0, 0), (0, width - v.shape[1])))


def _token_tile(n, cap):
    t = cap
    while n % t:
        t //= 2
    return t


def kernel(x, c, ctx, c_ctx, ada_w, ada_b, ln_g, ln_b, ffn1_w_in, ffn1_w_out, mix_w_in, mix_w_out,
           mix_norm_g, swa_sink, ssd_conv_w, ssd_conv_b, ssd_dt_bias, ssd_A_log, ssd_D,
           gqa_q_norm, gqa_k_norm, na_rpb, ffn2_w_in, ffn2_w_out):
    bsz, seq, d = x.shape
    m_ctx = ctx.shape[1]
    depth = ada_w.shape[0]
    alpha = float((2 * depth) ** 0.25)
    dff = ffn1_w_out.shape[1]
    tf = dff // 2 if (dff // 2) % 128 == 0 else dff

    mod_rows = 16 * ((bsz + 1 + 15) // 16)
    cc = jnp.concatenate([c, c_ctx[None], jnp.zeros((mod_rows - bsz - 1, d), F32)], axis=0)
    mods = _ada(cc, ada_w, ada_b)

    cos_x, sin_x = _rope_tables(seq)
    tm_x = _token_tile(seq, 512)
    tm_c = _token_tile(m_ctx, 256)
    cos_c = jnp.ones((tm_c, PAIR), F32)
    sin_c = jnp.zeros((tm_c, PAIR), F32)
    blockdiag = jnp.asarray(np.kron(np.eye(2), np.ones((HEAD_DIM, HEAD_DIM))), BF16)
    main_cols, dt_cols = _inproj_columns()
    out_rows = _outproj_rows()
    na_rows = seq // GRID_W

    xt = x.reshape(bsz * seq, d)
    ct = ctx.reshape(bsz * m_ctx, d)
    zero_state = jnp.zeros((bsz, 2, SSD_STATE, PAIR), F32)

    for l in range(depth):
        last = l == depth - 1
        mod_x = mods[l, :bsz].reshape(bsz, N_MOD, d)
        mod_c = mods[l, bsz:bsz + 1].reshape(1, N_MOD, d)
        w1_in, w1_out = ffn1_w_in[l].astype(BF16), ffn1_w_out[l].astype(BF16)
        w2_in, w2_out = ffn2_w_in[l].astype(BF16), ffn2_w_out[l].astype(BF16)
        w_mix = mix_w_in[l]
        w_mix = jnp.concatenate(
            [w_mix[:, main_cols], jnp.pad(w_mix[:, dt_cols], ((0, 0), (0, DT_PAD - dt_cols.size)))],
            axis=1).astype(BF16)
        w_o = mix_w_out[l][out_rows].astype(BF16)
        norm_g = mix_norm_g[l][out_rows].reshape(1, d)
        qg = jnp.tile(gqa_q_norm[l], 2).reshape(1, PAIR)
        kg = jnp.tile(gqa_k_norm[l], 2).reshape(1, PAIR)
        conv_w = jnp.pad(ssd_conv_w[l].reshape(SSD_CONV, SSD_CONV_CH), ((0, 8 - SSD_CONV), (0, 0)))
        conv_b = ssd_conv_b[l].reshape(1, SSD_CONV_CH)
        dt_bias = _pad_lanes(ssd_dt_bias[l], DT_PAD)
        a_log = _pad_lanes(ssd_A_log[l], DT_PAD)
        dskip = jnp.repeat(ssd_D[l], HEAD_DIM).reshape(1, GROUP_WIDTH)
        sink = swa_sink[l].astype(F32)
        na_bias = _na_bias_tables(na_rpb[l], na_rows)

        ffn_x = functools.partial(_ffn, tokens_per_mod=seq, alpha=alpha, tm=tm_x, tf=tf)
        ffn_c = functools.partial(_ffn, tokens_per_mod=bsz * m_ctx, alpha=alpha, tm=tm_c, tf=tf)

        xt = ffn_x(xt, mod_x, (0, 1, 2), w1_in, w1_out, ln_g[l, 0], ln_b[l, 0])
        ct = ffn_c(ct, mod_c, (0, 1, 2), w1_in, w1_out, ln_g[l, 0], ln_b[l, 0])

        px = _inproj(xt, mod_x, w_mix, cos_x, sin_x, qg, kg, blockdiag, tokens_per_mod=seq, tm=tm_x)
        pc = _inproj(ct, mod_c, w_mix, cos_c, sin_c, qg, kg, blockdiag,
                     tokens_per_mod=bsz * m_ctx, tm=tm_c)
        aq, ak, av, bz, bxbc, bdt, cq, ck, cv, dq, dk, dv = [
            t.reshape(bsz, seq, t.shape[-1]) for t in px]
        aq_c, ak_c, av_c, bz_c, bxbc_c, bdt_c, cq_c, ck_c, cv_c, dq_c, dk_c, dv_c = [
            t.reshape(bsz, m_ctx, t.shape[-1]) for t in pc]

        ssd = functools.partial(_ssd, conv_w=conv_w, conv_b=conv_b, dt_bias=dt_bias, a_log=a_log,
                                dskip=dskip)
        yf_c, yb_c, hf_c, hb_c = ssd(bxbc_c, bdt_c, h0f=zero_state, h0b=zero_state)
        yf, yb, _, _ = ssd(bxbc, bdt, h0f=hf_c, h0b=hb_c)

        oa = _swa(sink, aq, ak, av, ak_c, av_c)
        oc = _dense_attn(cq, [(ck, cv), (ck_c, cv_c)], kv_pairs=1)
        od = _na(dq, dk, dv, dk_c, dv_c, na_bias)

        flat = lambda t: t.reshape(-1, t.shape[-1])
        xt = _merge(xt, mod_x, flat(oa), flat(yf), flat(yb), flat(bz), flat(oc), flat(od), norm_g, w_o,
                    ln_g[l, 1], ln_b[l, 1], tokens_per_mod=seq, alpha=alpha, tm=tm_x)
        xt = ffn_x(xt, mod_x, (6, 7, 8), w2_in, w2_out, ln_g[l, 2], ln_b[l, 2])

        if not last:
            tq_c = _token_tile(m_ctx, 128)
            oa_c = _dense_attn(aq_c, [(ak_c, av_c)], kv_pairs=1, sink=sink, tq=tq_c)
            oc_c = _dense_attn(cq_c, [(ck_c, cv_c)], kv_pairs=1, tq=tq_c)
            od_c = _dense_attn(dq_c, [(dk_c, dv_c)], kv_pairs=2, tq=tq_c)
            ct = _merge(ct, mod_c, flat(oa_c), flat(yf_c), flat(yb_c), flat(bz_c), flat(oc_c), flat(od_c),
                        norm_g, w_o, ln_g[l, 1], ln_b[l, 1], tokens_per_mod=bsz * m_ctx, alpha=alpha,
                        tm=tm_c)
            ct = ffn_c(ct, mod_c, (6, 7, 8), w2_in, w2_out, ln_g[l, 2], ln_b[l, 2])

    return xt.reshape(bsz, seq, d)
```

```python
import functools

import numpy as np
import jax
import jax.numpy as jnp
from jax import lax
from jax.experimental import pallas as pl
from jax.experimental.pallas import tpu as pltpu

F32 = jnp.float32
BF16 = jnp.bfloat16

HEAD_DIM = 64
PAIR = 2 * HEAD_DIM
GROUP_WIDTH = 256
GRID_W = 64
N_MOD = 9
SWA_WINDOW = 128
SWA_BLOCK = 128
SSD_CHUNK = 128
SSD_STATE = 128
SSD_CONV = 5
SSD_HALO = 16
SSD_CONV_CH = 768
NA_ROWS = 8
NA_COLS = 16
NA_QROWS = 4
NA_KROWS = 12
ROPE_BASE = 10000.0
EPS = 1e-5
NEG = -1e30
Q_SCALE = HEAD_DIM ** -0.5
LOG2E = 1.4426950408889634
DT_PAD = 128
HEAD_PERM = (0, 2, 1, 3)

VMEM_LIMIT_BYTES = 56 * 1024 * 1024


def _cparams(*sem):
    return pltpu.CompilerParams(dimension_semantics=sem, vmem_limit_bytes=VMEM_LIMIT_BYTES)


def _dot(a, b):
    return jnp.dot(a, b, preferred_element_type=F32)


def _dot_nt(a, b):
    return lax.dot_general(a, b, (((1,), (1,)), ((), ())), preferred_element_type=F32)


def _sigmoid(x):
    return 1.0 / (1.0 + jnp.exp(-x))


def _silu(x):
    return x * _sigmoid(x)


def _softplus(x):
    return jnp.maximum(x, 0.0) + jnp.log(1.0 + jnp.exp(-jnp.abs(x)))


def _split3(a):
    hi = a.astype(BF16)
    r1 = a - hi.astype(F32)
    mid = r1.astype(BF16)
    lo = (r1 - mid.astype(F32)).astype(BF16)
    return hi, mid, lo


def _layer_norm(y, g, b):
    mu = jnp.mean(y, axis=-1, keepdims=True)
    d = y - mu
    var = jnp.mean(d * d, axis=-1, keepdims=True)
    return d * lax.rsqrt(var + EPS) * g + b


def _ada_kernel(c_ref, w_ref, b_ref, o_ref):
    s = _silu(c_ref[...])
    s_hi = s.astype(BF16)
    s_lo = (s - s_hi.astype(F32)).astype(BF16)
    w = w_ref[0]
    w_hi = w.astype(BF16)
    w_lo = (w - w_hi.astype(F32)).astype(BF16)
    o_ref[0] = _dot(s_hi, w_hi) + _dot(s_lo, w_hi) + _dot(s_hi, w_lo) + b_ref[0]


def _ada(cc, ada_w, ada_b):
    depth, d, n = ada_w.shape
    rows = cc.shape[0]
    tn = 1024
    return pl.pallas_call(
        _ada_kernel,
        grid=(depth, n // tn),
        in_specs=[
            pl.BlockSpec((rows, d), lambda l, j: (0, 0)),
            pl.BlockSpec((1, d, tn), lambda l, j: (l, 0, j)),
            pl.BlockSpec((1, 1, tn), lambda l, j: (l, 0, j)),
        ],
        out_specs=pl.BlockSpec((1, rows, tn), lambda l, j: (l, 0, j)),
        out_shape=jax.ShapeDtypeStruct((depth, rows, n), F32),
        compiler_params=_cparams("arbitrary", "arbitrary"),
        name="ada_mod",
    )(cc, ada_w, ada_b.reshape(depth, 1, n))


def _ffn_kernel(x_ref, mod_ref, wa_ref, wu_ref, wo_ref, g_ref, b_ref, o_ref, *, rows, alpha):
    r_shift, r_scale, r_gate = rows
    x = x_ref[...]
    h = x * (1.0 + mod_ref[0, r_scale:r_scale + 1, :]) + mod_ref[0, r_shift:r_shift + 1, :]
    h = h.astype(BF16)
    a = _dot(h, wa_ref[...])
    u = _dot(h, wu_ref[...])
    gated = (_silu(a) * u).astype(BF16)
    f = _dot(gated, wo_ref[...])
    y = alpha * x + (0.5 * mod_ref[0, r_gate:r_gate + 1, :]) * f
    o_ref[...] = _layer_norm(y, g_ref[...], b_ref[...])


def _resident(shape, index_map):
    return pl.BlockSpec(shape, index_map, pipeline_mode=pl.Buffered(1))


def _ffn(xt, mod, rows, w_in, w_out, ln_g, ln_b, *, tokens_per_mod, alpha, tm):
    t, d = xt.shape
    dff = w_out.shape[0]
    tiles_per_mod = tokens_per_mod // tm
    return pl.pallas_call(
        functools.partial(_ffn_kernel, rows=rows, alpha=alpha),
        grid=(t // tm,),
        in_specs=[
            pl.BlockSpec((tm, d), lambda i: (i, 0)),
            pl.BlockSpec((1, N_MOD, d), lambda i: (i // tiles_per_mod, 0, 0)),
            _resident((d, dff), lambda i: (0, 0)),
            _resident((d, dff), lambda i: (0, 1)),
            _resident((dff, d), lambda i: (0, 0)),
            pl.BlockSpec((1, d), lambda i: (0, 0)),
            pl.BlockSpec((1, d), lambda i: (0, 0)),
        ],
        out_specs=pl.BlockSpec((tm, d), lambda i: (i, 0)),
        out_shape=jax.ShapeDtypeStruct((t, d), F32),
        compiler_params=_cparams("arbitrary"),
        name="ffn",
    )(xt, mod, w_in, w_in, w_out, ln_g.reshape(1, d), ln_b.reshape(1, d))


_P_AQ, _P_AK, _P_AV, _P_Z, _P_XBC = 0, 256, 384, 512, 768
_P_CQ, _P_CK, _P_CV, _P_DQ, _P_DK, _P_DV, _P_DT, _P_END = 1536, 1792, 1920, 2048, 2304, 2560, 2816, 2944


def _rope_pair(t, cos, sin):
    lane = lax.broadcasted_iota(jnp.int32, t.shape, 1)
    up = pltpu.roll(t, HEAD_DIM // 4, 1)
    dn = pltpu.roll(t, PAIR - HEAD_DIM // 4, 1)
    partner = jnp.where((lane % 32) < 16, dn, up)
    return t * cos + partner * sin


def _rms_pair(t, g, blockdiag):
    sq = t * t
    hi = sq.astype(BF16)
    lo = (sq - hi.astype(F32)).astype(BF16)
    ss = _dot(hi, blockdiag) + _dot(lo, blockdiag)
    return t * lax.rsqrt(ss * (1.0 / HEAD_DIM) + EPS) * g


def _inproj_kernel(x_ref, mod_ref, w_ref, cos_ref, sin_ref, qg_ref, kg_ref, bd_ref,
                   aq_ref, ak_ref, av_ref, z_ref, xbc_ref, dt_ref,
                   cq_ref, ck_ref, cv_ref, dq_ref, dk_ref, dv_ref, *, gqa_q_scale):
    x = x_ref[...]
    h = (x * (1.0 + mod_ref[0, 4:5, :]) + mod_ref[0, 3:4, :]).astype(BF16)
    y = _dot(h, w_ref[...])
    cos = cos_ref[...]
    sin = sin_ref[...]
    bd = bd_ref[...]

    def pairs(lo, n):
        return [y[:, lo + PAIR * p: lo + PAIR * (p + 1)] for p in range(n)]

    def put(ref, parts, scale=None):
        for p, part in enumerate(parts):
            if scale is not None:
                part = part * scale
            ref[:, PAIR * p: PAIR * (p + 1)] = part.astype(ref.dtype)

    put(aq_ref, [_rope_pair(t, cos, sin) for t in pairs(_P_AQ, 2)], Q_SCALE)
    put(ak_ref, [_rope_pair(t, cos, sin) for t in pairs(_P_AK, 1)])
    put(av_ref, pairs(_P_AV, 1))
    put(z_ref, pairs(_P_Z, 2))
    put(xbc_ref, pairs(_P_XBC, 6))
    put(dt_ref, pairs(_P_DT, 1))
    put(cq_ref, [_rope_pair(_rms_pair(t, qg_ref[...], bd), cos, sin) for t in pairs(_P_CQ, 2)], gqa_q_scale)
    put(ck_ref, [_rope_pair(_rms_pair(t, kg_ref[...], bd), cos, sin) for t in pairs(_P_CK, 1)])
    put(cv_ref, pairs(_P_CV, 1))
    put(dq_ref, pairs(_P_DQ, 2), Q_SCALE)
    put(dk_ref, pairs(_P_DK, 2))
    put(dv_ref, pairs(_P_DV, 2))


def _inproj(xt, mod, w, cos, sin, qg, kg, blockdiag, *, tokens_per_mod, tm, gqa_q_scale):
    t, d = xt.shape
    tiles_per_mod = tokens_per_mod // tm
    table_tiles = cos.shape[0] // tm
    widths = (256, 128, 128, 256, SSD_CONV_CH, DT_PAD, 256, 128, 128, 256, 256, 256)
    dtypes = (BF16,) * 5 + (F32,) + (BF16,) * 6
    row = lambda i: (i, 0)
    const = lambda i: (0, 0)
    return pl.pallas_call(
        functools.partial(_inproj_kernel, gqa_q_scale=gqa_q_scale),
        grid=(t // tm,),
        in_specs=[
            pl.BlockSpec((tm, d), row),
            pl.BlockSpec((1, N_MOD, d), lambda i: (i // tiles_per_mod, 0, 0)),
            _resident((d, _P_END), const),
            pl.BlockSpec((tm, PAIR), lambda i: (i % table_tiles, 0)),
            pl.BlockSpec((tm, PAIR), lambda i: (i % table_tiles, 0)),
            pl.BlockSpec((1, PAIR), const),
            pl.BlockSpec((1, PAIR), const),
            pl.BlockSpec((PAIR, PAIR), const),
        ],
        out_specs=[pl.BlockSpec((tm, wd), row) for wd in widths],
        out_shape=[jax.ShapeDtypeStruct((t, wd), dt) for wd, dt in zip(widths, dtypes)],
        compiler_params=_cparams("arbitrary"),
        name="inproj",
    )(xt, mod, w, cos, sin, qg, kg, blockdiag)


def _stack_heads(q_pairs):
    lane = lax.broadcasted_iota(jnp.int32, q_pairs[0].shape, 1)
    lo = lane < HEAD_DIM
    zero = jnp.zeros_like(q_pairs[0])
    blocks = []
    for qp in q_pairs:
        blocks.append(jnp.where(lo, qp, zero))
        blocks.append(jnp.where(lo, zero, qp))
    return jnp.concatenate(blocks, axis=0)


def _unstack_heads(o, n_pairs, tq):
    lane = lax.broadcasted_iota(jnp.int32, (tq, PAIR), 1)
    lo = lane < HEAD_DIM
    return [jnp.where(lo, o[2 * p * tq:(2 * p + 1) * tq], o[(2 * p + 1) * tq:(2 * p + 2) * tq])
            for p in range(n_pairs)]


def _sink_column(sink_ref, tq):
    blk = lax.broadcasted_iota(jnp.int32, (4 * tq, 1), 0) // tq
    col = jnp.full((4 * tq, 1), sink_ref[HEAD_PERM[3]], F32)
    for b in (2, 1, 0):
        col = jnp.where(blk == b, sink_ref[HEAD_PERM[b]], col)
    return col


def _swa_kernel(sink_ref, q_ref, k_ref, v_ref, kc_ref, vc_ref, o_ref, *, seq):
    n = pl.program_id(1)
    blk = SWA_BLOCK
    span = 3 * blk
    start = pl.multiple_of(jnp.clip((n - 1) * blk, 0, seq - span), blk)
    kl = k_ref[0, pl.ds(start, span), :]
    vl = v_ref[0, pl.ds(start, span), :]
    q = q_ref[0]
    qs = _stack_heads([q[:, :PAIR], q[:, PAIR:]])
    s_loc = _dot_nt(qs, kl)
    s_ctx = _dot_nt(qs, kc_ref[0])
    qpos = n * blk + lax.broadcasted_iota(jnp.int32, s_loc.shape, 0) % blk
    kpos = start + lax.broadcasted_iota(jnp.int32, s_loc.shape, 1)
    s_loc = jnp.where(jnp.abs(kpos - qpos) <= SWA_WINDOW, s_loc, NEG)
    sk = _sink_column(sink_ref, blk)
    m = jnp.maximum(jnp.maximum(jnp.max(s_loc, axis=-1, keepdims=True),
                                jnp.max(s_ctx, axis=-1, keepdims=True)), sk)
    p_loc = jnp.exp(s_loc - m)
    p_ctx = jnp.exp(s_ctx - m)
    denom = (jnp.sum(p_loc, axis=-1, keepdims=True) + jnp.sum(p_ctx, axis=-1, keepdims=True)
             + jnp.exp(sk - m))
    o = _dot(p_loc.astype(BF16), vl) + _dot(p_ctx.astype(BF16), vc_ref[0])
    o = o * (1.0 / denom)
    o_a, o_b = _unstack_heads(o, 2, blk)
    o_ref[0, :, :PAIR] = o_a.astype(o_ref.dtype)
    o_ref[0, :, PAIR:] = o_b.astype(o_ref.dtype)


def _swa(sink, q, k, v, kc, vc):
    b, s, _ = q.shape
    m = kc.shape[1]
    blk = SWA_BLOCK
    assert s % blk == 0 and s >= 3 * blk
    return pl.pallas_call(
        functools.partial(_swa_kernel, seq=s),
        grid=(b, s // blk),
        in_specs=[
            pl.BlockSpec(memory_space=pltpu.SMEM),
            pl.BlockSpec((1, blk, GROUP_WIDTH), lambda i, n: (i, n, 0)),
            pl.BlockSpec((1, s, PAIR), lambda i, n: (i, 0, 0)),
            pl.BlockSpec((1, s, PAIR), lambda i, n: (i, 0, 0)),
            pl.BlockSpec((1, m, PAIR), lambda i, n: (i, 0, 0)),
            pl.BlockSpec((1, m, PAIR), lambda i, n: (i, 0, 0)),
        ],
        out_specs=pl.BlockSpec((1, blk, GROUP_WIDTH), lambda i, n: (i, n, 0)),
        out_shape=jax.ShapeDtypeStruct((b, s, GROUP_WIDTH), BF16),
        compiler_params=_cparams("arbitrary", "arbitrary"),
        name="swa_attn",
    )(sink, q, k, v, kc, vc)


def _dense_attn_kernel(*refs, kv_pairs, n_src, has_sink, tq, tk):
    refs = list(refs)
    sink_ref = refs.pop(0) if has_sink else None
    q_ref = refs.pop(0)
    srcs = [(refs[2 * i], refs[2 * i + 1]) for i in range(n_src)]
    o_ref = refs[2 * n_src]
    q = q_ref[0]
    if kv_pairs == 1:
        units = [([q[:, :PAIR], q[:, PAIR:]], 0)]
    else:
        units = [([q[:, :PAIR]], 0), ([q[:, PAIR:]], 1)]
    outs = []
    for q_pairs, kv in units:
        qs = _stack_heads(q_pairs)
        nrow = qs.shape[0]
        if has_sink:
            m = _sink_column(sink_ref, tq)
            l = jnp.ones((nrow, 1), F32)
        else:
            m = jnp.full((nrow, 1), NEG, F32)
            l = jnp.zeros((nrow, 1), F32)
        acc = jnp.zeros((nrow, PAIR), F32)
        for k_ref, v_ref in srcs:
            nk = k_ref.shape[1]
            step = min(tk, nk)
            for c in range(nk // step):
                kch = k_ref[0, c * step:(c + 1) * step, kv * PAIR:(kv + 1) * PAIR]
                vch = v_ref[0, c * step:(c + 1) * step, kv * PAIR:(kv + 1) * PAIR]
                s = _dot_nt(qs, kch)
                m_new = jnp.maximum(m, jnp.max(s, axis=-1, keepdims=True))
                a = jnp.exp(m - m_new)
                p = jnp.exp(s - m_new)
                l = a * l + jnp.sum(p, axis=-1, keepdims=True)
                acc = a * acc + _dot(p.astype(BF16), vch)
                m = m_new
        o = acc * (1.0 / l)
        outs += _unstack_heads(o, len(q_pairs), tq)
    o_ref[0, :, :PAIR] = outs[0].astype(o_ref.dtype)
    o_ref[0, :, PAIR:] = outs[1].astype(o_ref.dtype)


def _dense_attn(q, srcs, *, kv_pairs, sink=None, tq=128, tk=512):
    b, s, _ = q.shape
    kvw = kv_pairs * PAIR
    has_sink = sink is not None
    in_specs, args = [], []
    if has_sink:
        in_specs.append(pl.BlockSpec(memory_space=pltpu.SMEM))
        args.append(sink)
    in_specs.append(pl.BlockSpec((1, tq, GROUP_WIDTH), lambda i, n: (i, n, 0)))
    args.append(q)
    for k, v in srcs:
        nk = k.shape[1]
        assert nk % min(tk, nk) == 0
        in_specs += [pl.BlockSpec((1, nk, kvw), lambda i, n: (i, 0, 0))] * 2
        args += [k, v]
    return pl.pallas_call(
        functools.partial(_dense_attn_kernel, kv_pairs=kv_pairs, n_src=len(srcs),
                          has_sink=has_sink, tq=tq, tk=tk),
        grid=(b, s // tq),
        in_specs=in_specs,
        out_specs=pl.BlockSpec((1, tq, GROUP_WIDTH), lambda i, n: (i, n, 0)),
        out_shape=jax.ShapeDtypeStruct((b, s, GROUP_WIDTH), BF16),
        compiler_params=_cparams("arbitrary", "arbitrary"),
        name="dense_attn",
    )(*args)


def _gqa_kernel(q_ref, k_ref, v_ref, kc_ref, vc_ref, o_ref, vt_sc, *, tq, tk):
    n = pl.program_id(1)
    seq = k_ref.shape[1]
    m_ctx = kc_ref.shape[1]

    @pl.when(n == 0)
    def _():
        for c in range(seq // PAIR):
            rows = slice(c * PAIR, (c + 1) * PAIR)
            vt_sc[:, rows] = v_ref[0, rows, :].astype(F32).T.astype(BF16)
        for c in range(m_ctx // PAIR):
            rows = slice(c * PAIR, (c + 1) * PAIR)
            vt_sc[:, seq + c * PAIR: seq + (c + 1) * PAIR] = vc_ref[0, rows, :].astype(F32).T.astype(BF16)

    q = q_ref[0]
    qs = _stack_heads([q[:, :PAIR], q[:, PAIR:]])
    nrow = 4 * tq
    m = jnp.full((1, nrow), NEG, F32)
    l = jnp.zeros((1, nrow), F32)
    acc = jnp.zeros((PAIR, nrow), F32)
    chunks = [(k_ref, c * tk, tk, c * tk) for c in range(seq // tk)]
    chunks += [(kc_ref, c * min(tk, m_ctx), min(tk, m_ctx), seq + c * min(tk, m_ctx))
               for c in range(m_ctx // min(tk, m_ctx))]
    for ref, lo, size, col in chunks:
        s_t = _dot_nt(ref[0, lo:lo + size, :], qs)
        m_new = jnp.maximum(m, jnp.max(s_t, axis=0, keepdims=True))
        a = jnp.exp2(m - m_new)
        p_t = jnp.exp2(s_t - m_new)
        l = a * l + jnp.sum(p_t, axis=0, keepdims=True)
        acc = a * acc + _dot(vt_sc[:, col:col + size], p_t.astype(BF16))
        m = m_new
    o_t = acc * (1.0 / l)
    lo_rows = lax.broadcasted_iota(jnp.int32, (PAIR, tq), 0) < HEAD_DIM
    for p in range(2):
        pair_t = jnp.where(lo_rows, o_t[:, 2 * p * tq:(2 * p + 1) * tq],
                           o_t[:, (2 * p + 1) * tq:(2 * p + 2) * tq])
        o_ref[0, :, p * PAIR:(p + 1) * PAIR] = pair_t.T.astype(o_ref.dtype)


def _gqa(q, k, v, kc, vc, *, tq=256, tk=512):
    b, s, _ = q.shape
    m = kc.shape[1]
    tq = _token_tile(s, tq)
    assert s % tk == 0 and s % PAIR == 0 and m % PAIR == 0 and m % min(tk, m) == 0
    full = lambda i, n: (i, 0, 0)
    return pl.pallas_call(
        functools.partial(_gqa_kernel, tq=tq, tk=tk),
        grid=(b, s // tq),
        in_specs=[
            pl.BlockSpec((1, tq, GROUP_WIDTH), lambda i, n: (i, n, 0)),
            pl.BlockSpec((1, s, PAIR), full),
            pl.BlockSpec((1, s, PAIR), full),
            pl.BlockSpec((1, m, PAIR), full),
            pl.BlockSpec((1, m, PAIR), full),
        ],
        out_specs=pl.BlockSpec((1, tq, GROUP_WIDTH), lambda i, n: (i, n, 0)),
        out_shape=jax.ShapeDtypeStruct((b, s, GROUP_WIDTH), BF16),
        scratch_shapes=[pltpu.VMEM((PAIR, s + m), BF16)],
        compiler_params=_cparams("arbitrary", "arbitrary"),
        name="gqa_attn",
    )(q, k, v, kc, vc)


NA_RPB_ROWS = 2 * NA_ROWS


def _na_rpb_pairs(rpb):
    h = rpb.shape[0]
    t = jnp.pad(rpb.astype(F32), ((0, 0), (1, 1), (0, HEAD_DIM - rpb.shape[2])))
    pairs = jnp.concatenate([t[:, :-1], t[:, 1:]], axis=-1)
    assert pairs.shape == (h, NA_RPB_ROWS, PAIR)
    return jnp.roll(pairs, -(NA_COLS - 1), axis=-1)


def _na_build_bias(rpb_ref, bias_sc, g, rows):
    w = GRID_W
    r0 = NA_QROWS * g
    start_row = jnp.clip(r0 - NA_ROWS // 2, 0, rows - NA_KROWS)
    qcol = lax.broadcasted_iota(jnp.int32, (w, PAIR), 0)
    lane = lax.broadcasted_iota(jnp.int32, (w, PAIR), 1)
    kcol = lane % w
    odd = (lane >= w).astype(jnp.int32)
    cs = jnp.clip(qcol - NA_COLS // 2, 0, w - NA_COLS)
    col_ok = (kcol >= cs) & (kcol < cs + NA_COLS)

    def body(t, carry):
        rr = t // (NA_KROWS // 2)
        a2 = t % (NA_KROWS // 2)
        r = r0 + rr
        rs = jnp.clip(r - NA_ROWS // 2, 0, rows - NA_ROWS)
        krow0 = start_row + 2 * a2
        krow = krow0 + odd
        ok = col_ok & (krow >= rs) & (krow < rs + NA_ROWS)
        e = jnp.clip(krow0 - r + NA_ROWS, 0, NA_RPB_ROWS - 1)
        for h in range(4):
            tile = jnp.broadcast_to(rpb_ref[h, pl.ds(e, 1), :], (w, PAIR))
            for bit in range(6):
                tile = jnp.where(((qcol >> bit) & 1) == 1, pltpu.roll(tile, 1 << bit, 1), tile)
            row0 = pl.multiple_of((h % 2) * NA_QROWS * w + rr * w, w)
            bias_sc[h // 2, a2, pl.ds(row0, w), :] = jnp.where(ok, tile, NEG)
        return carry

    lax.fori_loop(0, NA_QROWS * (NA_KROWS // 2), body, 0)


def _na_kernel(q_ref, k_ref, v_ref, kc_ref, vc_ref, rpb_ref, o_ref, bias_sc, *, rows):
    g = pl.program_id(1)
    groups = rows // NA_QROWS
    tq = NA_QROWS * GRID_W
    span = NA_KROWS * GRID_W

    @pl.when((g == 0) | (g == 1) | (g == groups - 1))
    def _():
        _na_build_bias(rpb_ref, bias_sc, g, rows)

    start = pl.multiple_of(jnp.clip(NA_QROWS * g - NA_ROWS // 2, 0, rows - NA_KROWS) * GRID_W, GRID_W)
    q = q_ref[0]
    for p in range(2):
        lanes = slice(p * PAIR, (p + 1) * PAIR)
        qs = _stack_heads([q[:, lanes]])
        kl = k_ref[0, pl.ds(start, span), lanes]
        vl = v_ref[0, pl.ds(start, span), lanes]
        bias = jnp.concatenate([bias_sc[p, a2] for a2 in range(NA_KROWS // 2)], axis=1)
        s_nb = _dot_nt(qs, kl) + bias
        s_ctx = _dot_nt(qs, kc_ref[0, :, lanes])
        m = jnp.maximum(jnp.max(s_nb, axis=-1, keepdims=True), jnp.max(s_ctx, axis=-1, keepdims=True))
        p_nb = jnp.exp(s_nb - m)
        p_ctx = jnp.exp(s_ctx - m)
        denom = jnp.sum(p_nb, axis=-1, keepdims=True) + jnp.sum(p_ctx, axis=-1, keepdims=True)
        o = _dot(p_nb.astype(BF16), vl) + _dot(p_ctx.astype(BF16), vc_ref[0, :, lanes])
        o = o * (1.0 / denom)
        o_ref[0, :, lanes] = _unstack_heads(o, 1, tq)[0].astype(o_ref.dtype)


def _na(q, k, v, kc, vc, rpb_pairs):
    b, s, _ = q.shape
    m = kc.shape[1]
    rows = s // GRID_W
    groups = rows // NA_QROWS
    tq = NA_QROWS * GRID_W
    assert rows % NA_QROWS == 0 and groups >= 4
    return pl.pallas_call(
        functools.partial(_na_kernel, rows=rows),
        grid=(b, groups),
        in_specs=[
            pl.BlockSpec((1, tq, GROUP_WIDTH), lambda i, g: (i, g, 0)),
            pl.BlockSpec((1, s, GROUP_WIDTH), lambda i, g: (i, 0, 0)),
            pl.BlockSpec((1, s, GROUP_WIDTH), lambda i, g: (i, 0, 0)),
            pl.BlockSpec((1, m, GROUP_WIDTH), lambda i, g: (i, 0, 0)),
            pl.BlockSpec((1, m, GROUP_WIDTH), lambda i, g: (i, 0, 0)),
            pl.BlockSpec((4, NA_RPB_ROWS, PAIR), lambda i, g: (0, 0, 0)),
        ],
        out_specs=pl.BlockSpec((1, tq, GROUP_WIDTH), lambda i, g: (i, g, 0)),
        out_shape=jax.ShapeDtypeStruct((b, s, GROUP_WIDTH), BF16),
        scratch_shapes=[pltpu.VMEM((2, NA_KROWS // 2, 2 * tq, PAIR), F32)],
        compiler_params=_cparams("arbitrary", "arbitrary"),
        name="na_attn",
    )(q, k, v, kc, vc, rpb_pairs)


def _ssd_kernel(xf_ref, xfp_ref, xfn_ref, xb_ref, xbp_ref, xbn_ref, dtf_ref, dtb_ref,
                cw_ref, cb_ref, dtbias_ref, alog_ref, dskip_ref, h0f_ref, h0b_ref,
                yf_ref, yb_ref, hf_ref, hb_ref, hf_sc, hb_sc, *, nc):
    i = pl.program_id(1)
    q = SSD_CHUNK

    @pl.when(i == 0)
    def _():
        hf_sc[...] = h0f_ref[0]
        hb_sc[...] = h0b_ref[0]

    def conv(main_ref, prev_ref, next_ref, first, last):
        xm = main_ref[0].astype(F32)
        xp = jnp.where(first, 0.0, prev_ref[0].astype(F32))
        xn = jnp.where(last, 0.0, next_ref[0].astype(F32))
        ext = jnp.concatenate([xp, xm, xn], axis=0)
        acc = cb_ref[...]
        for k in range(SSD_CONV):
            lo = SSD_HALO - SSD_CONV // 2 + k
            acc = acc + cw_ref[k:k + 1, :] * ext[lo:lo + q, :]
        return _silu(acc)

    ii = lax.broadcasted_iota(jnp.int32, (q, q), 0)
    jj = lax.broadcasted_iota(jnp.int32, (q, q), 1)
    lo_lanes = jj < HEAD_DIM
    a_coef = -jnp.exp(alog_ref[...])

    def chunk(u, dt_raw, col0, reverse, h_sc):
        causal = (jj >= ii) if reverse else (jj <= ii)
        dt = _softplus(dt_raw + dtbias_ref[...])
        tri = jnp.where(causal, 1.0, 0.0).astype(BF16)
        a_hi, a_mid, a_lo = _split3(dt * a_coef)
        cum = _dot(tri, a_hi) + _dot(tri, a_mid) + _dot(tri, a_lo)
        cum_t = cum.T
        dt_t = dt.T
        end = cum[0:1, :] if reverse else cum[q - 1:q, :]
        xs = u[:, :2 * PAIR]
        ys = []
        for k in range(2):
            bk = u[:, 2 * PAIR + k * SSD_STATE: 2 * PAIR + (k + 1) * SSD_STATE]
            ck = u[:, 2 * PAIR + 2 * SSD_STATE + k * SSD_STATE: 2 * PAIR + 2 * SSD_STATE + (k + 1) * SSD_STATE]
            xk = xs[:, k * PAIR:(k + 1) * PAIR]
            ck_b = ck.astype(BF16)
            xk_b = xk.astype(BF16)
            cb = _dot_nt(ck_b, bk.astype(BF16))
            c0 = col0 + 2 * k
            cols = []
            for r in range(2):
                c = c0 + r
                seg = cum[:, c:c + 1] - cum_t[c:c + 1, :]
                lmat = jnp.exp(jnp.where(causal, seg, NEG))
                att = (cb * lmat * dt_t[c:c + 1, :]).astype(BF16)
                cols.append(_dot(att, xk_b))
            y_intra = jnp.where(lo_lanes, cols[0], cols[1])
            e_in = jnp.where(lo_lanes, jnp.exp(cum[:, c0:c0 + 1]), jnp.exp(cum[:, c0 + 1:c0 + 2]))
            h_t = h_sc[k]
            y_state = _dot(ck_b, h_t.astype(BF16)) * e_in
            w0 = jnp.exp(end[:, c0:c0 + 1] - cum[:, c0:c0 + 1]) * dt[:, c0:c0 + 1]
            w1 = jnp.exp(end[:, c0 + 1:c0 + 2] - cum[:, c0 + 1:c0 + 2]) * dt[:, c0 + 1:c0 + 2]
            xw = (xk * jnp.where(lo_lanes, w0, w1)).astype(BF16)
            st = _dot(bk.T.astype(BF16), xw)
            decay = jnp.where(lo_lanes[0:1], jnp.exp(end[:, c0:c0 + 1]), jnp.exp(end[:, c0 + 1:c0 + 2]))
            h_sc[k] = h_t * decay + st
            ys.append(y_intra + y_state)
        return ys, xs

    uf = conv(xf_ref, xfp_ref, xfn_ref, i == 0, i == nc - 1)
    ys, xs = chunk(uf, dtf_ref[0], 0, False, hf_sc)
    dskip = dskip_ref[...]
    for k in range(2):
        lanes = slice(k * PAIR, (k + 1) * PAIR)
        yf_ref[0, :, lanes] = ys[k] + dskip[:, lanes] * xs[:, lanes]

    ub = conv(xb_ref, xbp_ref, xbn_ref, i == nc - 1, i == 0)
    ys, _ = chunk(ub, dtb_ref[0], 4, True, hb_sc)
    for k in range(2):
        yb_ref[0, :, k * PAIR:(k + 1) * PAIR] = ys[k]

    @pl.when(i == nc - 1)
    def _():
        hf_ref[0] = hf_sc[...]
        hb_ref[0] = hb_sc[...]


def _ssd(xbc, dt, conv_w, conv_b, dt_bias, a_log, dskip, h0f, h0b):
    b, length, ch = xbc.shape
    q = SSD_CHUNK
    nc = length // q
    per = q // SSD_HALO
    last_halo = length // SSD_HALO - 1
    fwd = lambda i, c: (i, c, 0)
    bwd = lambda i, c: (i, nc - 1 - c, 0)
    fwd_prev = lambda i, c: (i, jnp.maximum(c * per - 1, 0), 0)
    fwd_next = lambda i, c: (i, jnp.minimum((c + 1) * per, last_halo), 0)
    bwd_prev = lambda i, c: (i, jnp.maximum((nc - 1 - c) * per - 1, 0), 0)
    bwd_next = lambda i, c: (i, jnp.minimum((nc - c) * per, last_halo), 0)
    const = lambda i, c: (0, 0)
    state = lambda i, c: (i, 0, 0, 0)
    state_shape = (b, 2, SSD_STATE, PAIR)
    return pl.pallas_call(
        functools.partial(_ssd_kernel, nc=nc),
        grid=(b, nc),
        in_specs=[
            pl.BlockSpec((1, q, ch), fwd),
            pl.BlockSpec((1, SSD_HALO, ch), fwd_prev),
            pl.BlockSpec((1, SSD_HALO, ch), fwd_next),
            pl.BlockSpec((1, q, ch), bwd),
            pl.BlockSpec((1, SSD_HALO, ch), bwd_prev),
            pl.BlockSpec((1, SSD_HALO, ch), bwd_next),
            pl.BlockSpec((1, q, DT_PAD), fwd),
            pl.BlockSpec((1, q, DT_PAD), bwd),
            pl.BlockSpec((8, ch), const),
            pl.BlockSpec((1, ch), const),
            pl.BlockSpec((1, DT_PAD), const),
            pl.BlockSpec((1, DT_PAD), const),
            pl.BlockSpec((1, GROUP_WIDTH), const),
            pl.BlockSpec((1, 2, SSD_STATE, PAIR), state),
            pl.BlockSpec((1, 2, SSD_STATE, PAIR), state),
        ],
        out_specs=[
            pl.BlockSpec((1, q, GROUP_WIDTH), fwd),
            pl.BlockSpec((1, q, GROUP_WIDTH), bwd),
            pl.BlockSpec((1, 2, SSD_STATE, PAIR), state),
            pl.BlockSpec((1, 2, SSD_STATE, PAIR), state),
        ],
        out_shape=[
            jax.ShapeDtypeStruct((b, length, GROUP_WIDTH), F32),
            jax.ShapeDtypeStruct((b, length, GROUP_WIDTH), F32),
            jax.ShapeDtypeStruct(state_shape, F32),
            jax.ShapeDtypeStruct(state_shape, F32),
        ],
        scratch_shapes=[pltpu.VMEM((2, SSD_STATE, PAIR), F32), pltpu.VMEM((2, SSD_STATE, PAIR), F32)],
        compiler_params=_cparams("arbitrary", "arbitrary"),
        name="ssd_scan",
    )(xbc, xbc, xbc, xbc, xbc, xbc, dt, dt, conv_w, conv_b, dt_bias, a_log, dskip, h0f, h0b)


def _merge_kernel(x_ref, mod_ref, oa_ref, yf_ref, yb_ref, z_ref, oc_ref, od_ref, ng_ref, w_ref,
                  g_ref, b_ref, o_ref, *, alpha):
    ob = (yf_ref[...] + yb_ref[...]) * _silu(z_ref[...].astype(F32))
    parts = (oa_ref[...].astype(F32), ob, oc_ref[...].astype(F32), od_ref[...].astype(F32))
    acc = None
    for gi, y in enumerate(parts):
        rows = slice(gi * GROUP_WIDTH, (gi + 1) * GROUP_WIDTH)
        ms = jnp.mean(y * y, axis=-1, keepdims=True)
        yn = (y * lax.rsqrt(ms + EPS) * ng_ref[:, rows]).astype(BF16)
        term = _dot(yn, w_ref[rows, :])
        acc = term if acc is None else acc + term
    y = alpha * x_ref[...] + mod_ref[0, 5:6, :] * acc
    o_ref[...] = _layer_norm(y, g_ref[...], b_ref[...])


def _merge(xt, mod, oa, yf, yb, z, oc, od, norm_g, w_out, ln_g, ln_b, *, tokens_per_mod, alpha, tm):
    t, d = xt.shape
    tiles_per_mod = tokens_per_mod // tm
    row = lambda i: (i, 0)
    const = lambda i: (0, 0)
    grp = pl.BlockSpec((tm, GROUP_WIDTH), row)
    return pl.pallas_call(
        functools.partial(_merge_kernel, alpha=alpha),
        grid=(t // tm,),
        in_specs=[
            pl.BlockSpec((tm, d), row),
            pl.BlockSpec((1, N_MOD, d), lambda i: (i // tiles_per_mod, 0, 0)),
            grp, grp, grp, grp, grp, grp,
            pl.BlockSpec((1, d), const),
            pl.BlockSpec((d, d), const),
            pl.BlockSpec((1, d), const),
            pl.BlockSpec((1, d), const),
        ],
        out_specs=pl.BlockSpec((tm, d), row),
        out_shape=jax.ShapeDtypeStruct((t, d), F32),
        compiler_params=_cparams("arbitrary"),
        name="merge_out",
    )(xt, mod, oa, yf, yb, z, oc, od, norm_g, w_out, ln_g.reshape(1, d), ln_b.reshape(1, d))


def _head_block_index(start, perm):
    return np.concatenate([np.arange(start + h * HEAD_DIM, start + (h + 1) * HEAD_DIM) for h in perm])


def _inproj_columns():
    cols = [
        _head_block_index(0, HEAD_PERM), np.arange(256, 1536),
        _head_block_index(1544, HEAD_PERM), np.arange(1800, 2824),
    ]
    return np.concatenate(cols), np.arange(1536, 1544)


def _outproj_rows():
    return np.concatenate([
        _head_block_index(0, HEAD_PERM), np.arange(256, 512),
        _head_block_index(512, HEAD_PERM), np.arange(768, 1024),
    ])


def _rope_tables(seq):
    t = np.arange(seq)
    quarter = HEAD_DIM // 4
    inv = ROPE_BASE ** (-jnp.arange(quarter, dtype=F32) / quarter)
    a_row = jnp.asarray(t // GRID_W, F32)[:, None] * inv
    a_col = jnp.asarray(t % GRID_W, F32)[:, None] * inv
    cos = jnp.concatenate([jnp.cos(a_row), jnp.cos(a_row), jnp.cos(a_col), jnp.cos(a_col)], axis=1)
    sin = jnp.concatenate([-jnp.sin(a_row), jnp.sin(a_row), -jnp.sin(a_col), jnp.sin(a_col)], axis=1)
    return jnp.tile(cos, (1, 2)), jnp.tile(sin, (1, 2))


def _pad_lanes(v, width):
    v = v.reshape(1, -1).astype(F32)
    return jnp.pad(v, ((0, 0), (0, width - v.shape[1])))


def _token_tile(n, cap):
    t = cap
    while n % t:
        t //= 2
    return t


def kernel(x, c, ctx, c_ctx, ada_w, ada_b, ln_g, ln_b, ffn1_w_in, ffn1_w_out, mix_w_in, mix_w_out,
           mix_norm_g, swa_sink, ssd_conv_w, ssd_conv_b, ssd_dt_bias, ssd_A_log, ssd_D,
           gqa_q_norm, gqa_k_norm, na_rpb, ffn2_w_in, ffn2_w_out):
    bsz, seq, d = x.shape
    m_ctx = ctx.shape[1]
    depth = ada_w.shape[0]
    alpha = float((2 * depth) ** 0.25)
    mod_rows = 16 * ((bsz + 1 + 15) // 16)
    cc = jnp.concatenate([c, c_ctx[None], jnp.zeros((mod_rows - bsz - 1, d), F32)], axis=0)
    mods = _ada(cc, ada_w, ada_b)

    cos_x, sin_x = _rope_tables(seq)
    tm_x = _token_tile(seq, 512)
    tm_c = _token_tile(m_ctx, 256)
    cos_c = jnp.ones((tm_c, PAIR), F32)
    sin_c = jnp.zeros((tm_c, PAIR), F32)
    blockdiag = jnp.asarray(np.kron(np.eye(2), np.ones((HEAD_DIM, HEAD_DIM))), BF16)
    main_cols, dt_cols = _inproj_columns()
    out_rows = _outproj_rows()

    xt = x.reshape(bsz * seq, d)
    ct = ctx.reshape(bsz * m_ctx, d)
    zero_state = jnp.zeros((bsz, 2, SSD_STATE, PAIR), F32)

    for l in range(depth):
        last = l == depth - 1
        mod_x = mods[l, :bsz].reshape(bsz, N_MOD, d)
        mod_c = mods[l, bsz:bsz + 1].reshape(1, N_MOD, d)
        w1_in, w1_out = ffn1_w_in[l].astype(BF16), ffn1_w_out[l].astype(BF16)
        w2_in, w2_out = ffn2_w_in[l].astype(BF16), ffn2_w_out[l].astype(BF16)
        w_mix = mix_w_in[l]
        w_mix = jnp.concatenate(
            [w_mix[:, main_cols], jnp.pad(w_mix[:, dt_cols], ((0, 0), (0, DT_PAD - dt_cols.size)))],
            axis=1).astype(BF16)
        w_o = mix_w_out[l][out_rows].astype(BF16)
        norm_g = mix_norm_g[l][out_rows].reshape(1, d)
        qg = jnp.tile(gqa_q_norm[l], 2).reshape(1, PAIR)
        kg = jnp.tile(gqa_k_norm[l], 2).reshape(1, PAIR)
        conv_w = jnp.pad(ssd_conv_w[l].reshape(SSD_CONV, SSD_CONV_CH), ((0, 8 - SSD_CONV), (0, 0)))
        conv_b = ssd_conv_b[l].reshape(1, SSD_CONV_CH)
        dt_bias = _pad_lanes(ssd_dt_bias[l], DT_PAD)
        a_log = _pad_lanes(ssd_A_log[l], DT_PAD)
        dskip = jnp.repeat(ssd_D[l], HEAD_DIM).reshape(1, GROUP_WIDTH)
        sink = swa_sink[l].astype(F32)
        rpb_pairs = _na_rpb_pairs(na_rpb[l])

        ffn_x = functools.partial(_ffn, tokens_per_mod=seq, alpha=alpha, tm=tm_x)
        ffn_c = functools.partial(_ffn, tokens_per_mod=bsz * m_ctx, alpha=alpha, tm=tm_c)

        xt = ffn_x(xt, mod_x, (0, 1, 2), w1_in, w1_out, ln_g[l, 0], ln_b[l, 0])
        ct = ffn_c(ct, mod_c, (0, 1, 2), w1_in, w1_out, ln_g[l, 0], ln_b[l, 0])

        px = _inproj(xt, mod_x, w_mix, cos_x, sin_x, qg, kg, blockdiag, tokens_per_mod=seq, tm=tm_x,
                     gqa_q_scale=Q_SCALE * LOG2E)
        pc = _inproj(ct, mod_c, w_mix, cos_c, sin_c, qg, kg, blockdiag,
                     tokens_per_mod=bsz * m_ctx, tm=tm_c, gqa_q_scale=Q_SCALE)
        aq, ak, av, bz, bxbc, bdt, cq, ck, cv, dq, dk, dv = [
            t.reshape(bsz, seq, t.shape[-1]) for t in px]
        aq_c, ak_c, av_c, bz_c, bxbc_c, bdt_c, cq_c, ck_c, cv_c, dq_c, dk_c, dv_c = [
            t.reshape(bsz, m_ctx, t.shape[-1]) for t in pc]

        ssd = functools.partial(_ssd, conv_w=conv_w, conv_b=conv_b, dt_bias=dt_bias, a_log=a_log,
                                dskip=dskip)
        yf_c, yb_c, hf_c, hb_c = ssd(bxbc_c, bdt_c, h0f=zero_state, h0b=zero_state)
        yf, yb, _, _ = ssd(bxbc, bdt, h0f=hf_c, h0b=hb_c)

        oa = _swa(sink, aq, ak, av, ak_c, av_c)
        oc = _gqa(cq, ck, cv, ck_c, cv_c)
        od = _na(dq, dk, dv, dk_c, dv_c, rpb_pairs)

        flat = lambda t: t.reshape(-1, t.shape[-1])
        xt = _merge(xt, mod_x, flat(oa), flat(yf), flat(yb), flat(bz), flat(oc), flat(od), norm_g, w_o,
                    ln_g[l, 1], ln_b[l, 1], tokens_per_mod=seq, alpha=alpha, tm=tm_x)
        xt = ffn_x(xt, mod_x, (6, 7, 8), w2_in, w2_out, ln_g[l, 2], ln_b[l, 2])

        if not last:
            tq_c = _token_tile(m_ctx, 128)
            oa_c = _dense_attn(aq_c, [(ak_c, av_c)], kv_pairs=1, sink=sink, tq=tq_c)
            oc_c = _dense_attn(cq_c, [(ck_c, cv_c)], kv_pairs=1, tq=tq_c)
            od_c = _dense_attn(dq_c, [(dk_c, dv_c)], kv_pairs=2, tq=tq_c)
            ct = _merge(ct, mod_c, flat(oa_c), flat(yf_c), flat(yb_c), flat(bz_c), flat(oc_c), flat(od_c),
                        norm_g, w_o, ln_g[l, 1], ln_b[l, 1], tokens_per_mod=bsz * m_ctx, alpha=alpha,
                        tm=tm_c)
            ct = ffn_c(ct, mod_c, (6, 7, 8), w2_in, w2_out, ln_g[l, 2], ln_b[l, 2])

    return xt.reshape(bsz, seq, d)
```

```python
import functools

import numpy as np
import jax
import jax.numpy as jnp
from jax import lax
from jax.experimental import pallas as pl
from jax.experimental.pallas import tpu as pltpu

F32 = jnp.float32
BF16 = jnp.bfloat16

HEAD_DIM = 64
PAIR = 2 * HEAD_DIM
GROUP_WIDTH = 256
GRID_W = 64
N_MOD = 9
SWA_WINDOW = 128
SWA_BLOCK = 128
SSD_CHUNK = 128
SSD_STATE = 128
SSD_CONV = 5
CONV_HALO = 8
FFN_SUBTILE = 256
SSD_CONV_CH = 768
NA_ROWS = 8
NA_COLS = 16
NA_QROWS = 4
NA_KROWS = 12
ROPE_BASE = 10000.0
EPS = 1e-5
NEG = -1e30
Q_SCALE = HEAD_DIM ** -0.5
LOG2E = 1.4426950408889634
DT_PAD = 128
HEAD_PERM = (0, 2, 1, 3)

VMEM_LIMIT_BYTES = 56 * 1024 * 1024


def _cparams(*sem):
    return pltpu.CompilerParams(dimension_semantics=sem, vmem_limit_bytes=VMEM_LIMIT_BYTES)


def _dot(a, b):
    return jnp.dot(a, b, preferred_element_type=F32)


def _dot_nt(a, b):
    return lax.dot_general(a, b, (((1,), (1,)), ((), ())), preferred_element_type=F32)


def _sigmoid(x):
    return 1.0 / (1.0 + jnp.exp(-x))


def _silu(x):
    return x * _sigmoid(x)


def _softplus(x):
    return jnp.maximum(x, 0.0) + jnp.log(1.0 + jnp.exp(-jnp.abs(x)))


def _split3(a):
    hi = a.astype(BF16)
    r1 = a - hi.astype(F32)
    mid = r1.astype(BF16)
    lo = (r1 - mid.astype(F32)).astype(BF16)
    return hi, mid, lo


def _layer_norm(y, g, b):
    mu = jnp.mean(y, axis=-1, keepdims=True)
    d = y - mu
    var = jnp.mean(d * d, axis=-1, keepdims=True)
    return d * lax.rsqrt(var + EPS) * g + b


def _ada_kernel(c_ref, w_ref, b_ref, o_ref):
    s = _silu(c_ref[...])
    s_hi = s.astype(BF16)
    s_lo = (s - s_hi.astype(F32)).astype(BF16)
    w = w_ref[0]
    w_hi = w.astype(BF16)
    w_lo = (w - w_hi.astype(F32)).astype(BF16)
    o_ref[0] = _dot(s_hi, w_hi) + _dot(s_lo, w_hi) + _dot(s_hi, w_lo) + b_ref[0]


def _ada(cc, ada_w, ada_b):
    depth, d, n = ada_w.shape
    rows = cc.shape[0]
    tn = 1024
    return pl.pallas_call(
        _ada_kernel,
        grid=(depth, n // tn),
        in_specs=[
            pl.BlockSpec((rows, d), lambda l, j: (0, 0)),
            pl.BlockSpec((1, d, tn), lambda l, j: (l, 0, j)),
            pl.BlockSpec((1, 1, tn), lambda l, j: (l, 0, j)),
        ],
        out_specs=pl.BlockSpec((1, rows, tn), lambda l, j: (l, 0, j)),
        out_shape=jax.ShapeDtypeStruct((depth, rows, n), F32),
        compiler_params=_cparams("arbitrary", "arbitrary"),
        name="ada_mod",
    )(cc, ada_w, ada_b.reshape(depth, 1, n))


def _ffn_kernel(x_ref, mod_ref, wa_ref, wu_ref, wo_ref, g_ref, b_ref, o_ref, *, rows, alpha):
    r_shift, r_scale, r_gate = rows
    tm = x_ref.shape[0]
    sub = FFN_SUBTILE if tm % FFN_SUBTILE == 0 else tm
    for r0 in range(0, tm, sub):
        x = x_ref[r0:r0 + sub, :]
        h = x * (1.0 + mod_ref[0, r_scale:r_scale + 1, :]) + mod_ref[0, r_shift:r_shift + 1, :]
        h = h.astype(BF16)
        a = _dot(h, wa_ref[...])
        u = _dot(h, wu_ref[...])
        gated = (_silu(a) * u).astype(BF16)
        f = _dot(gated, wo_ref[...])
        y = alpha * x + (0.5 * mod_ref[0, r_gate:r_gate + 1, :]) * f
        o_ref[r0:r0 + sub, :] = _layer_norm(y, g_ref[...], b_ref[...])


def _resident(shape, index_map):
    return pl.BlockSpec(shape, index_map, pipeline_mode=pl.Buffered(1))


def _ffn(xt, mod, rows, w_in, w_out, ln_g, ln_b, *, tokens_per_mod, alpha, tm):
    t, d = xt.shape
    dff = w_out.shape[0]
    tiles_per_mod = tokens_per_mod // tm
    return pl.pallas_call(
        functools.partial(_ffn_kernel, rows=rows, alpha=alpha),
        grid=(t // tm,),
        in_specs=[
            pl.BlockSpec((tm, d), lambda i: (i, 0)),
            pl.BlockSpec((1, N_MOD, d), lambda i: (i // tiles_per_mod, 0, 0)),
            _resident((d, dff), lambda i: (0, 0)),
            _resident((d, dff), lambda i: (0, 1)),
            _resident((dff, d), lambda i: (0, 0)),
            pl.BlockSpec((1, d), lambda i: (0, 0)),
            pl.BlockSpec((1, d), lambda i: (0, 0)),
        ],
        out_specs=pl.BlockSpec((tm, d), lambda i: (i, 0)),
        out_shape=jax.ShapeDtypeStruct((t, d), F32),
        compiler_params=_cparams("arbitrary"),
        name="ffn",
    )(xt, mod, w_in, w_in, w_out, ln_g.reshape(1, d), ln_b.reshape(1, d))


_P_AQ, _P_AK, _P_AV, _P_Z, _P_XBC = 0, 256, 384, 512, 768
_P_CQ, _P_CK, _P_CV, _P_DQ, _P_DK, _P_DV, _P_DT, _P_END = 1536, 1792, 1920, 2048, 2304, 2560, 2816, 2944


def _rope_pair(t, cos, sin):
    lane = lax.broadcasted_iota(jnp.int32, t.shape, 1)
    up = pltpu.roll(t, HEAD_DIM // 4, 1)
    dn = pltpu.roll(t, PAIR - HEAD_DIM // 4, 1)
    partner = jnp.where((lane % 32) < 16, dn, up)
    return t * cos + partner * sin


def _rms_pair(t, g, blockdiag):
    sq = t * t
    hi = sq.astype(BF16)
    lo = (sq - hi.astype(F32)).astype(BF16)
    ss = _dot(hi, blockdiag) + _dot(lo, blockdiag)
    return t * lax.rsqrt(ss * (1.0 / HEAD_DIM) + EPS) * g


def _inproj_kernel(x_ref, xp_ref, xn_ref, mod_ref, w_ref, cos_ref, sin_ref, qg_ref, kg_ref, bd_ref,
                   cw_ref, cb_ref,
                   aq_ref, ak_ref, av_ref, z_ref, u_ref, dt_ref,
                   cq_ref, ck_ref, cv_ref, dq_ref, dk_ref, dv_ref, *, gqa_q_scale, tiles_per_seq):
    tm = x_ref.shape[0]
    halo = xp_ref.shape[0]
    pos = pl.program_id(0) % tiles_per_seq
    x_ext = jnp.concatenate([xp_ref[...], x_ref[...], xn_ref[...]], axis=0)
    h = (x_ext * (1.0 + mod_ref[0, 4:5, :]) + mod_ref[0, 3:4, :]).astype(BF16)
    y_ext = _dot(h, w_ref[...])
    y = y_ext[halo:halo + tm]
    cos = cos_ref[...]
    sin = sin_ref[...]
    bd = bd_ref[...]

    xbc = y_ext[:, _P_XBC:_P_XBC + SSD_CONV_CH]
    ext = jnp.concatenate([jnp.where(pos == 0, 0.0, xbc[:halo]), xbc[halo:halo + tm],
                           jnp.where(pos == tiles_per_seq - 1, 0.0, xbc[halo + tm:])], axis=0)
    conv = cb_ref[...]
    for k in range(SSD_CONV):
        lo = halo - SSD_CONV // 2 + k
        conv = conv + cw_ref[k:k + 1, :] * ext[lo:lo + tm, :]
    u_ref[...] = _silu(conv).astype(u_ref.dtype)

    def pairs(lo, n):
        return [y[:, lo + PAIR * p: lo + PAIR * (p + 1)] for p in range(n)]

    def put(ref, parts, scale=None):
        for p, part in enumerate(parts):
            if scale is not None:
                part = part * scale
            ref[:, PAIR * p: PAIR * (p + 1)] = part.astype(ref.dtype)

    put(aq_ref, [_rope_pair(t, cos, sin) for t in pairs(_P_AQ, 2)], Q_SCALE)
    put(ak_ref, [_rope_pair(t, cos, sin) for t in pairs(_P_AK, 1)])
    put(av_ref, pairs(_P_AV, 1))
    put(z_ref, pairs(_P_Z, 2))
    put(dt_ref, pairs(_P_DT, 1))
    put(cq_ref, [_rope_pair(_rms_pair(t, qg_ref[...], bd), cos, sin) for t in pairs(_P_CQ, 2)], gqa_q_scale)
    put(ck_ref, [_rope_pair(_rms_pair(t, kg_ref[...], bd), cos, sin) for t in pairs(_P_CK, 1)])
    put(cv_ref, pairs(_P_CV, 1))
    put(dq_ref, pairs(_P_DQ, 2), Q_SCALE)
    put(dk_ref, pairs(_P_DK, 2))
    put(dv_ref, pairs(_P_DV, 2))


def _inproj(xt, mod, w, cos, sin, qg, kg, blockdiag, conv_w, conv_b, *, tokens_per_mod, seq_len, tm,
            gqa_q_scale):
    t, d = xt.shape
    tiles_per_mod = tokens_per_mod // tm
    table_tiles = cos.shape[0] // tm
    widths = (256, 128, 128, 256, SSD_CONV_CH, DT_PAD, 256, 128, 128, 256, 256, 256)
    dtypes = (BF16,) * 5 + (F32,) + (BF16,) * 6
    per = tm // CONV_HALO
    last_halo = t // CONV_HALO - 1
    row = lambda i: (i, 0)
    const = lambda i: (0, 0)
    return pl.pallas_call(
        functools.partial(_inproj_kernel, gqa_q_scale=gqa_q_scale, tiles_per_seq=seq_len // tm),
        grid=(t // tm,),
        in_specs=[
            pl.BlockSpec((tm, d), row),
            pl.BlockSpec((CONV_HALO, d), lambda i: (jnp.maximum(i * per - 1, 0), 0)),
            pl.BlockSpec((CONV_HALO, d), lambda i: (jnp.minimum((i + 1) * per, last_halo), 0)),
            pl.BlockSpec((1, N_MOD, d), lambda i: (i // tiles_per_mod, 0, 0)),
            _resident((d, _P_END), const),
            pl.BlockSpec((tm, PAIR), lambda i: (i % table_tiles, 0)),
            pl.BlockSpec((tm, PAIR), lambda i: (i % table_tiles, 0)),
            pl.BlockSpec((1, PAIR), const),
            pl.BlockSpec((1, PAIR), const),
            pl.BlockSpec((PAIR, PAIR), const),
            pl.BlockSpec((8, SSD_CONV_CH), const),
            pl.BlockSpec((1, SSD_CONV_CH), const),
        ],
        out_specs=[pl.BlockSpec((tm, wd), row) for wd in widths],
        out_shape=[jax.ShapeDtypeStruct((t, wd), dt) for wd, dt in zip(widths, dtypes)],
        compiler_params=_cparams("arbitrary"),
        name="inproj",
    )(xt, xt, xt, mod, w, cos, sin, qg, kg, blockdiag, conv_w, conv_b)


def _stack_heads(q_pairs):
    lane = lax.broadcasted_iota(jnp.int32, q_pairs[0].shape, 1)
    lo = lane < HEAD_DIM
    zero = jnp.zeros_like(q_pairs[0])
    blocks = []
    for qp in q_pairs:
        blocks.append(jnp.where(lo, qp, zero))
        blocks.append(jnp.where(lo, zero, qp))
    return jnp.concatenate(blocks, axis=0)


def _unstack_heads(o, n_pairs, tq):
    lane = lax.broadcasted_iota(jnp.int32, (tq, PAIR), 1)
    lo = lane < HEAD_DIM
    return [jnp.where(lo, o[2 * p * tq:(2 * p + 1) * tq], o[(2 * p + 1) * tq:(2 * p + 2) * tq])
            for p in range(n_pairs)]


def _sink_column(sink_ref, tq):
    blk = lax.broadcasted_iota(jnp.int32, (4 * tq, 1), 0) // tq
    col = jnp.full((4 * tq, 1), sink_ref[HEAD_PERM[3]], F32)
    for b in (2, 1, 0):
        col = jnp.where(blk == b, sink_ref[HEAD_PERM[b]], col)
    return col


def _swa_kernel(sink_ref, q_ref, k_ref, v_ref, kc_ref, vc_ref, o_ref, *, seq):
    n = pl.program_id(1)
    blk = SWA_BLOCK
    span = 3 * blk
    start = pl.multiple_of(jnp.clip((n - 1) * blk, 0, seq - span), blk)
    kl = k_ref[0, pl.ds(start, span), :]
    vl = v_ref[0, pl.ds(start, span), :]
    q = q_ref[0]
    qs = _stack_heads([q[:, :PAIR], q[:, PAIR:]])
    s_loc = _dot_nt(qs, kl)
    s_ctx = _dot_nt(qs, kc_ref[0])
    qpos = n * blk + lax.broadcasted_iota(jnp.int32, s_loc.shape, 0) % blk
    kpos = start + lax.broadcasted_iota(jnp.int32, s_loc.shape, 1)
    s_loc = jnp.where(jnp.abs(kpos - qpos) <= SWA_WINDOW, s_loc, NEG)
    sk = _sink_column(sink_ref, blk)
    m = jnp.maximum(jnp.maximum(jnp.max(s_loc, axis=-1, keepdims=True),
                                jnp.max(s_ctx, axis=-1, keepdims=True)), sk)
    p_loc = jnp.exp(s_loc - m)
    p_ctx = jnp.exp(s_ctx - m)
    denom = (jnp.sum(p_loc, axis=-1, keepdims=True) + jnp.sum(p_ctx, axis=-1, keepdims=True)
             + jnp.exp(sk - m))
    o = _dot(p_loc.astype(BF16), vl) + _dot(p_ctx.astype(BF16), vc_ref[0])
    o = o * (1.0 / denom)
    o_a, o_b = _unstack_heads(o, 2, blk)
    o_ref[0, :, :PAIR] = o_a.astype(o_ref.dtype)
    o_ref[0, :, PAIR:] = o_b.astype(o_ref.dtype)


def _swa(sink, q, k, v, kc, vc):
    b, s, _ = q.shape
    m = kc.shape[1]
    blk = SWA_BLOCK
    assert s % blk == 0 and s >= 3 * blk
    return pl.pallas_call(
        functools.partial(_swa_kernel, seq=s),
        grid=(b, s // blk),
        in_specs=[
            pl.BlockSpec(memory_space=pltpu.SMEM),
            pl.BlockSpec((1, blk, GROUP_WIDTH), lambda i, n: (i, n, 0)),
            pl.BlockSpec((1, s, PAIR), lambda i, n: (i, 0, 0)),
            pl.BlockSpec((1, s, PAIR), lambda i, n: (i, 0, 0)),
            pl.BlockSpec((1, m, PAIR), lambda i, n: (i, 0, 0)),
            pl.BlockSpec((1, m, PAIR), lambda i, n: (i, 0, 0)),
        ],
        out_specs=pl.BlockSpec((1, blk, GROUP_WIDTH), lambda i, n: (i, n, 0)),
        out_shape=jax.ShapeDtypeStruct((b, s, GROUP_WIDTH), BF16),
        compiler_params=_cparams("arbitrary", "arbitrary"),
        name="swa_attn",
    )(sink, q, k, v, kc, vc)


def _dense_attn_kernel(*refs, kv_pairs, n_src, has_sink, tq, tk):
    refs = list(refs)
    sink_ref = refs.pop(0) if has_sink else None
    q_ref = refs.pop(0)
    srcs = [(refs[2 * i], refs[2 * i + 1]) for i in range(n_src)]
    o_ref = refs[2 * n_src]
    q = q_ref[0]
    if kv_pairs == 1:
        units = [([q[:, :PAIR], q[:, PAIR:]], 0)]
    else:
        units = [([q[:, :PAIR]], 0), ([q[:, PAIR:]], 1)]
    outs = []
    for q_pairs, kv in units:
        qs = _stack_heads(q_pairs)
        nrow = qs.shape[0]
        if has_sink:
            m = _sink_column(sink_ref, tq)
            l = jnp.ones((nrow, 1), F32)
        else:
            m = jnp.full((nrow, 1), NEG, F32)
            l = jnp.zeros((nrow, 1), F32)
        acc = jnp.zeros((nrow, PAIR), F32)
        for k_ref, v_ref in srcs:
            nk = k_ref.shape[1]
            step = min(tk, nk)
            for c in range(nk // step):
                kch = k_ref[0, c * step:(c + 1) * step, kv * PAIR:(kv + 1) * PAIR]
                vch = v_ref[0, c * step:(c + 1) * step, kv * PAIR:(kv + 1) * PAIR]
                s = _dot_nt(qs, kch)
                m_new = jnp.maximum(m, jnp.max(s, axis=-1, keepdims=True))
                a = jnp.exp(m - m_new)
                p = jnp.exp(s - m_new)
                l = a * l + jnp.sum(p, axis=-1, keepdims=True)
                acc = a * acc + _dot(p.astype(BF16), vch)
                m = m_new
        o = acc * (1.0 / l)
        outs += _unstack_heads(o, len(q_pairs), tq)
    o_ref[0, :, :PAIR] = outs[0].astype(o_ref.dtype)
    o_ref[0, :, PAIR:] = outs[1].astype(o_ref.dtype)


def _dense_attn(q, srcs, *, kv_pairs, sink=None, tq=128, tk=512):
    b, s, _ = q.shape
    kvw = kv_pairs * PAIR
    has_sink = sink is not None
    in_specs, args = [], []
    if has_sink:
        in_specs.append(pl.BlockSpec(memory_space=pltpu.SMEM))
        args.append(sink)
    in_specs.append(pl.BlockSpec((1, tq, GROUP_WIDTH), lambda i, n: (i, n, 0)))
    args.append(q)
    for k, v in srcs:
        nk = k.shape[1]
        assert nk % min(tk, nk) == 0
        in_specs += [pl.BlockSpec((1, nk, kvw), lambda i, n: (i, 0, 0))] * 2
        args += [k, v]
    return pl.pallas_call(
        functools.partial(_dense_attn_kernel, kv_pairs=kv_pairs, n_src=len(srcs),
                          has_sink=has_sink, tq=tq, tk=tk),
        grid=(b, s // tq),
        in_specs=in_specs,
        out_specs=pl.BlockSpec((1, tq, GROUP_WIDTH), lambda i, n: (i, n, 0)),
        out_shape=jax.ShapeDtypeStruct((b, s, GROUP_WIDTH), BF16),
        compiler_params=_cparams("arbitrary", "arbitrary"),
        name="dense_attn",
    )(*args)


GQA_EXTRA_ROWS = 16


def _gqa_kernel(q_ref, k_ref, v_ref, kc_ref, vc_ref, o_ref, vt_sc, *, tq, tk):
    n = pl.program_id(1)
    seq = k_ref.shape[1]
    m_ctx = kc_ref.shape[1]

    @pl.when(n == 0)
    def _():
        for c in range(seq // PAIR):
            rows = slice(c * PAIR, (c + 1) * PAIR)
            vt_sc[:PAIR, rows] = v_ref[0, rows, :].astype(F32).T.astype(BF16)
        for c in range(m_ctx // PAIR):
            rows = slice(c * PAIR, (c + 1) * PAIR)
            vt_sc[:PAIR, seq + c * PAIR: seq + (c + 1) * PAIR] = vc_ref[0, rows, :].astype(F32).T.astype(BF16)
        ones_row = lax.broadcasted_iota(jnp.int32, (GQA_EXTRA_ROWS, seq + m_ctx), 0) == 0
        vt_sc[PAIR:, :] = jnp.where(ones_row, 1.0, 0.0).astype(BF16)

    q = q_ref[0]
    qs = _stack_heads([q[:, :PAIR], q[:, PAIR:]])
    chunks = [(k_ref, c * tk, tk, c * tk) for c in range(seq // tk)]
    chunks += [(kc_ref, c * min(tk, m_ctx), min(tk, m_ctx), seq + c * min(tk, m_ctx))
               for c in range(m_ctx // min(tk, m_ctx))]
    nrow = 4 * tq
    m = jnp.full((1, nrow), NEG, F32)
    acc = jnp.zeros((PAIR + GQA_EXTRA_ROWS, nrow), F32)
    for ref, lo, size, col in chunks:
        s_t = _dot_nt(ref[0, lo:lo + size, :], qs)
        m_new = jnp.maximum(m, jnp.max(s_t, axis=0, keepdims=True))
        p_t = jnp.exp2((s_t - m_new).astype(BF16))
        acc = jnp.exp2(m - m_new) * acc + _dot(vt_sc[:, col:col + size], p_t)
        m = m_new
    o_t = acc[:PAIR] * (1.0 / acc[PAIR:PAIR + 1])
    lo_rows = lax.broadcasted_iota(jnp.int32, (PAIR, tq), 0) < HEAD_DIM
    for p in range(2):
        pair_t = jnp.where(lo_rows, o_t[:, 2 * p * tq:(2 * p + 1) * tq],
                           o_t[:, (2 * p + 1) * tq:(2 * p + 2) * tq])
        o_ref[0, :, p * PAIR:(p + 1) * PAIR] = pair_t.T.astype(o_ref.dtype)


def _gqa(q, k, v, kc, vc, *, tq=256, tk=1024):
    b, s, _ = q.shape
    m = kc.shape[1]
    tq = _token_tile(s, tq)
    assert s % tk == 0 and s % PAIR == 0 and m % PAIR == 0 and m % min(tk, m) == 0
    full = lambda i, n: (i, 0, 0)
    return pl.pallas_call(
        functools.partial(_gqa_kernel, tq=tq, tk=tk),
        grid=(b, s // tq),
        in_specs=[
            pl.BlockSpec((1, tq, GROUP_WIDTH), lambda i, n: (i, n, 0)),
            pl.BlockSpec((1, s, PAIR), full),
            pl.BlockSpec((1, s, PAIR), full),
            pl.BlockSpec((1, m, PAIR), full),
            pl.BlockSpec((1, m, PAIR), full),
        ],
        out_specs=pl.BlockSpec((1, tq, GROUP_WIDTH), lambda i, n: (i, n, 0)),
        out_shape=jax.ShapeDtypeStruct((b, s, GROUP_WIDTH), BF16),
        scratch_shapes=[pltpu.VMEM((PAIR + GQA_EXTRA_ROWS, s + m), BF16)],
        compiler_params=_cparams("arbitrary", "arbitrary"),
        name="gqa_attn",
    )(q, k, v, kc, vc)


NA_RPB_ROWS = 2 * NA_ROWS


def _na_rpb_pairs(rpb):
    h = rpb.shape[0]
    t = jnp.pad(rpb.astype(F32), ((0, 0), (1, 1), (0, HEAD_DIM - rpb.shape[2])))
    pairs = jnp.concatenate([t[:, :-1], t[:, 1:]], axis=-1)
    assert pairs.shape == (h, NA_RPB_ROWS, PAIR)
    return jnp.roll(pairs, -(NA_COLS - 1), axis=-1)


def _na_build_bias(rpb_ref, bias_sc, kind, g, rows):
    w = GRID_W
    r0 = NA_QROWS * g
    start_row = jnp.clip(r0 - NA_ROWS // 2, 0, rows - NA_KROWS)
    qcol = lax.broadcasted_iota(jnp.int32, (w, PAIR), 0)
    lane = lax.broadcasted_iota(jnp.int32, (w, PAIR), 1)
    kcol = lane % w
    odd = (lane >= w).astype(jnp.int32)
    cs = jnp.clip(qcol - NA_COLS // 2, 0, w - NA_COLS)
    col_ok = (kcol >= cs) & (kcol < cs + NA_COLS)

    def body(t, carry):
        rr = t // (NA_KROWS // 2)
        a2 = t % (NA_KROWS // 2)
        r = r0 + rr
        rs = jnp.clip(r - NA_ROWS // 2, 0, rows - NA_ROWS)
        krow0 = start_row + 2 * a2
        krow = krow0 + odd
        ok = col_ok & (krow >= rs) & (krow < rs + NA_ROWS)
        e = jnp.clip(krow0 - r + NA_ROWS, 0, NA_RPB_ROWS - 1)
        for h in range(4):
            tile = jnp.broadcast_to(rpb_ref[h, pl.ds(e, 1), :], (w, PAIR))
            for bit in range(6):
                tile = jnp.where(((qcol >> bit) & 1) == 1, pltpu.roll(tile, 1 << bit, 1), tile)
            row0 = pl.multiple_of((h % 2) * NA_QROWS * w + rr * w, w)
            bias_sc[kind, h // 2, a2, pl.ds(row0, w), :] = jnp.where(ok, tile, NEG)
        return carry

    lax.fori_loop(0, NA_QROWS * (NA_KROWS // 2), body, 0)


def _na_kernel(q_ref, k_ref, v_ref, kc_ref, vc_ref, rpb_ref, o_ref, bias_sc, *, rows):
    g = pl.program_id(1)
    groups = rows // NA_QROWS
    tq = NA_QROWS * GRID_W
    span = NA_KROWS * GRID_W

    kind = jnp.where(g == 0, 0, jnp.where(g == groups - 1, 2, 1))

    @pl.when((pl.program_id(0) == 0) & ((g == 0) | (g == 1) | (g == groups - 1)))
    def _():
        _na_build_bias(rpb_ref, bias_sc, kind, g, rows)

    start = pl.multiple_of(jnp.clip(NA_QROWS * g - NA_ROWS // 2, 0, rows - NA_KROWS) * GRID_W, GRID_W)
    q = q_ref[0]
    for p in range(2):
        lanes = slice(p * PAIR, (p + 1) * PAIR)
        qs = _stack_heads([q[:, lanes]])
        kl = k_ref[0, pl.ds(start, span), lanes]
        vl = v_ref[0, pl.ds(start, span), lanes]
        bias = jnp.concatenate([bias_sc[kind, p, a2] for a2 in range(NA_KROWS // 2)], axis=1)
        s_nb = _dot_nt(qs, kl) + bias
        s_ctx = _dot_nt(qs, kc_ref[0, :, lanes])
        m = jnp.maximum(jnp.max(s_nb, axis=-1, keepdims=True), jnp.max(s_ctx, axis=-1, keepdims=True))
        p_nb = jnp.exp(s_nb - m)
        p_ctx = jnp.exp(s_ctx - m)
        denom = jnp.sum(p_nb, axis=-1, keepdims=True) + jnp.sum(p_ctx, axis=-1, keepdims=True)
        o = _dot(p_nb.astype(BF16), vl) + _dot(p_ctx.astype(BF16), vc_ref[0, :, lanes])
        o = o * (1.0 / denom)
        o_ref[0, :, lanes] = _unstack_heads(o, 1, tq)[0].astype(o_ref.dtype)


def _na(q, k, v, kc, vc, rpb_pairs):
    b, s, _ = q.shape
    m = kc.shape[1]
    rows = s // GRID_W
    groups = rows // NA_QROWS
    tq = NA_QROWS * GRID_W
    assert rows % NA_QROWS == 0 and groups >= 4
    return pl.pallas_call(
        functools.partial(_na_kernel, rows=rows),
        grid=(b, groups),
        in_specs=[
            pl.BlockSpec((1, tq, GROUP_WIDTH), lambda i, g: (i, g, 0)),
            pl.BlockSpec((1, s, GROUP_WIDTH), lambda i, g: (i, 0, 0)),
            pl.BlockSpec((1, s, GROUP_WIDTH), lambda i, g: (i, 0, 0)),
            pl.BlockSpec((1, m, GROUP_WIDTH), lambda i, g: (i, 0, 0)),
            pl.BlockSpec((1, m, GROUP_WIDTH), lambda i, g: (i, 0, 0)),
            pl.BlockSpec((4, NA_RPB_ROWS, PAIR), lambda i, g: (0, 0, 0)),
        ],
        out_specs=pl.BlockSpec((1, tq, GROUP_WIDTH), lambda i, g: (i, g, 0)),
        out_shape=jax.ShapeDtypeStruct((b, s, GROUP_WIDTH), BF16),
        scratch_shapes=[pltpu.VMEM((3, 2, NA_KROWS // 2, 2 * tq, PAIR), F32)],
        compiler_params=_cparams("arbitrary", "arbitrary"),
        name="na_attn",
    )(q, k, v, kc, vc, rpb_pairs)


def _ssd_kernel(uf_ref, ub_ref, dtf_ref, dtb_ref, dtbias_ref, alog_ref, dskip_ref, h0f_ref, h0b_ref,
                yf_ref, yb_ref, hf_ref, hb_ref, hf_sc, hb_sc, *, nc):
    i = pl.program_id(1)
    q = SSD_CHUNK

    @pl.when(i == 0)
    def _():
        hf_sc[...] = h0f_ref[0]
        hb_sc[...] = h0b_ref[0]

    ii = lax.broadcasted_iota(jnp.int32, (q, q), 0)
    jj = lax.broadcasted_iota(jnp.int32, (q, q), 1)
    lo_lanes = jj < HEAD_DIM
    a_coef = -jnp.exp(alog_ref[...])

    def chunk(u, dt_raw, col0, reverse, h_sc):
        causal = (jj >= ii) if reverse else (jj <= ii)
        dt = _softplus(dt_raw + dtbias_ref[...])
        tri = jnp.where(causal, 1.0, 0.0).astype(BF16)
        a_hi, a_mid, a_lo = _split3(dt * a_coef)
        cum = _dot(tri, a_hi) + _dot(tri, a_mid) + _dot(tri, a_lo)
        cum_t = cum.T
        dt_t = dt.T
        end = cum[0:1, :] if reverse else cum[q - 1:q, :]
        xs = u[:, :2 * PAIR]
        ys = []
        for k in range(2):
            bk = u[:, 2 * PAIR + k * SSD_STATE: 2 * PAIR + (k + 1) * SSD_STATE]
            ck = u[:, 2 * PAIR + 2 * SSD_STATE + k * SSD_STATE: 2 * PAIR + 2 * SSD_STATE + (k + 1) * SSD_STATE]
            xk_b = xs[:, k * PAIR:(k + 1) * PAIR]
            xk = xk_b.astype(F32)
            ck_b = ck
            cb = _dot_nt(ck_b, bk)
            c0 = col0 + 2 * k
            cols = []
            for r in range(2):
                c = c0 + r
                seg = cum[:, c:c + 1] - cum_t[c:c + 1, :]
                lmat = jnp.exp(jnp.where(causal, seg, NEG))
                att = (cb * lmat * dt_t[c:c + 1, :]).astype(BF16)
                cols.append(_dot(att, xk_b))
            y_intra = jnp.where(lo_lanes, cols[0], cols[1])
            e_in = jnp.where(lo_lanes, jnp.exp(cum[:, c0:c0 + 1]), jnp.exp(cum[:, c0 + 1:c0 + 2]))
            h_t = h_sc[k]
            y_state = _dot(ck_b, h_t.astype(BF16)) * e_in
            w0 = jnp.exp(end[:, c0:c0 + 1] - cum[:, c0:c0 + 1]) * dt[:, c0:c0 + 1]
            w1 = jnp.exp(end[:, c0 + 1:c0 + 2] - cum[:, c0 + 1:c0 + 2]) * dt[:, c0 + 1:c0 + 2]
            xw = (xk * jnp.where(lo_lanes, w0, w1)).astype(BF16)
            st = _dot(bk.astype(F32).T.astype(BF16), xw)
            decay = jnp.where(lo_lanes[0:1], jnp.exp(end[:, c0:c0 + 1]), jnp.exp(end[:, c0 + 1:c0 + 2]))
            h_sc[k] = h_t * decay + st
            ys.append(y_intra + y_state)
        return ys, xs

    ys, xs = chunk(uf_ref[0], dtf_ref[0], 0, False, hf_sc)
    dskip = dskip_ref[...]
    for k in range(2):
        lanes = slice(k * PAIR, (k + 1) * PAIR)
        yf_ref[0, :, lanes] = ys[k] + dskip[:, lanes] * xs[:, lanes].astype(F32)

    ys, _ = chunk(ub_ref[0], dtb_ref[0], 4, True, hb_sc)
    for k in range(2):
        yb_ref[0, :, k * PAIR:(k + 1) * PAIR] = ys[k]

    @pl.when(i == nc - 1)
    def _():
        hf_ref[0] = hf_sc[...]
        hb_ref[0] = hb_sc[...]


def _ssd(u, dt, dt_bias, a_log, dskip, h0f, h0b):
    b, length, ch = u.shape
    q = SSD_CHUNK
    nc = length // q
    fwd = lambda i, c: (i, c, 0)
    bwd = lambda i, c: (i, nc - 1 - c, 0)
    const = lambda i, c: (0, 0)
    state = lambda i, c: (i, 0, 0, 0)
    state_shape = (b, 2, SSD_STATE, PAIR)
    return pl.pallas_call(
        functools.partial(_ssd_kernel, nc=nc),
        grid=(b, nc),
        in_specs=[
            pl.BlockSpec((1, q, ch), fwd),
            pl.BlockSpec((1, q, ch), bwd),
            pl.BlockSpec((1, q, DT_PAD), fwd),
            pl.BlockSpec((1, q, DT_PAD), bwd),
            pl.BlockSpec((1, DT_PAD), const),
            pl.BlockSpec((1, DT_PAD), const),
            pl.BlockSpec((1, GROUP_WIDTH), const),
            pl.BlockSpec((1, 2, SSD_STATE, PAIR), state),
            pl.BlockSpec((1, 2, SSD_STATE, PAIR), state),
        ],
        out_specs=[
            pl.BlockSpec((1, q, GROUP_WIDTH), fwd),
            pl.BlockSpec((1, q, GROUP_WIDTH), bwd),
            pl.BlockSpec((1, 2, SSD_STATE, PAIR), state),
            pl.BlockSpec((1, 2, SSD_STATE, PAIR), state),
        ],
        out_shape=[
            jax.ShapeDtypeStruct((b, length, GROUP_WIDTH), F32),
            jax.ShapeDtypeStruct((b, length, GROUP_WIDTH), F32),
            jax.ShapeDtypeStruct(state_shape, F32),
            jax.ShapeDtypeStruct(state_shape, F32),
        ],
        scratch_shapes=[pltpu.VMEM((2, SSD_STATE, PAIR), F32), pltpu.VMEM((2, SSD_STATE, PAIR), F32)],
        compiler_params=_cparams("arbitrary", "arbitrary"),
        name="ssd_scan",
    )(u, u, dt, dt, dt_bias, a_log, dskip, h0f, h0b)


def _merge_kernel(x_ref, mod_ref, oa_ref, yf_ref, yb_ref, z_ref, oc_ref, od_ref, ng_ref, w_ref,
                  g_ref, b_ref, o_ref, *, alpha):
    ob = (yf_ref[...] + yb_ref[...]) * _silu(z_ref[...].astype(F32))
    parts = (oa_ref[...].astype(F32), ob, oc_ref[...].astype(F32), od_ref[...].astype(F32))
    acc = None
    for gi, y in enumerate(parts):
        rows = slice(gi * GROUP_WIDTH, (gi + 1) * GROUP_WIDTH)
        ms = jnp.mean(y * y, axis=-1, keepdims=True)
        yn = (y * lax.rsqrt(ms + EPS) * ng_ref[:, rows]).astype(BF16)
        term = _dot(yn, w_ref[rows, :])
        acc = term if acc is None else acc + term
    y = alpha * x_ref[...] + mod_ref[0, 5:6, :] * acc
    o_ref[...] = _layer_norm(y, g_ref[...], b_ref[...])


def _merge(xt, mod, oa, yf, yb, z, oc, od, norm_g, w_out, ln_g, ln_b, *, tokens_per_mod, alpha, tm):
    t, d = xt.shape
    tiles_per_mod = tokens_per_mod // tm
    row = lambda i: (i, 0)
    const = lambda i: (0, 0)
    grp = pl.BlockSpec((tm, GROUP_WIDTH), row)
    return pl.pallas_call(
        functools.partial(_merge_kernel, alpha=alpha),
        grid=(t // tm,),
        in_specs=[
            pl.BlockSpec((tm, d), row),
            pl.BlockSpec((1, N_MOD, d), lambda i: (i // tiles_per_mod, 0, 0)),
            grp, grp, grp, grp, grp, grp,
            pl.BlockSpec((1, d), const),
            pl.BlockSpec((d, d), const),
            pl.BlockSpec((1, d), const),
            pl.BlockSpec((1, d), const),
        ],
        out_specs=pl.BlockSpec((tm, d), row),
        out_shape=jax.ShapeDtypeStruct((t, d), F32),
        compiler_params=_cparams("arbitrary"),
        name="merge_out",
    )(xt, mod, oa, yf, yb, z, oc, od, norm_g, w_out, ln_g.reshape(1, d), ln_b.reshape(1, d))


def _head_block_index(start, perm):
    return np.concatenate([np.arange(start + h * HEAD_DIM, start + (h + 1) * HEAD_DIM) for h in perm])


def _inproj_columns():
    cols = [
        _head_block_index(0, HEAD_PERM), np.arange(256, 1536),
        _head_block_index(1544, HEAD_PERM), np.arange(1800, 2824),
    ]
    return np.concatenate(cols), np.arange(1536, 1544)


def _outproj_rows():
    return np.concatenate([
        _head_block_index(0, HEAD_PERM), np.arange(256, 512),
        _head_block_index(512, HEAD_PERM), np.arange(768, 1024),
    ])


def _rope_tables(seq):
    t = np.arange(seq)
    quarter = HEAD_DIM // 4
    inv = ROPE_BASE ** (-jnp.arange(quarter, dtype=F32) / quarter)
    a_row = jnp.asarray(t // GRID_W, F32)[:, None] * inv
    a_col = jnp.asarray(t % GRID_W, F32)[:, None] * inv
    cos = jnp.concatenate([jnp.cos(a_row), jnp.cos(a_row), jnp.cos(a_col), jnp.cos(a_col)], axis=1)
    sin = jnp.concatenate([-jnp.sin(a_row), jnp.sin(a_row), -jnp.sin(a_col), jnp.sin(a_col)], axis=1)
    return jnp.tile(cos, (1, 2)), jnp.tile(sin, (1, 2))


def _pad_lanes(v, width):
    v = v.reshape(1, -1).astype(F32)
    return jnp.pad(v, ((0, 0), (0, width - v.shape[1])))


def _token_tile(n, cap):
    t = cap
    while n % t:
        t //= 2
    return t


def kernel(x, c, ctx, c_ctx, ada_w, ada_b, ln_g, ln_b, ffn1_w_in, ffn1_w_out, mix_w_in, mix_w_out,
           mix_norm_g, swa_sink, ssd_conv_w, ssd_conv_b, ssd_dt_bias, ssd_A_log, ssd_D,
           gqa_q_norm, gqa_k_norm, na_rpb, ffn2_w_in, ffn2_w_out):
    bsz, seq, d = x.shape
    m_ctx = ctx.shape[1]
    depth = ada_w.shape[0]
    alpha = float((2 * depth) ** 0.25)
    mod_rows = 16 * ((bsz + 1 + 15) // 16)
    cc = jnp.concatenate([c, c_ctx[None], jnp.zeros((mod_rows - bsz - 1, d), F32)], axis=0)
    mods = _ada(cc, ada_w, ada_b)

    cos_x, sin_x = _rope_tables(seq)
    tm_x = _token_tile(seq, 512)
    tm_c = _token_tile(m_ctx, 256)
    cos_c = jnp.ones((tm_c, PAIR), F32)
    sin_c = jnp.zeros((tm_c, PAIR), F32)
    blockdiag = jnp.asarray(np.kron(np.eye(2), np.ones((HEAD_DIM, HEAD_DIM))), BF16)
    main_cols, dt_cols = _inproj_columns()
    out_rows = _outproj_rows()

    xt = x.reshape(bsz * seq, d)
    ct = ctx.reshape(bsz * m_ctx, d)
    zero_state = jnp.zeros((bsz, 2, SSD_STATE, PAIR), F32)

    for l in range(depth):
        last = l == depth - 1
        mod_x = mods[l, :bsz].reshape(bsz, N_MOD, d)
        mod_c = mods[l, bsz:bsz + 1].reshape(1, N_MOD, d)
        w1_in, w1_out = ffn1_w_in[l].astype(BF16), ffn1_w_out[l].astype(BF16)
        w2_in, w2_out = ffn2_w_in[l].astype(BF16), ffn2_w_out[l].astype(BF16)
        w_mix = mix_w_in[l]
        w_mix = jnp.concatenate(
            [w_mix[:, main_cols], jnp.pad(w_mix[:, dt_cols], ((0, 0), (0, DT_PAD - dt_cols.size)))],
            axis=1).astype(BF16)
        w_o = mix_w_out[l][out_rows].astype(BF16)
        norm_g = mix_norm_g[l][out_rows].reshape(1, d)
        qg = jnp.tile(gqa_q_norm[l], 2).reshape(1, PAIR)
        kg = jnp.tile(gqa_k_norm[l], 2).reshape(1, PAIR)
        conv_w = jnp.pad(ssd_conv_w[l].reshape(SSD_CONV, SSD_CONV_CH), ((0, 8 - SSD_CONV), (0, 0)))
        conv_b = ssd_conv_b[l].reshape(1, SSD_CONV_CH)
        dt_bias = _pad_lanes(ssd_dt_bias[l], DT_PAD)
        a_log = _pad_lanes(ssd_A_log[l], DT_PAD)
        dskip = jnp.repeat(ssd_D[l], HEAD_DIM).reshape(1, GROUP_WIDTH)
        sink = swa_sink[l].astype(F32)
        rpb_pairs = _na_rpb_pairs(na_rpb[l])

        ffn_x = functools.partial(_ffn, tokens_per_mod=seq, alpha=alpha, tm=tm_x)
        ffn_c = functools.partial(_ffn, tokens_per_mod=bsz * m_ctx, alpha=alpha, tm=tm_c)

        xt = ffn_x(xt, mod_x, (0, 1, 2), w1_in, w1_out, ln_g[l, 0], ln_b[l, 0])
        ct = ffn_c(ct, mod_c, (0, 1, 2), w1_in, w1_out, ln_g[l, 0], ln_b[l, 0])

        px = _inproj(xt, mod_x, w_mix, cos_x, sin_x, qg, kg, blockdiag, conv_w, conv_b,
                     tokens_per_mod=seq, seq_len=seq, tm=tm_x, gqa_q_scale=Q_SCALE * LOG2E)
        pc = _inproj(ct, mod_c, w_mix, cos_c, sin_c, qg, kg, blockdiag, conv_w, conv_b,
                     tokens_per_mod=bsz * m_ctx, seq_len=m_ctx, tm=tm_c, gqa_q_scale=Q_SCALE)
        aq, ak, av, bz, bu, bdt, cq, ck, cv, dq, dk, dv = [
            t.reshape(bsz, seq, t.shape[-1]) for t in px]
        aq_c, ak_c, av_c, bz_c, bu_c, bdt_c, cq_c, ck_c, cv_c, dq_c, dk_c, dv_c = [
            t.reshape(bsz, m_ctx, t.shape[-1]) for t in pc]

        ssd = functools.partial(_ssd, dt_bias=dt_bias, a_log=a_log, dskip=dskip)
        yf_c, yb_c, hf_c, hb_c = ssd(bu_c, bdt_c, h0f=zero_state, h0b=zero_state)
        yf, yb, _, _ = ssd(bu, bdt, h0f=hf_c, h0b=hb_c)

        oa = _swa(sink, aq, ak, av, ak_c, av_c)
        oc = _gqa(cq, ck, cv, ck_c, cv_c)
        od = _na(dq, dk, dv, dk_c, dv_c, rpb_pairs)

        flat = lambda t: t.reshape(-1, t.shape[-1])
        xt = _merge(xt, mod_x, flat(oa), flat(yf), flat(yb), flat(bz), flat(oc), flat(od), norm_g, w_o,
                    ln_g[l, 1], ln_b[l, 1], tokens_per_mod=seq, alpha=alpha, tm=tm_x)
        xt = ffn_x(xt, mod_x, (6, 7, 8), w2_in, w2_out, ln_g[l, 2], ln_b[l, 2])

        if not last:
            tq_c = _token_tile(m_ctx, 128)
            oa_c = _dense_attn(aq_c, [(ak_c, av_c)], kv_pairs=1, sink=sink, tq=tq_c)
            oc_c = _dense_attn(cq_c, [(ck_c, cv_c)], kv_pairs=1, tq=tq_c)
            od_c = _dense_attn(dq_c, [(dk_c, dv_c)], kv_pairs=2, tq=tq_c)
            ct = _merge(ct, mod_c, flat(oa_c), flat(yf_c), flat(yb_c), flat(bz_c), flat(oc_c), flat(od_c),
                        norm_g, w_o, ln_g[l, 1], ln_b[l, 1], tokens_per_mod=bsz * m_ctx, alpha=alpha,
                        tm=tm_c)
            ct = ffn_c(ct, mod_c, (6, 7, 8), w2_in, w2_out, ln_g[l, 2], ln_b[l, 2])

    return xt.reshape(bsz, seq, d)
```

```python
import functools

import numpy as np
import jax
import jax.numpy as jnp
from jax import lax
from jax.experimental import pallas as pl
from jax.experimental.pallas import tpu as pltpu

F32 = jnp.float32
BF16 = jnp.bfloat16

HEAD_DIM = 64
PAIR = 2 * HEAD_DIM
GROUP_WIDTH = 256
GRID_W = 64
N_MOD = 9
SWA_WINDOW = 128
SWA_BLOCK = 128
SSD_CHUNK = 128
SSD_STATE = 128
SSD_CONV = 5
CONV_HALO = 8
FFN_SUBTILE = 256
SSD_CONV_CH = 768
NA_ROWS = 8
NA_COLS = 16
NA_QROWS = 4
NA_KROWS = 12
ROPE_BASE = 10000.0
EPS = 1e-5
NEG = -1e30
Q_SCALE = HEAD_DIM ** -0.5
LOG2E = 1.4426950408889634
DT_PAD = 128
HEAD_PERM = (0, 2, 1, 3)

VMEM_LIMIT_BYTES = 56 * 1024 * 1024


def _cparams(*sem):
    return pltpu.CompilerParams(dimension_semantics=sem, vmem_limit_bytes=VMEM_LIMIT_BYTES)


def _dot(a, b):
    return jnp.dot(a, b, preferred_element_type=F32)


def _dot_nt(a, b):
    return lax.dot_general(a, b, (((1,), (1,)), ((), ())), preferred_element_type=F32)


def _sigmoid(x):
    return 1.0 / (1.0 + jnp.exp(-x))


def _silu(x):
    return x * _sigmoid(x)


def _softplus(x):
    return jnp.maximum(x, 0.0) + jnp.log(1.0 + jnp.exp(-jnp.abs(x)))


def _split3(a):
    hi = a.astype(BF16)
    r1 = a - hi.astype(F32)
    mid = r1.astype(BF16)
    lo = (r1 - mid.astype(F32)).astype(BF16)
    return hi, mid, lo


def _layer_norm(y, g, b):
    mu = jnp.mean(y, axis=-1, keepdims=True)
    d = y - mu
    var = jnp.mean(d * d, axis=-1, keepdims=True)
    return d * lax.rsqrt(var + EPS) * g + b


def _ada_kernel(c_ref, w_ref, b_ref, o_ref):
    s = _silu(c_ref[...])
    s_hi = s.astype(BF16)
    s_lo = (s - s_hi.astype(F32)).astype(BF16)
    w = w_ref[0]
    w_hi = w.astype(BF16)
    w_lo = (w - w_hi.astype(F32)).astype(BF16)
    o_ref[0] = _dot(s_hi, w_hi) + _dot(s_lo, w_hi) + _dot(s_hi, w_lo) + b_ref[0]


def _ada(cc, ada_w, ada_b):
    depth, d, n = ada_w.shape
    rows = cc.shape[0]
    tn = 1024
    return pl.pallas_call(
        _ada_kernel,
        grid=(depth, n // tn),
        in_specs=[
            pl.BlockSpec((rows, d), lambda l, j: (0, 0)),
            pl.BlockSpec((1, d, tn), lambda l, j: (l, 0, j)),
            pl.BlockSpec((1, 1, tn), lambda l, j: (l, 0, j)),
        ],
        out_specs=pl.BlockSpec((1, rows, tn), lambda l, j: (l, 0, j)),
        out_shape=jax.ShapeDtypeStruct((depth, rows, n), F32),
        compiler_params=_cparams("arbitrary", "arbitrary"),
        name="ada_mod",
    )(cc, ada_w, ada_b.reshape(depth, 1, n))


def _ffn_rows(x, mod_ref, rows, wa_ref, wu_ref, wo_ref, g, b, alpha):
    r_shift, r_scale, r_gate = rows
    h = x * (1.0 + mod_ref[0, r_scale:r_scale + 1, :]) + mod_ref[0, r_shift:r_shift + 1, :]
    h = h.astype(BF16)
    a = _dot(h, wa_ref[...])
    u = _dot(h, wu_ref[...])
    gated = (_silu(a) * u).astype(BF16)
    f = _dot(gated, wo_ref[...])
    y = alpha * x + (0.5 * mod_ref[0, r_gate:r_gate + 1, :]) * f
    return _layer_norm(y, g, b)


def _mixer_rows(x, mod_ref, parts, ng_ref, w_ref, g, b, alpha):
    acc = None
    for gi, y in enumerate(parts):
        rows = slice(gi * GROUP_WIDTH, (gi + 1) * GROUP_WIDTH)
        ms = jnp.mean(y * y, axis=-1, keepdims=True)
        yn = (y * lax.rsqrt(ms + EPS) * ng_ref[:, rows]).astype(BF16)
        term = _dot(yn, w_ref[rows, :])
        acc = term if acc is None else acc + term
    return _layer_norm(alpha * x + mod_ref[0, 5:6, :] * acc, g, b)


def _subtiles(tm):
    sub = FFN_SUBTILE if tm % FFN_SUBTILE == 0 else tm
    return [slice(r0, r0 + sub) for r0 in range(0, tm, sub)]


def _ffn_kernel(x_ref, mod_ref, wa_ref, wu_ref, wo_ref, g_ref, b_ref, o_ref, *, rows, alpha):
    for r in _subtiles(x_ref.shape[0]):
        o_ref[r, :] = _ffn_rows(x_ref[r, :], mod_ref, rows, wa_ref, wu_ref, wo_ref,
                                g_ref[...], b_ref[...], alpha)


def _mixer_ffn_kernel(x_ref, mod_ref, oa_ref, yf_ref, yb_ref, z_ref, oc_ref, od_ref, ng_ref, wm_ref,
                      g1_ref, b1_ref, wa_ref, wu_ref, wo_ref, g2_ref, b2_ref, o_ref, *, alpha):
    for r in _subtiles(x_ref.shape[0]):
        ob = (yf_ref[r, :] + yb_ref[r, :]) * _silu(z_ref[r, :].astype(F32))
        parts = (oa_ref[r, :].astype(F32), ob, oc_ref[r, :].astype(F32), od_ref[r, :].astype(F32))
        x_mid = _mixer_rows(x_ref[r, :], mod_ref, parts, ng_ref, wm_ref, g1_ref[...], b1_ref[...], alpha)
        o_ref[r, :] = _ffn_rows(x_mid, mod_ref, (6, 7, 8), wa_ref, wu_ref, wo_ref,
                                g2_ref[...], b2_ref[...], alpha)


def _resident(shape, index_map):
    return pl.BlockSpec(shape, index_map, pipeline_mode=pl.Buffered(1))


def _ffn(xt, mod, rows, w_in, w_out, ln_g, ln_b, *, tokens_per_mod, alpha, tm):
    t, d = xt.shape
    dff = w_out.shape[0]
    tiles_per_mod = tokens_per_mod // tm
    return pl.pallas_call(
        functools.partial(_ffn_kernel, rows=rows, alpha=alpha),
        grid=(t // tm,),
        in_specs=[
            pl.BlockSpec((tm, d), lambda i: (i, 0)),
            pl.BlockSpec((1, N_MOD, d), lambda i: (i // tiles_per_mod, 0, 0)),
            _resident((d, dff), lambda i: (0, 0)),
            _resident((d, dff), lambda i: (0, 1)),
            _resident((dff, d), lambda i: (0, 0)),
            pl.BlockSpec((1, d), lambda i: (0, 0)),
            pl.BlockSpec((1, d), lambda i: (0, 0)),
        ],
        out_specs=pl.BlockSpec((tm, d), lambda i: (i, 0)),
        out_shape=jax.ShapeDtypeStruct((t, d), F32),
        compiler_params=_cparams("arbitrary"),
        name="ffn",
    )(xt, mod, w_in, w_in, w_out, ln_g.reshape(1, d), ln_b.reshape(1, d))


def _mixer_ffn(xt, mod, oa, yf, yb, z, oc, od, norm_g, w_mix_out, ln1_g, ln1_b, w_in, w_out, ln2_g, ln2_b,
               *, tokens_per_mod, alpha, tm):
    t, d = xt.shape
    dff = w_out.shape[0]
    tiles_per_mod = tokens_per_mod // tm
    row = lambda i: (i, 0)
    const = lambda i: (0, 0)
    grp = pl.BlockSpec((tm, GROUP_WIDTH), row)
    vec = pl.BlockSpec((1, d), const)
    return pl.pallas_call(
        functools.partial(_mixer_ffn_kernel, alpha=alpha),
        grid=(t // tm,),
        in_specs=[
            pl.BlockSpec((tm, d), row),
            pl.BlockSpec((1, N_MOD, d), lambda i: (i // tiles_per_mod, 0, 0)),
            grp, grp, grp, grp, grp, grp,
            vec,
            _resident((d, d), const),
            vec, vec,
            _resident((d, dff), const),
            _resident((d, dff), lambda i: (0, 1)),
            _resident((dff, d), const),
            vec, vec,
        ],
        out_specs=pl.BlockSpec((tm, d), row),
        out_shape=jax.ShapeDtypeStruct((t, d), F32),
        compiler_params=_cparams("arbitrary"),
        name="mixer_ffn",
    )(xt, mod, oa, yf, yb, z, oc, od, norm_g, w_mix_out, ln1_g.reshape(1, d), ln1_b.reshape(1, d),
      w_in, w_in, w_out, ln2_g.reshape(1, d), ln2_b.reshape(1, d))


_P_AQ, _P_AK, _P_AV, _P_Z, _P_XBC = 0, 256, 384, 512, 768
_P_CQ, _P_CK, _P_CV, _P_DQ, _P_DK, _P_DV, _P_DT, _P_END = 1536, 1792, 1920, 2048, 2304, 2560, 2816, 2944


def _rope_pair(t, cos, sin):
    lane = lax.broadcasted_iota(jnp.int32, t.shape, 1)
    up = pltpu.roll(t, HEAD_DIM // 4, 1)
    dn = pltpu.roll(t, PAIR - HEAD_DIM // 4, 1)
    partner = jnp.where((lane % 32) < 16, dn, up)
    return t * cos + partner * sin


def _rms_pair(t, g, blockdiag):
    sq = t * t
    hi = sq.astype(BF16)
    lo = (sq - hi.astype(F32)).astype(BF16)
    ss = _dot(hi, blockdiag) + _dot(lo, blockdiag)
    return t * lax.rsqrt(ss * (1.0 / HEAD_DIM) + EPS) * g


def _inproj_kernel(x_ref, xp_ref, xn_ref, mod_ref, w_ref, cos_ref, sin_ref, qg_ref, kg_ref, bd_ref,
                   cw_ref, cb_ref,
                   aq_ref, ak_ref, av_ref, z_ref, u_ref, dt_ref,
                   cq_ref, ck_ref, cv_ref, dq_ref, dk_ref, dv_ref, ext_sc, *, gqa_q_scale, tiles_per_seq):
    tm = x_ref.shape[0]
    halo = xp_ref.shape[0]
    pos = pl.program_id(0) % tiles_per_seq
    x_ext = jnp.concatenate([xp_ref[...], x_ref[...], xn_ref[...]], axis=0)
    h = (x_ext * (1.0 + mod_ref[0, 4:5, :]) + mod_ref[0, 3:4, :]).astype(BF16)
    y_ext = _dot(h, w_ref[...])
    y = y_ext[halo:halo + tm]
    cos = cos_ref[...]
    sin = sin_ref[...]
    bd = bd_ref[...]

    xbc = y_ext[:, _P_XBC:_P_XBC + SSD_CONV_CH]
    ext_sc[:halo] = jnp.where(pos == 0, 0.0, xbc[:halo])
    ext_sc[halo:halo + tm] = xbc[halo:halo + tm]
    ext_sc[halo + tm:] = jnp.where(pos == tiles_per_seq - 1, 0.0, xbc[halo + tm:])
    conv = cb_ref[...]
    for k in range(SSD_CONV):
        lo = halo - SSD_CONV // 2 + k
        conv = conv + cw_ref[k:k + 1, :] * ext_sc[lo:lo + tm, :]
    u_ref[...] = _silu(conv).astype(u_ref.dtype)

    def pairs(lo, n):
        return [y[:, lo + PAIR * p: lo + PAIR * (p + 1)] for p in range(n)]

    def put(ref, parts, scale=None):
        for p, part in enumerate(parts):
            if scale is not None:
                part = part * scale
            ref[:, PAIR * p: PAIR * (p + 1)] = part.astype(ref.dtype)

    put(aq_ref, [_rope_pair(t, cos, sin) for t in pairs(_P_AQ, 2)], Q_SCALE)
    put(ak_ref, [_rope_pair(t, cos, sin) for t in pairs(_P_AK, 1)])
    put(av_ref, pairs(_P_AV, 1))
    put(z_ref, pairs(_P_Z, 2))
    put(dt_ref, pairs(_P_DT, 1))
    put(cq_ref, [_rope_pair(_rms_pair(t, qg_ref[...], bd), cos, sin) for t in pairs(_P_CQ, 2)], gqa_q_scale)
    put(ck_ref, [_rope_pair(_rms_pair(t, kg_ref[...], bd), cos, sin) for t in pairs(_P_CK, 1)])
    put(cv_ref, pairs(_P_CV, 1))
    put(dq_ref, pairs(_P_DQ, 2), Q_SCALE)
    put(dk_ref, pairs(_P_DK, 2))
    put(dv_ref, pairs(_P_DV, 2))


def _inproj(xt, mod, w, cos, sin, qg, kg, blockdiag, conv_w, conv_b, *, tokens_per_mod, seq_len, tm,
            gqa_q_scale):
    t, d = xt.shape
    tiles_per_mod = tokens_per_mod // tm
    table_tiles = cos.shape[0] // tm
    widths = (256, 128, 128, 256, SSD_CONV_CH, DT_PAD, 256, 128, 128, 256, 256, 256)
    dtypes = (BF16,) * 5 + (F32,) + (BF16,) * 6
    per = tm // CONV_HALO
    last_halo = t // CONV_HALO - 1
    row = lambda i: (i, 0)
    const = lambda i: (0, 0)
    return pl.pallas_call(
        functools.partial(_inproj_kernel, gqa_q_scale=gqa_q_scale, tiles_per_seq=seq_len // tm),
        grid=(t // tm,),
        in_specs=[
            pl.BlockSpec((tm, d), row),
            pl.BlockSpec((CONV_HALO, d), lambda i: (jnp.maximum(i * per - 1, 0), 0)),
            pl.BlockSpec((CONV_HALO, d), lambda i: (jnp.minimum((i + 1) * per, last_halo), 0)),
            pl.BlockSpec((1, N_MOD, d), lambda i: (i // tiles_per_mod, 0, 0)),
            _resident((d, _P_END), const),
            pl.BlockSpec((tm, PAIR), lambda i: (i % table_tiles, 0)),
            pl.BlockSpec((tm, PAIR), lambda i: (i % table_tiles, 0)),
            pl.BlockSpec((1, PAIR), const),
            pl.BlockSpec((1, PAIR), const),
            pl.BlockSpec((PAIR, PAIR), const),
            pl.BlockSpec((8, SSD_CONV_CH), const),
            pl.BlockSpec((1, SSD_CONV_CH), const),
        ],
        out_specs=[pl.BlockSpec((tm, wd), row) for wd in widths],
        out_shape=[jax.ShapeDtypeStruct((t, wd), dt) for wd, dt in zip(widths, dtypes)],
        scratch_shapes=[pltpu.VMEM((tm + 2 * CONV_HALO, SSD_CONV_CH), F32)],
        compiler_params=_cparams("arbitrary"),
        name="inproj",
    )(xt, xt, xt, mod, w, cos, sin, qg, kg, blockdiag, conv_w, conv_b)


def _stack_heads(q_pairs):
    lane = lax.broadcasted_iota(jnp.int32, q_pairs[0].shape, 1)
    lo = lane < HEAD_DIM
    zero = jnp.zeros_like(q_pairs[0])
    blocks = []
    for qp in q_pairs:
        blocks.append(jnp.where(lo, qp, zero))
        blocks.append(jnp.where(lo, zero, qp))
    return jnp.concatenate(blocks, axis=0)


def _unstack_heads(o, n_pairs, tq):
    lane = lax.broadcasted_iota(jnp.int32, (tq, PAIR), 1)
    lo = lane < HEAD_DIM
    return [jnp.where(lo, o[2 * p * tq:(2 * p + 1) * tq], o[(2 * p + 1) * tq:(2 * p + 2) * tq])
            for p in range(n_pairs)]


def _sink_column(sink_ref, tq):
    blk = lax.broadcasted_iota(jnp.int32, (4 * tq, 1), 0) // tq
    col = jnp.full((4 * tq, 1), sink_ref[HEAD_PERM[3]], F32)
    for b in (2, 1, 0):
        col = jnp.where(blk == b, sink_ref[HEAD_PERM[b]], col)
    return col


def _swa_kernel(sink_ref, q_ref, k_ref, v_ref, kc_ref, vc_ref, o_ref, *, seq, nblk):
    blk = SWA_BLOCK
    span = 3 * blk
    sk = _sink_column(sink_ref, blk)
    for j in range(nblk):
        n = pl.program_id(1) * nblk + j
        rows = slice(j * blk, (j + 1) * blk)
        start = pl.multiple_of(jnp.clip((n - 1) * blk, 0, seq - span), blk)
        kl = k_ref[0, pl.ds(start, span), :]
        vl = v_ref[0, pl.ds(start, span), :]
        q = q_ref[0, rows, :]
        qs = _stack_heads([q[:, :PAIR], q[:, PAIR:]])
        s_loc = _dot_nt(qs, kl)
        s_ctx = _dot_nt(qs, kc_ref[0])
        qpos = n * blk + lax.broadcasted_iota(jnp.int32, s_loc.shape, 0) % blk
        kpos = start + lax.broadcasted_iota(jnp.int32, s_loc.shape, 1)
        s_loc = jnp.where(jnp.abs(kpos - qpos) <= SWA_WINDOW, s_loc, NEG)
        m = jnp.maximum(jnp.maximum(jnp.max(s_loc, axis=-1, keepdims=True),
                                    jnp.max(s_ctx, axis=-1, keepdims=True)), sk)
        p_loc = jnp.exp(s_loc - m)
        p_ctx = jnp.exp(s_ctx - m)
        denom = (jnp.sum(p_loc, axis=-1, keepdims=True) + jnp.sum(p_ctx, axis=-1, keepdims=True)
                 + jnp.exp(sk - m))
        o = _dot(p_loc.astype(BF16), vl) + _dot(p_ctx.astype(BF16), vc_ref[0])
        o = o * (1.0 / denom)
        o_a, o_b = _unstack_heads(o, 2, blk)
        o_ref[0, rows, :PAIR] = o_a.astype(o_ref.dtype)
        o_ref[0, rows, PAIR:] = o_b.astype(o_ref.dtype)


def _swa(sink, q, k, v, kc, vc, *, nblk=4):
    b, s, _ = q.shape
    m = kc.shape[1]
    blk = SWA_BLOCK * nblk
    assert s % blk == 0 and s >= 3 * SWA_BLOCK
    return pl.pallas_call(
        functools.partial(_swa_kernel, seq=s, nblk=nblk),
        grid=(b, s // blk),
        in_specs=[
            pl.BlockSpec(memory_space=pltpu.SMEM),
            pl.BlockSpec((1, blk, GROUP_WIDTH), lambda i, n: (i, n, 0)),
            pl.BlockSpec((1, s, PAIR), lambda i, n: (i, 0, 0)),
            pl.BlockSpec((1, s, PAIR), lambda i, n: (i, 0, 0)),
            pl.BlockSpec((1, m, PAIR), lambda i, n: (i, 0, 0)),
            pl.BlockSpec((1, m, PAIR), lambda i, n: (i, 0, 0)),
        ],
        out_specs=pl.BlockSpec((1, blk, GROUP_WIDTH), lambda i, n: (i, n, 0)),
        out_shape=jax.ShapeDtypeStruct((b, s, GROUP_WIDTH), BF16),
        compiler_params=_cparams("arbitrary", "arbitrary"),
        name="swa_attn",
    )(sink, q, k, v, kc, vc)


def _dense_attn_kernel(*refs, kv_pairs, n_src, has_sink, tq, tk):
    refs = list(refs)
    sink_ref = refs.pop(0) if has_sink else None
    q_ref = refs.pop(0)
    srcs = [(refs[2 * i], refs[2 * i + 1]) for i in range(n_src)]
    o_ref = refs[2 * n_src]
    q = q_ref[0]
    if kv_pairs == 1:
        units = [([q[:, :PAIR], q[:, PAIR:]], 0)]
    else:
        units = [([q[:, :PAIR]], 0), ([q[:, PAIR:]], 1)]
    outs = []
    for q_pairs, kv in units:
        qs = _stack_heads(q_pairs)
        nrow = qs.shape[0]
        if has_sink:
            m = _sink_column(sink_ref, tq)
            l = jnp.ones((nrow, 1), F32)
        else:
            m = jnp.full((nrow, 1), NEG, F32)
            l = jnp.zeros((nrow, 1), F32)
        acc = jnp.zeros((nrow, PAIR), F32)
        for k_ref, v_ref in srcs:
            nk = k_ref.shape[1]
            step = min(tk, nk)
            for c in range(nk // step):
                kch = k_ref[0, c * step:(c + 1) * step, kv * PAIR:(kv + 1) * PAIR]
                vch = v_ref[0, c * step:(c + 1) * step, kv * PAIR:(kv + 1) * PAIR]
                s = _dot_nt(qs, kch)
                m_new = jnp.maximum(m, jnp.max(s, axis=-1, keepdims=True))
                a = jnp.exp(m - m_new)
                p = jnp.exp(s - m_new)
                l = a * l + jnp.sum(p, axis=-1, keepdims=True)
                acc = a * acc + _dot(p.astype(BF16), vch)
                m = m_new
        o = acc * (1.0 / l)
        outs += _unstack_heads(o, len(q_pairs), tq)
    o_ref[0, :, :PAIR] = outs[0].astype(o_ref.dtype)
    o_ref[0, :, PAIR:] = outs[1].astype(o_ref.dtype)


def _dense_attn(q, srcs, *, kv_pairs, sink=None, tq=128, tk=512):
    b, s, _ = q.shape
    kvw = kv_pairs * PAIR
    has_sink = sink is not None
    in_specs, args = [], []
    if has_sink:
        in_specs.append(pl.BlockSpec(memory_space=pltpu.SMEM))
        args.append(sink)
    in_specs.append(pl.BlockSpec((1, tq, GROUP_WIDTH), lambda i, n: (i, n, 0)))
    args.append(q)
    for k, v in srcs:
        nk = k.shape[1]
        assert nk % min(tk, nk) == 0
        in_specs += [pl.BlockSpec((1, nk, kvw), lambda i, n: (i, 0, 0))] * 2
        args += [k, v]
    return pl.pallas_call(
        functools.partial(_dense_attn_kernel, kv_pairs=kv_pairs, n_src=len(srcs),
                          has_sink=has_sink, tq=tq, tk=tk),
        grid=(b, s // tq),
        in_specs=in_specs,
        out_specs=pl.BlockSpec((1, tq, GROUP_WIDTH), lambda i, n: (i, n, 0)),
        out_shape=jax.ShapeDtypeStruct((b, s, GROUP_WIDTH), BF16),
        compiler_params=_cparams("arbitrary", "arbitrary"),
        name="dense_attn",
    )(*args)


GQA_EXTRA_ROWS = 16


def _gqa_kernel(q_ref, k_ref, v_ref, kc_ref, vc_ref, o_ref, vt_sc, *, tq, tk):
    n = pl.program_id(1)
    seq = k_ref.shape[1]
    m_ctx = kc_ref.shape[1]

    @pl.when(n == 0)
    def _():
        for c in range(seq // PAIR):
            rows = slice(c * PAIR, (c + 1) * PAIR)
            vt_sc[:PAIR, rows] = v_ref[0, rows, :].astype(F32).T.astype(BF16)
        for c in range(m_ctx // PAIR):
            rows = slice(c * PAIR, (c + 1) * PAIR)
            vt_sc[:PAIR, seq + c * PAIR: seq + (c + 1) * PAIR] = vc_ref[0, rows, :].astype(F32).T.astype(BF16)
        ones_row = lax.broadcasted_iota(jnp.int32, (GQA_EXTRA_ROWS, seq + m_ctx), 0) == 0
        vt_sc[PAIR:, :] = jnp.where(ones_row, 1.0, 0.0).astype(BF16)

    q = q_ref[0]
    qs = _stack_heads([q[:, :PAIR], q[:, PAIR:]])
    chunks = [(k_ref, c * tk, tk, c * tk) for c in range(seq // tk)]
    chunks += [(kc_ref, c * min(tk, m_ctx), min(tk, m_ctx), seq + c * min(tk, m_ctx))
               for c in range(m_ctx // min(tk, m_ctx))]
    nrow = 4 * tq
    m = jnp.full((1, nrow), NEG, F32)
    acc = jnp.zeros((PAIR + GQA_EXTRA_ROWS, nrow), F32)
    for ref, lo, size, col in chunks:
        s_t = _dot_nt(ref[0, lo:lo + size, :], qs)
        m_new = jnp.maximum(m, jnp.max(s_t, axis=0, keepdims=True))
        p_t = jnp.exp2((s_t - m_new).astype(BF16))
        acc = jnp.exp2(m - m_new) * acc + _dot(vt_sc[:, col:col + size], p_t)
        m = m_new
    o_t = acc[:PAIR] * (1.0 / acc[PAIR:PAIR + 1])
    lo_rows = lax.broadcasted_iota(jnp.int32, (PAIR, tq), 0) < HEAD_DIM
    for p in range(2):
        pair_t = jnp.where(lo_rows, o_t[:, 2 * p * tq:(2 * p + 1) * tq],
                           o_t[:, (2 * p + 1) * tq:(2 * p + 2) * tq])
        o_ref[0, :, p * PAIR:(p + 1) * PAIR] = pair_t.T.astype(o_ref.dtype)


def _gqa(q, k, v, kc, vc, *, tq=256, tk=1024):
    b, s, _ = q.shape
    m = kc.shape[1]
    tq = _token_tile(s, tq)
    assert s % tk == 0 and s % PAIR == 0 and m % PAIR == 0 and m % min(tk, m) == 0
    full = lambda i, n: (i, 0, 0)
    return pl.pallas_call(
        functools.partial(_gqa_kernel, tq=tq, tk=tk),
        grid=(b, s // tq),
        in_specs=[
            pl.BlockSpec((1, tq, GROUP_WIDTH), lambda i, n: (i, n, 0)),
            pl.BlockSpec((1, s, PAIR), full),
            pl.BlockSpec((1, s, PAIR), full),
            pl.BlockSpec((1, m, PAIR), full),
            pl.BlockSpec((1, m, PAIR), full),
        ],
        out_specs=pl.BlockSpec((1, tq, GROUP_WIDTH), lambda i, n: (i, n, 0)),
        out_shape=jax.ShapeDtypeStruct((b, s, GROUP_WIDTH), BF16),
        scratch_shapes=[pltpu.VMEM((PAIR + GQA_EXTRA_ROWS, s + m), BF16)],
        compiler_params=_cparams("arbitrary", "arbitrary"),
        name="gqa_attn",
    )(q, k, v, kc, vc)


NA_RPB_ROWS = 2 * NA_ROWS


def _na_rpb_pairs(rpb):
    h = rpb.shape[0]
    t = jnp.pad(rpb.astype(F32), ((0, 0), (1, 1), (0, HEAD_DIM - rpb.shape[2])))
    pairs = jnp.concatenate([t[:, :-1], t[:, 1:]], axis=-1)
    assert pairs.shape == (h, NA_RPB_ROWS, PAIR)
    return jnp.roll(pairs, -(NA_COLS - 1), axis=-1)


def _na_build_bias(rpb_ref, bias_sc, kind, g, rows):
    w = GRID_W
    r0 = NA_QROWS * g
    start_row = jnp.clip(r0 - NA_ROWS // 2, 0, rows - NA_KROWS)
    qcol = lax.broadcasted_iota(jnp.int32, (w, PAIR), 0)
    lane = lax.broadcasted_iota(jnp.int32, (w, PAIR), 1)
    kcol = lane % w
    odd = (lane >= w).astype(jnp.int32)
    cs = jnp.clip(qcol - NA_COLS // 2, 0, w - NA_COLS)
    col_ok = (kcol >= cs) & (kcol < cs + NA_COLS)

    def body(t, carry):
        rr = t // (NA_KROWS // 2)
        a2 = t % (NA_KROWS // 2)
        r = r0 + rr
        rs = jnp.clip(r - NA_ROWS // 2, 0, rows - NA_ROWS)
        krow0 = start_row + 2 * a2
        krow = krow0 + odd
        ok = col_ok & (krow >= rs) & (krow < rs + NA_ROWS)
        e = jnp.clip(krow0 - r + NA_ROWS, 0, NA_RPB_ROWS - 1)
        for h in range(4):
            tile = jnp.broadcast_to(rpb_ref[h, pl.ds(e, 1), :], (w, PAIR))
            for bit in range(6):
                tile = jnp.where(((qcol >> bit) & 1) == 1, pltpu.roll(tile, 1 << bit, 1), tile)
            row0 = pl.multiple_of((h % 2) * NA_QROWS * w + rr * w, w)
            bias_sc[kind, h // 2, a2, pl.ds(row0, w), :] = jnp.where(ok, tile, NEG)
        return carry

    lax.fori_loop(0, NA_QROWS * (NA_KROWS // 2), body, 0)


def _na_kernel(q_ref, k_ref, v_ref, kc_ref, vc_ref, rpb_ref, o_ref, bias_sc, *, rows):
    g = pl.program_id(1)
    groups = rows // NA_QROWS
    tq = NA_QROWS * GRID_W
    span = NA_KROWS * GRID_W

    kind = jnp.where(g == 0, 0, jnp.where(g == groups - 1, 2, 1))

    @pl.when((pl.program_id(0) == 0) & ((g == 0) | (g == 1) | (g == groups - 1)))
    def _():
        _na_build_bias(rpb_ref, bias_sc, kind, g, rows)

    start = pl.multiple_of(jnp.clip(NA_QROWS * g - NA_ROWS // 2, 0, rows - NA_KROWS) * GRID_W, GRID_W)
    q = q_ref[0]
    for p in range(2):
        lanes = slice(p * PAIR, (p + 1) * PAIR)
        qs = _stack_heads([q[:, lanes]])
        kl = k_ref[0, pl.ds(start, span), lanes]
        vl = v_ref[0, pl.ds(start, span), lanes]
        bias = jnp.concatenate([bias_sc[kind, p, a2] for a2 in range(NA_KROWS // 2)], axis=1)
        s_nb = _dot_nt(qs, kl) + bias
        s_ctx = _dot_nt(qs, kc_ref[0, :, lanes])
        m = jnp.maximum(jnp.max(s_nb, axis=-1, keepdims=True), jnp.max(s_ctx, axis=-1, keepdims=True))
        p_nb = jnp.exp(s_nb - m)
        p_ctx = jnp.exp(s_ctx - m)
        denom = jnp.sum(p_nb, axis=-1, keepdims=True) + jnp.sum(p_ctx, axis=-1, keepdims=True)
        o = _dot(p_nb.astype(BF16), vl) + _dot(p_ctx.astype(BF16), vc_ref[0, :, lanes])
        o = o * (1.0 / denom)
        o_ref[0, :, lanes] = _unstack_heads(o, 1, tq)[0].astype(o_ref.dtype)


def _na(q, k, v, kc, vc, rpb_pairs):
    b, s, _ = q.shape
    m = kc.shape[1]
    rows = s // GRID_W
    groups = rows // NA_QROWS
    tq = NA_QROWS * GRID_W
    assert rows % NA_QROWS == 0 and groups >= 4
    return pl.pallas_call(
        functools.partial(_na_kernel, rows=rows),
        grid=(b, groups),
        in_specs=[
            pl.BlockSpec((1, tq, GROUP_WIDTH), lambda i, g: (i, g, 0)),
            pl.BlockSpec((1, s, GROUP_WIDTH), lambda i, g: (i, 0, 0)),
            pl.BlockSpec((1, s, GROUP_WIDTH), lambda i, g: (i, 0, 0)),
            pl.BlockSpec((1, m, GROUP_WIDTH), lambda i, g: (i, 0, 0)),
            pl.BlockSpec((1, m, GROUP_WIDTH), lambda i, g: (i, 0, 0)),
            pl.BlockSpec((4, NA_RPB_ROWS, PAIR), lambda i, g: (0, 0, 0)),
        ],
        out_specs=pl.BlockSpec((1, tq, GROUP_WIDTH), lambda i, g: (i, g, 0)),
        out_shape=jax.ShapeDtypeStruct((b, s, GROUP_WIDTH), BF16),
        scratch_shapes=[pltpu.VMEM((3, 2, NA_KROWS // 2, 2 * tq, PAIR), F32)],
        compiler_params=_cparams("arbitrary", "arbitrary"),
        name="na_attn",
    )(q, k, v, kc, vc, rpb_pairs)


def _ssd_kernel(uf_ref, ub_ref, dtf_ref, dtb_ref, dtbias_ref, alog_ref, dskip_ref, h0f_ref, h0b_ref,
                yf_ref, yb_ref, hf_ref, hb_ref, hf_sc, hb_sc, *, nc):
    i = pl.program_id(1)
    q = SSD_CHUNK

    @pl.when(i == 0)
    def _():
        hf_sc[...] = h0f_ref[0]
        hb_sc[...] = h0b_ref[0]

    ii = lax.broadcasted_iota(jnp.int32, (q, q), 0)
    jj = lax.broadcasted_iota(jnp.int32, (q, q), 1)
    lo_lanes = jj < HEAD_DIM
    a_coef = -jnp.exp(alog_ref[...])

    def chunk(u, dt_raw, col0, reverse, h_sc):
        causal = (jj >= ii) if reverse else (jj <= ii)
        dt = _softplus(dt_raw + dtbias_ref[...])
        tri = jnp.where(causal, 1.0, 0.0).astype(BF16)
        a_hi, a_mid, a_lo = _split3(dt * a_coef)
        cum = _dot(tri, a_hi) + _dot(tri, a_mid) + _dot(tri, a_lo)
        cum_t = cum.T
        dt_t = dt.T
        end = cum[0:1, :] if reverse else cum[q - 1:q, :]
        xs = u[:, :2 * PAIR]
        ys = []
        for k in range(2):
            bk = u[:, 2 * PAIR + k * SSD_STATE: 2 * PAIR + (k + 1) * SSD_STATE]
            ck = u[:, 2 * PAIR + 2 * SSD_STATE + k * SSD_STATE: 2 * PAIR + 2 * SSD_STATE + (k + 1) * SSD_STATE]
            xk_b = xs[:, k * PAIR:(k + 1) * PAIR]
            xk = xk_b.astype(F32)
            ck_b = ck
            cb = _dot_nt(ck_b, bk)
            c0 = col0 + 2 * k
            cols = []
            for r in range(2):
                c = c0 + r
                seg = cum[:, c:c + 1] - cum_t[c:c + 1, :]
                lmat = jnp.exp(jnp.where(causal, seg, NEG))
                att = (cb * lmat * dt_t[c:c + 1, :]).astype(BF16)
                cols.append(_dot(att, xk_b))
            y_intra = jnp.where(lo_lanes, cols[0], cols[1])
            e_in = jnp.where(lo_lanes, jnp.exp(cum[:, c0:c0 + 1]), jnp.exp(cum[:, c0 + 1:c0 + 2]))
            h_t = h_sc[k]
            y_state = _dot(ck_b, h_t.astype(BF16)) * e_in
            w0 = jnp.exp(end[:, c0:c0 + 1] - cum[:, c0:c0 + 1]) * dt[:, c0:c0 + 1]
            w1 = jnp.exp(end[:, c0 + 1:c0 + 2] - cum[:, c0 + 1:c0 + 2]) * dt[:, c0 + 1:c0 + 2]
            xw = (xk * jnp.where(lo_lanes, w0, w1)).astype(BF16)
            st = _dot(bk.astype(F32).T.astype(BF16), xw)
            decay = jnp.where(lo_lanes[0:1], jnp.exp(end[:, c0:c0 + 1]), jnp.exp(end[:, c0 + 1:c0 + 2]))
            h_sc[k] = h_t * decay + st
            ys.append(y_intra + y_state)
        return ys, xs

    ys, xs = chunk(uf_ref[0], dtf_ref[0], 0, False, hf_sc)
    dskip = dskip_ref[...]
    for k in range(2):
        lanes = slice(k * PAIR, (k + 1) * PAIR)
        yf_ref[0, :, lanes] = ys[k] + dskip[:, lanes] * xs[:, lanes].astype(F32)

    ys, _ = chunk(ub_ref[0], dtb_ref[0], 4, True, hb_sc)
    for k in range(2):
        yb_ref[0, :, k * PAIR:(k + 1) * PAIR] = ys[k]

    @pl.when(i == nc - 1)
    def _():
        hf_ref[0] = hf_sc[...]
        hb_ref[0] = hb_sc[...]


def _ssd(u, dt, dt_bias, a_log, dskip, h0f, h0b):
    b, length, ch = u.shape
    q = SSD_CHUNK
    nc = length // q
    fwd = lambda i, c: (i, c, 0)
    bwd = lambda i, c: (i, nc - 1 - c, 0)
    const = lambda i, c: (0, 0)
    state = lambda i, c: (i, 0, 0, 0)
    state_shape = (b, 2, SSD_STATE, PAIR)
    return pl.pallas_call(
        functools.partial(_ssd_kernel, nc=nc),
        grid=(b, nc),
        in_specs=[
            pl.BlockSpec((1, q, ch), fwd),
            pl.BlockSpec((1, q, ch), bwd),
            pl.BlockSpec((1, q, DT_PAD), fwd),
            pl.BlockSpec((1, q, DT_PAD), bwd),
            pl.BlockSpec((1, DT_PAD), const),
            pl.BlockSpec((1, DT_PAD), const),
            pl.BlockSpec((1, GROUP_WIDTH), const),
            pl.BlockSpec((1, 2, SSD_STATE, PAIR), state),
            pl.BlockSpec((1, 2, SSD_STATE, PAIR), state),
        ],
        out_specs=[
            pl.BlockSpec((1, q, GROUP_WIDTH), fwd),
            pl.BlockSpec((1, q, GROUP_WIDTH), bwd),
            pl.BlockSpec((1, 2, SSD_STATE, PAIR), state),
            pl.BlockSpec((1, 2, SSD_STATE, PAIR), state),
        ],
        out_shape=[
            jax.ShapeDtypeStruct((b, length, GROUP_WIDTH), F32),
            jax.ShapeDtypeStruct((b, length, GROUP_WIDTH), F32),
            jax.ShapeDtypeStruct(state_shape, F32),
            jax.ShapeDtypeStruct(state_shape, F32),
        ],
        scratch_shapes=[pltpu.VMEM((2, SSD_STATE, PAIR), F32), pltpu.VMEM((2, SSD_STATE, PAIR), F32)],
        compiler_params=_cparams("arbitrary", "arbitrary"),
        name="ssd_scan",
    )(u, u, dt, dt, dt_bias, a_log, dskip, h0f, h0b)


def _head_block_index(start, perm):
    return np.concatenate([np.arange(start + h * HEAD_DIM, start + (h + 1) * HEAD_DIM) for h in perm])


def _inproj_columns():
    cols = [
        _head_block_index(0, HEAD_PERM), np.arange(256, 1536),
        _head_block_index(1544, HEAD_PERM), np.arange(1800, 2824),
    ]
    return np.concatenate(cols), np.arange(1536, 1544)


def _outproj_rows():
    return np.concatenate([
        _head_block_index(0, HEAD_PERM), np.arange(256, 512),
        _head_block_index(512, HEAD_PERM), np.arange(768, 1024),
    ])


def _rope_tables(seq):
    t = np.arange(seq)
    quarter = HEAD_DIM // 4
    inv = ROPE_BASE ** (-jnp.arange(quarter, dtype=F32) / quarter)
    a_row = jnp.asarray(t // GRID_W, F32)[:, None] * inv
    a_col = jnp.asarray(t % GRID_W, F32)[:, None] * inv
    cos = jnp.concatenate([jnp.cos(a_row), jnp.cos(a_row), jnp.cos(a_col), jnp.cos(a_col)], axis=1)
    sin = jnp.concatenate([-jnp.sin(a_row), jnp.sin(a_row), -jnp.sin(a_col), jnp.sin(a_col)], axis=1)
    return jnp.tile(cos, (1, 2)), jnp.tile(sin, (1, 2))


def _pad_lanes(v, width):
    v = v.reshape(1, -1).astype(F32)
    return jnp.pad(v, ((0, 0), (0, width - v.shape[1])))


def _token_tile(n, cap):
    t = cap
    while n % t:
        t //= 2
    return t


def kernel(x, c, ctx, c_ctx, ada_w, ada_b, ln_g, ln_b, ffn1_w_in, ffn1_w_out, mix_w_in, mix_w_out,
           mix_norm_g, swa_sink, ssd_conv_w, ssd_conv_b, ssd_dt_bias, ssd_A_log, ssd_D,
           gqa_q_norm, gqa_k_norm, na_rpb, ffn2_w_in, ffn2_w_out):
    bsz, seq, d = x.shape
    m_ctx = ctx.shape[1]
    depth = ada_w.shape[0]
    alpha = float((2 * depth) ** 0.25)
    mod_rows = 16 * ((bsz + 1 + 15) // 16)
    cc = jnp.concatenate([c, c_ctx[None], jnp.zeros((mod_rows - bsz - 1, d), F32)], axis=0)
    mods = _ada(cc, ada_w, ada_b)

    cos_x, sin_x = _rope_tables(seq)
    tm_x = _token_tile(seq, 512)
    tm_c = _token_tile(m_ctx, 256)
    cos_c = jnp.ones((tm_c, PAIR), F32)
    sin_c = jnp.zeros((tm_c, PAIR), F32)
    blockdiag = jnp.asarray(np.kron(np.eye(2), np.ones((HEAD_DIM, HEAD_DIM))), BF16)
    main_cols, dt_cols = _inproj_columns()
    out_rows = _outproj_rows()

    xt = x.reshape(bsz * seq, d)
    ct = ctx.reshape(bsz * m_ctx, d)
    zero_state = jnp.zeros((bsz, 2, SSD_STATE, PAIR), F32)

    for l in range(depth):
        last = l == depth - 1
        mod_x = mods[l, :bsz].reshape(bsz, N_MOD, d)
        mod_c = mods[l, bsz:bsz + 1].reshape(1, N_MOD, d)
        w1_in, w1_out = ffn1_w_in[l].astype(BF16), ffn1_w_out[l].astype(BF16)
        w2_in, w2_out = ffn2_w_in[l].astype(BF16), ffn2_w_out[l].astype(BF16)
        w_mix = mix_w_in[l]
        w_mix = jnp.concatenate(
            [w_mix[:, main_cols], jnp.pad(w_mix[:, dt_cols], ((0, 0), (0, DT_PAD - dt_cols.size)))],
            axis=1).astype(BF16)
        w_o = mix_w_out[l][out_rows].astype(BF16)
        norm_g = mix_norm_g[l][out_rows].reshape(1, d)
        qg = jnp.tile(gqa_q_norm[l], 2).reshape(1, PAIR)
        kg = jnp.tile(gqa_k_norm[l], 2).reshape(1, PAIR)
        conv_w = jnp.pad(ssd_conv_w[l].reshape(SSD_CONV, SSD_CONV_CH), ((0, 8 - SSD_CONV), (0, 0)))
        conv_b = ssd_conv_b[l].reshape(1, SSD_CONV_CH)
        dt_bias = _pad_lanes(ssd_dt_bias[l], DT_PAD)
        a_log = _pad_lanes(ssd_A_log[l], DT_PAD)
        dskip = jnp.repeat(ssd_D[l], HEAD_DIM).reshape(1, GROUP_WIDTH)
        sink = swa_sink[l].astype(F32)
        rpb_pairs = _na_rpb_pairs(na_rpb[l])

        ffn_x = functools.partial(_ffn, tokens_per_mod=seq, alpha=alpha, tm=tm_x)
        ffn_c = functools.partial(_ffn, tokens_per_mod=bsz * m_ctx, alpha=alpha, tm=tm_c)

        xt = ffn_x(xt, mod_x, (0, 1, 2), w1_in, w1_out, ln_g[l, 0], ln_b[l, 0])
        ct = ffn_c(ct, mod_c, (0, 1, 2), w1_in, w1_out, ln_g[l, 0], ln_b[l, 0])

        px = _inproj(xt, mod_x, w_mix, cos_x, sin_x, qg, kg, blockdiag, conv_w, conv_b,
                     tokens_per_mod=seq, seq_len=seq, tm=tm_x, gqa_q_scale=Q_SCALE * LOG2E)
        pc = _inproj(ct, mod_c, w_mix, cos_c, sin_c, qg, kg, blockdiag, conv_w, conv_b,
                     tokens_per_mod=bsz * m_ctx, seq_len=m_ctx, tm=tm_c, gqa_q_scale=Q_SCALE)
        aq, ak, av, bz, bu, bdt, cq, ck, cv, dq, dk, dv = [
            t.reshape(bsz, seq, t.shape[-1]) for t in px]
        aq_c, ak_c, av_c, bz_c, bu_c, bdt_c, cq_c, ck_c, cv_c, dq_c, dk_c, dv_c = [
            t.reshape(bsz, m_ctx, t.shape[-1]) for t in pc]

        ssd = functools.partial(_ssd, dt_bias=dt_bias, a_log=a_log, dskip=dskip)
        yf_c, yb_c, hf_c, hb_c = ssd(bu_c, bdt_c, h0f=zero_state, h0b=zero_state)
        yf, yb, _, _ = ssd(bu, bdt, h0f=hf_c, h0b=hb_c)

        oa = _swa(sink, aq, ak, av, ak_c, av_c)
        oc = _gqa(cq, ck, cv, ck_c, cv_c)
        od = _na(dq, dk, dv, dk_c, dv_c, rpb_pairs)

        flat = lambda t: t.reshape(-1, t.shape[-1])
        xt = _mixer_ffn(xt, mod_x, flat(oa), flat(yf), flat(yb), flat(bz), flat(oc), flat(od), norm_g, w_o,
                        ln_g[l, 1], ln_b[l, 1], w2_in, w2_out, ln_g[l, 2], ln_b[l, 2],
                        tokens_per_mod=seq, alpha=alpha, tm=tm_x)

        if not last:
            tq_c = _token_tile(m_ctx, 128)
            oa_c = _dense_attn(aq_c, [(ak_c, av_c)], kv_pairs=1, sink=sink, tq=tq_c)
            oc_c = _dense_attn(cq_c, [(ck_c, cv_c)], kv_pairs=1, tq=tq_c)
            od_c = _dense_attn(dq_c, [(dk_c, dv_c)], kv_pairs=2, tq=tq_c)
            ct = _mixer_ffn(ct, mod_c, flat(oa_c), flat(yf_c), flat(yb_c), flat(bz_c), flat(oc_c),
                            flat(od_c), norm_g, w_o, ln_g[l, 1], ln_b[l, 1], w2_in, w2_out,
                            ln_g[l, 2], ln_b[l, 2], tokens_per_mod=bsz * m_ctx, alpha=alpha, tm=tm_c)

    return xt.reshape(bsz, seq, d)
```

```python
import functools

import numpy as np
import jax
import jax.numpy as jnp
from jax import lax
from jax.experimental import pallas as pl
from jax.experimental.pallas import tpu as pltpu

F32 = jnp.float32
BF16 = jnp.bfloat16

HEAD_DIM = 64
PAIR = 2 * HEAD_DIM
GROUP_WIDTH = 256
GRID_W = 64
N_MOD = 9
SWA_WINDOW = 128
SWA_BLOCK = 128
SSD_CHUNK = 128
SSD_STATE = 128
SSD_CONV = 5
CONV_HALO = 8
FFN_SUBTILE = 256
SSD_CONV_CH = 768
NA_ROWS = 8
NA_COLS = 16
NA_QROWS = 4
NA_KROWS = 12
ROPE_BASE = 10000.0
EPS = 1e-5
NEG = -1e30
Q_SCALE = HEAD_DIM ** -0.5
LOG2E = 1.4426950408889634
DT_PAD = 128
HEAD_PERM = (0, 2, 1, 3)

VMEM_LIMIT_BYTES = 56 * 1024 * 1024


def _cparams(*sem):
    return pltpu.CompilerParams(dimension_semantics=sem, vmem_limit_bytes=VMEM_LIMIT_BYTES)


def _dot(a, b):
    return jnp.dot(a, b, preferred_element_type=F32)


def _dot_nt(a, b):
    return lax.dot_general(a, b, (((1,), (1,)), ((), ())), preferred_element_type=F32)


def _sigmoid(x):
    return 1.0 / (1.0 + jnp.exp(-x))


def _silu(x):
    return x * _sigmoid(x)


def _softplus(x):
    return jnp.maximum(x, 0.0) + jnp.log(1.0 + jnp.exp(-jnp.abs(x)))


def _split3(a):
    hi = a.astype(BF16)
    r1 = a - hi.astype(F32)
    mid = r1.astype(BF16)
    lo = (r1 - mid.astype(F32)).astype(BF16)
    return hi, mid, lo


def _layer_norm(y, g, b):
    mu = jnp.mean(y, axis=-1, keepdims=True)
    d = y - mu
    var = jnp.mean(d * d, axis=-1, keepdims=True)
    return d * lax.rsqrt(var + EPS) * g + b


def _ada_kernel(c_ref, w_ref, b_ref, o_ref):
    s = _silu(c_ref[...])
    s_hi = s.astype(BF16)
    s_lo = (s - s_hi.astype(F32)).astype(BF16)
    w = w_ref[0]
    w_hi = w.astype(BF16)
    w_lo = (w - w_hi.astype(F32)).astype(BF16)
    o_ref[0] = _dot(s_hi, w_hi) + _dot(s_lo, w_hi) + _dot(s_hi, w_lo) + b_ref[0]


def _ada(cc, ada_w, ada_b):
    depth, d, n = ada_w.shape
    rows = cc.shape[0]
    tn = 1024
    return pl.pallas_call(
        _ada_kernel,
        grid=(depth, n // tn),
        in_specs=[
            pl.BlockSpec((rows, d), lambda l, j: (0, 0)),
            pl.BlockSpec((1, d, tn), lambda l, j: (l, 0, j)),
            pl.BlockSpec((1, 1, tn), lambda l, j: (l, 0, j)),
        ],
        out_specs=pl.BlockSpec((1, rows, tn), lambda l, j: (l, 0, j)),
        out_shape=jax.ShapeDtypeStruct((depth, rows, n), F32),
        compiler_params=_cparams("arbitrary", "arbitrary"),
        name="ada_mod",
    )(cc, ada_w, ada_b.reshape(depth, 1, n))


def _ffn_rows(x, mod_ref, rows, wa_ref, wu_ref, wo_ref, g, b, alpha):
    r_shift, r_scale, r_gate = rows
    h = x * (1.0 + mod_ref[0, r_scale:r_scale + 1, :]) + mod_ref[0, r_shift:r_shift + 1, :]
    h = h.astype(BF16)
    a = _dot(h, wa_ref[...])
    u = _dot(h, wu_ref[...])
    gated = (_silu(a) * u).astype(BF16)
    f = _dot(gated, wo_ref[...])
    y = alpha * x + (0.5 * mod_ref[0, r_gate:r_gate + 1, :]) * f
    return _layer_norm(y, g, b)


def _mixer_rows(x, mod_ref, parts, ng_ref, w_ref, g, b, alpha):
    acc = None
    for gi, y in enumerate(parts):
        rows = slice(gi * GROUP_WIDTH, (gi + 1) * GROUP_WIDTH)
        ms = jnp.mean(y * y, axis=-1, keepdims=True)
        yn = (y * lax.rsqrt(ms + EPS) * ng_ref[:, rows]).astype(BF16)
        term = _dot(yn, w_ref[rows, :])
        acc = term if acc is None else acc + term
    return _layer_norm(alpha * x + mod_ref[0, 5:6, :] * acc, g, b)


def _subtiles(tm):
    sub = FFN_SUBTILE if tm % FFN_SUBTILE == 0 else tm
    return [slice(r0, r0 + sub) for r0 in range(0, tm, sub)]


def _ffn_kernel(x_ref, mod_ref, wa_ref, wu_ref, wo_ref, g_ref, b_ref, o_ref, *, rows, alpha):
    for r in _subtiles(x_ref.shape[0]):
        o_ref[r, :] = _ffn_rows(x_ref[r, :], mod_ref, rows, wa_ref, wu_ref, wo_ref,
                                g_ref[...], b_ref[...], alpha)


def _mixer_ffn_kernel(x_ref, mod_ref, oa_ref, yf_ref, yb_ref, z_ref, oc_ref, od_ref, ng_ref, wm_ref,
                      g1_ref, b1_ref, wa_ref, wu_ref, wo_ref, g2_ref, b2_ref, o_ref, *, alpha):
    for r in _subtiles(x_ref.shape[0]):
        ob = (yf_ref[r, :] + yb_ref[r, :]) * _silu(z_ref[r, :].astype(F32))
        parts = (oa_ref[r, :].astype(F32), ob, oc_ref[r, :].astype(F32), od_ref[r, :].astype(F32))
        x_mid = _mixer_rows(x_ref[r, :], mod_ref, parts, ng_ref, wm_ref, g1_ref[...], b1_ref[...], alpha)
        o_ref[r, :] = _ffn_rows(x_mid, mod_ref, (6, 7, 8), wa_ref, wu_ref, wo_ref,
                                g2_ref[...], b2_ref[...], alpha)


def _resident(shape, index_map):
    return pl.BlockSpec(shape, index_map, pipeline_mode=pl.Buffered(1))


def _ffn(xt, mod, rows, w_in, w_out, ln_g, ln_b, *, tokens_per_mod, alpha, tm):
    t, d = xt.shape
    dff = w_out.shape[0]
    tiles_per_mod = tokens_per_mod // tm
    return pl.pallas_call(
        functools.partial(_ffn_kernel, rows=rows, alpha=alpha),
        grid=(t // tm,),
        in_specs=[
            pl.BlockSpec((tm, d), lambda i: (i, 0)),
            pl.BlockSpec((1, N_MOD, d), lambda i: (i // tiles_per_mod, 0, 0)),
            _resident((d, dff), lambda i: (0, 0)),
            _resident((d, dff), lambda i: (0, 1)),
            _resident((dff, d), lambda i: (0, 0)),
            pl.BlockSpec((1, d), lambda i: (0, 0)),
            pl.BlockSpec((1, d), lambda i: (0, 0)),
        ],
        out_specs=pl.BlockSpec((tm, d), lambda i: (i, 0)),
        out_shape=jax.ShapeDtypeStruct((t, d), F32),
        compiler_params=_cparams("arbitrary"),
        name="ffn",
    )(xt, mod, w_in, w_in, w_out, ln_g.reshape(1, d), ln_b.reshape(1, d))


def _mixer_ffn(xt, mod, oa, yf, yb, z, oc, od, norm_g, w_mix_out, ln1_g, ln1_b, w_in, w_out, ln2_g, ln2_b,
               *, tokens_per_mod, alpha, tm):
    t, d = xt.shape
    dff = w_out.shape[0]
    tiles_per_mod = tokens_per_mod // tm
    row = lambda i: (i, 0)
    const = lambda i: (0, 0)
    grp = pl.BlockSpec((tm, GROUP_WIDTH), row)
    vec = pl.BlockSpec((1, d), const)
    return pl.pallas_call(
        functools.partial(_mixer_ffn_kernel, alpha=alpha),
        grid=(t // tm,),
        in_specs=[
            pl.BlockSpec((tm, d), row),
            pl.BlockSpec((1, N_MOD, d), lambda i: (i // tiles_per_mod, 0, 0)),
            grp, grp, grp, grp, grp, grp,
            vec,
            _resident((d, d), const),
            vec, vec,
            _resident((d, dff), const),
            _resident((d, dff), lambda i: (0, 1)),
            _resident((dff, d), const),
            vec, vec,
        ],
        out_specs=pl.BlockSpec((tm, d), row),
        out_shape=jax.ShapeDtypeStruct((t, d), F32),
        compiler_params=_cparams("arbitrary"),
        name="mixer_ffn",
    )(xt, mod, oa, yf, yb, z, oc, od, norm_g, w_mix_out, ln1_g.reshape(1, d), ln1_b.reshape(1, d),
      w_in, w_in, w_out, ln2_g.reshape(1, d), ln2_b.reshape(1, d))


_P_AQ, _P_AK, _P_AV, _P_Z, _P_XBC = 0, 256, 384, 512, 768
_P_CQ, _P_CK, _P_CV, _P_DQ, _P_DK, _P_DV, _P_DT, _P_END = 1536, 1792, 1920, 2048, 2304, 2560, 2816, 2944


def _rope_pair(t, cos, sin):
    lane = lax.broadcasted_iota(jnp.int32, t.shape, 1)
    up = pltpu.roll(t, HEAD_DIM // 4, 1)
    dn = pltpu.roll(t, PAIR - HEAD_DIM // 4, 1)
    partner = jnp.where((lane % 32) < 16, dn, up)
    return t * cos + partner * sin


def _rms_pair(t, g, blockdiag):
    sq = t * t
    hi = sq.astype(BF16)
    lo = (sq - hi.astype(F32)).astype(BF16)
    ss = _dot(hi, blockdiag) + _dot(lo, blockdiag)
    return t * lax.rsqrt(ss * (1.0 / HEAD_DIM) + EPS) * g


def _inproj_kernel(x_ref, xp_ref, xn_ref, mod_ref, w_ref, cos_ref, sin_ref, qg_ref, kg_ref, bd_ref,
                   cw_ref, cb_ref,
                   aq_ref, ak_ref, av_ref, z_ref, u_ref, dt_ref,
                   cq_ref, ck_ref, cv_ref, dq_ref, dk_ref, dv_ref, ext_sc, *, gqa_q_scale, tiles_per_seq):
    tm = x_ref.shape[0]
    halo = xp_ref.shape[0]
    pos = pl.program_id(0) % tiles_per_seq
    x_ext = jnp.concatenate([xp_ref[...], x_ref[...], xn_ref[...]], axis=0)
    h = (x_ext * (1.0 + mod_ref[0, 4:5, :]) + mod_ref[0, 3:4, :]).astype(BF16)
    y_ext = _dot(h, w_ref[...])
    y = y_ext[halo:halo + tm]
    cos = cos_ref[...]
    sin = sin_ref[...]
    bd = bd_ref[...]

    xbc = y_ext[:, _P_XBC:_P_XBC + SSD_CONV_CH]
    ext_sc[:halo] = jnp.where(pos == 0, 0.0, xbc[:halo])
    ext_sc[halo:halo + tm] = xbc[halo:halo + tm]
    ext_sc[halo + tm:] = jnp.where(pos == tiles_per_seq - 1, 0.0, xbc[halo + tm:])
    conv = cb_ref[...]
    for k in range(SSD_CONV):
        lo = halo - SSD_CONV // 2 + k
        conv = conv + cw_ref[k:k + 1, :] * ext_sc[lo:lo + tm, :]
    u_ref[...] = _silu(conv).astype(u_ref.dtype)

    def pairs(lo, n):
        return [y[:, lo + PAIR * p: lo + PAIR * (p + 1)] for p in range(n)]

    def put(ref, parts, scale=None):
        for p, part in enumerate(parts):
            if scale is not None:
                part = part * scale
            ref[:, PAIR * p: PAIR * (p + 1)] = part.astype(ref.dtype)

    put(aq_ref, [_rope_pair(t, cos, sin) for t in pairs(_P_AQ, 2)], Q_SCALE)
    put(ak_ref, [_rope_pair(t, cos, sin) for t in pairs(_P_AK, 1)])
    put(av_ref, pairs(_P_AV, 1))
    put(z_ref, pairs(_P_Z, 2))
    put(dt_ref, pairs(_P_DT, 1))
    put(cq_ref, [_rope_pair(_rms_pair(t, qg_ref[...], bd), cos, sin) for t in pairs(_P_CQ, 2)], gqa_q_scale)
    put(ck_ref, [_rope_pair(_rms_pair(t, kg_ref[...], bd), cos, sin) for t in pairs(_P_CK, 1)])
    put(cv_ref, pairs(_P_CV, 1))
    put(dq_ref, pairs(_P_DQ, 2), Q_SCALE)
    put(dk_ref, pairs(_P_DK, 2))
    put(dv_ref, pairs(_P_DV, 2))


def _inproj(xt, mod, w, cos, sin, qg, kg, blockdiag, conv_w, conv_b, *, tokens_per_mod, seq_len, tm,
            gqa_q_scale):
    t, d = xt.shape
    tiles_per_mod = tokens_per_mod // tm
    table_tiles = cos.shape[0] // tm
    widths = (256, 128, 128, 256, SSD_CONV_CH, DT_PAD, 256, 128, 128, 256, 256, 256)
    dtypes = (BF16,) * 5 + (F32,) + (BF16,) * 6
    per = tm // CONV_HALO
    last_halo = t // CONV_HALO - 1
    row = lambda i: (i, 0)
    const = lambda i: (0, 0)
    return pl.pallas_call(
        functools.partial(_inproj_kernel, gqa_q_scale=gqa_q_scale, tiles_per_seq=seq_len // tm),
        grid=(t // tm,),
        in_specs=[
            pl.BlockSpec((tm, d), row),
            pl.BlockSpec((CONV_HALO, d), lambda i: (jnp.maximum(i * per - 1, 0), 0)),
            pl.BlockSpec((CONV_HALO, d), lambda i: (jnp.minimum((i + 1) * per, last_halo), 0)),
            pl.BlockSpec((1, N_MOD, d), lambda i: (i // tiles_per_mod, 0, 0)),
            _resident((d, _P_END), const),
            pl.BlockSpec((tm, PAIR), lambda i: (i % table_tiles, 0)),
            pl.BlockSpec((tm, PAIR), lambda i: (i % table_tiles, 0)),
            pl.BlockSpec((1, PAIR), const),
            pl.BlockSpec((1, PAIR), const),
            pl.BlockSpec((PAIR, PAIR), const),
            pl.BlockSpec((8, SSD_CONV_CH), const),
            pl.BlockSpec((1, SSD_CONV_CH), const),
        ],
        out_specs=[pl.BlockSpec((tm, wd), row) for wd in widths],
        out_shape=[jax.ShapeDtypeStruct((t, wd), dt) for wd, dt in zip(widths, dtypes)],
        scratch_shapes=[pltpu.VMEM((tm + 2 * CONV_HALO, SSD_CONV_CH), F32)],
        compiler_params=_cparams("arbitrary"),
        name="inproj",
    )(xt, xt, xt, mod, w, cos, sin, qg, kg, blockdiag, conv_w, conv_b)


def _stack_heads(q_pairs):
    lane = lax.broadcasted_iota(jnp.int32, q_pairs[0].shape, 1)
    lo = lane < HEAD_DIM
    zero = jnp.zeros_like(q_pairs[0])
    blocks = []
    for qp in q_pairs:
        blocks.append(jnp.where(lo, qp, zero))
        blocks.append(jnp.where(lo, zero, qp))
    return jnp.concatenate(blocks, axis=0)


def _unstack_heads(o, n_pairs, tq):
    lane = lax.broadcasted_iota(jnp.int32, (tq, PAIR), 1)
    lo = lane < HEAD_DIM
    return [jnp.where(lo, o[2 * p * tq:(2 * p + 1) * tq], o[(2 * p + 1) * tq:(2 * p + 2) * tq])
            for p in range(n_pairs)]


def _sink_column(sink_ref, tq):
    blk = lax.broadcasted_iota(jnp.int32, (4 * tq, 1), 0) // tq
    col = jnp.full((4 * tq, 1), sink_ref[HEAD_PERM[3]], F32)
    for b in (2, 1, 0):
        col = jnp.where(blk == b, sink_ref[HEAD_PERM[b]], col)
    return col


def _swa_kernel(sink_ref, q_ref, k_ref, v_ref, kc_ref, vc_ref, o_ref, *, seq, nblk):
    blk = SWA_BLOCK
    span = 3 * blk
    sk = _sink_column(sink_ref, blk)
    for j in range(nblk):
        n = pl.program_id(1) * nblk + j
        rows = slice(j * blk, (j + 1) * blk)
        start = pl.multiple_of(jnp.clip((n - 1) * blk, 0, seq - span), blk)
        kl = k_ref[0, pl.ds(start, span), :]
        vl = v_ref[0, pl.ds(start, span), :]
        q = q_ref[0, rows, :]
        qs = _stack_heads([q[:, :PAIR], q[:, PAIR:]])
        s_loc = _dot_nt(qs, kl)
        s_ctx = _dot_nt(qs, kc_ref[0])
        qpos = n * blk + lax.broadcasted_iota(jnp.int32, s_loc.shape, 0) % blk
        kpos = start + lax.broadcasted_iota(jnp.int32, s_loc.shape, 1)
        s_loc = jnp.where(jnp.abs(kpos - qpos) <= SWA_WINDOW, s_loc, NEG)
        m = jnp.maximum(jnp.maximum(jnp.max(s_loc, axis=-1, keepdims=True),
                                    jnp.max(s_ctx, axis=-1, keepdims=True)), sk)
        p_loc = jnp.exp(s_loc - m)
        p_ctx = jnp.exp(s_ctx - m)
        denom = (jnp.sum(p_loc, axis=-1, keepdims=True) + jnp.sum(p_ctx, axis=-1, keepdims=True)
                 + jnp.exp(sk - m))
        o = _dot(p_loc.astype(BF16), vl) + _dot(p_ctx.astype(BF16), vc_ref[0])
        o = o * (1.0 / denom)
        o_a, o_b = _unstack_heads(o, 2, blk)
        o_ref[0, rows, :PAIR] = o_a.astype(o_ref.dtype)
        o_ref[0, rows, PAIR:] = o_b.astype(o_ref.dtype)


def _swa(sink, q, k, v, kc, vc, *, nblk=8):
    b, s, _ = q.shape
    m = kc.shape[1]
    blk = SWA_BLOCK * nblk
    assert s % blk == 0 and s >= 3 * SWA_BLOCK
    return pl.pallas_call(
        functools.partial(_swa_kernel, seq=s, nblk=nblk),
        grid=(b, s // blk),
        in_specs=[
            pl.BlockSpec(memory_space=pltpu.SMEM),
            pl.BlockSpec((1, blk, GROUP_WIDTH), lambda i, n: (i, n, 0)),
            pl.BlockSpec((1, s, PAIR), lambda i, n: (i, 0, 0)),
            pl.BlockSpec((1, s, PAIR), lambda i, n: (i, 0, 0)),
            pl.BlockSpec((1, m, PAIR), lambda i, n: (i, 0, 0)),
            pl.BlockSpec((1, m, PAIR), lambda i, n: (i, 0, 0)),
        ],
        out_specs=pl.BlockSpec((1, blk, GROUP_WIDTH), lambda i, n: (i, n, 0)),
        out_shape=jax.ShapeDtypeStruct((b, s, GROUP_WIDTH), BF16),
        compiler_params=_cparams("arbitrary", "arbitrary"),
        name="swa_attn",
    )(sink, q, k, v, kc, vc)


def _dense_attn_kernel(*refs, kv_pairs, n_src, has_sink, tq, tk):
    refs = list(refs)
    sink_ref = refs.pop(0) if has_sink else None
    q_ref = refs.pop(0)
    srcs = [(refs[2 * i], refs[2 * i + 1]) for i in range(n_src)]
    o_ref = refs[2 * n_src]
    q = q_ref[0]
    if kv_pairs == 1:
        units = [([q[:, :PAIR], q[:, PAIR:]], 0)]
    else:
        units = [([q[:, :PAIR]], 0), ([q[:, PAIR:]], 1)]
    outs = []
    for q_pairs, kv in units:
        qs = _stack_heads(q_pairs)
        nrow = qs.shape[0]
        if has_sink:
            m = _sink_column(sink_ref, tq)
            l = jnp.ones((nrow, 1), F32)
        else:
            m = jnp.full((nrow, 1), NEG, F32)
            l = jnp.zeros((nrow, 1), F32)
        acc = jnp.zeros((nrow, PAIR), F32)
        for k_ref, v_ref in srcs:
            nk = k_ref.shape[1]
            step = min(tk, nk)
            for c in range(nk // step):
                kch = k_ref[0, c * step:(c + 1) * step, kv * PAIR:(kv + 1) * PAIR]
                vch = v_ref[0, c * step:(c + 1) * step, kv * PAIR:(kv + 1) * PAIR]
                s = _dot_nt(qs, kch)
                m_new = jnp.maximum(m, jnp.max(s, axis=-1, keepdims=True))
                a = jnp.exp(m - m_new)
                p = jnp.exp(s - m_new)
                l = a * l + jnp.sum(p, axis=-1, keepdims=True)
                acc = a * acc + _dot(p.astype(BF16), vch)
                m = m_new
        o = acc * (1.0 / l)
        outs += _unstack_heads(o, len(q_pairs), tq)
    o_ref[0, :, :PAIR] = outs[0].astype(o_ref.dtype)
    o_ref[0, :, PAIR:] = outs[1].astype(o_ref.dtype)


def _dense_attn(q, srcs, *, kv_pairs, sink=None, tq=128, tk=512):
    b, s, _ = q.shape
    kvw = kv_pairs * PAIR
    has_sink = sink is not None
    in_specs, args = [], []
    if has_sink:
        in_specs.append(pl.BlockSpec(memory_space=pltpu.SMEM))
        args.append(sink)
    in_specs.append(pl.BlockSpec((1, tq, GROUP_WIDTH), lambda i, n: (i, n, 0)))
    args.append(q)
    for k, v in srcs:
        nk = k.shape[1]
        assert nk % min(tk, nk) == 0
        in_specs += [pl.BlockSpec((1, nk, kvw), lambda i, n: (i, 0, 0))] * 2
        args += [k, v]
    return pl.pallas_call(
        functools.partial(_dense_attn_kernel, kv_pairs=kv_pairs, n_src=len(srcs),
                          has_sink=has_sink, tq=tq, tk=tk),
        grid=(b, s // tq),
        in_specs=in_specs,
        out_specs=pl.BlockSpec((1, tq, GROUP_WIDTH), lambda i, n: (i, n, 0)),
        out_shape=jax.ShapeDtypeStruct((b, s, GROUP_WIDTH), BF16),
        compiler_params=_cparams("arbitrary", "arbitrary"),
        name="dense_attn",
    )(*args)


GQA_EXTRA_ROWS = 16


def _gqa_kernel(q_ref, k_ref, v_ref, kc_ref, vc_ref, o_ref, vt_sc, *, tq, tk):
    n = pl.program_id(1)
    seq = k_ref.shape[1]
    m_ctx = kc_ref.shape[1]

    @pl.when(n == 0)
    def _():
        for c in range(seq // PAIR):
            rows = slice(c * PAIR, (c + 1) * PAIR)
            vt_sc[:PAIR, rows] = v_ref[0, rows, :].astype(F32).T.astype(BF16)
        for c in range(m_ctx // PAIR):
            rows = slice(c * PAIR, (c + 1) * PAIR)
            vt_sc[:PAIR, seq + c * PAIR: seq + (c + 1) * PAIR] = vc_ref[0, rows, :].astype(F32).T.astype(BF16)
        ones_row = lax.broadcasted_iota(jnp.int32, (GQA_EXTRA_ROWS, seq + m_ctx), 0) == 0
        vt_sc[PAIR:, :] = jnp.where(ones_row, 1.0, 0.0).astype(BF16)

    q = q_ref[0]
    qs = _stack_heads([q[:, :PAIR], q[:, PAIR:]])
    chunks = [(k_ref, c * tk, tk, c * tk) for c in range(seq // tk)]
    chunks += [(kc_ref, c * min(tk, m_ctx), min(tk, m_ctx), seq + c * min(tk, m_ctx))
               for c in range(m_ctx // min(tk, m_ctx))]
    nrow = 4 * tq
    m = jnp.full((1, nrow), NEG, F32)
    acc = jnp.zeros((PAIR + GQA_EXTRA_ROWS, nrow), F32)
    for ref, lo, size, col in chunks:
        s_t = _dot_nt(ref[0, lo:lo + size, :], qs)
        m_new = jnp.maximum(m, jnp.max(s_t, axis=0, keepdims=True))
        p_t = jnp.exp2((s_t - m_new).astype(BF16))
        acc = jnp.exp2(m - m_new) * acc + _dot(vt_sc[:, col:col + size], p_t)
        m = m_new
    o_t = acc[:PAIR] * (1.0 / acc[PAIR:PAIR + 1])
    lo_rows = lax.broadcasted_iota(jnp.int32, (PAIR, tq), 0) < HEAD_DIM
    for p in range(2):
        pair_t = jnp.where(lo_rows, o_t[:, 2 * p * tq:(2 * p + 1) * tq],
                           o_t[:, (2 * p + 1) * tq:(2 * p + 2) * tq])
        o_ref[0, :, p * PAIR:(p + 1) * PAIR] = pair_t.T.astype(o_ref.dtype)


def _gqa(q, k, v, kc, vc, *, tq=256, tk=1024):
    b, s, _ = q.shape
    m = kc.shape[1]
    tq = _token_tile(s, tq)
    assert s % tk == 0 and s % PAIR == 0 and m % PAIR == 0 and m % min(tk, m) == 0
    full = lambda i, n: (i, 0, 0)
    return pl.pallas_call(
        functools.partial(_gqa_kernel, tq=tq, tk=tk),
        grid=(b, s // tq),
        in_specs=[
            pl.BlockSpec((1, tq, GROUP_WIDTH), lambda i, n: (i, n, 0)),
            pl.BlockSpec((1, s, PAIR), full),
            pl.BlockSpec((1, s, PAIR), full),
            pl.BlockSpec((1, m, PAIR), full),
            pl.BlockSpec((1, m, PAIR), full),
        ],
        out_specs=pl.BlockSpec((1, tq, GROUP_WIDTH), lambda i, n: (i, n, 0)),
        out_shape=jax.ShapeDtypeStruct((b, s, GROUP_WIDTH), BF16),
        scratch_shapes=[pltpu.VMEM((PAIR + GQA_EXTRA_ROWS, s + m), BF16)],
        compiler_params=_cparams("arbitrary", "arbitrary"),
        name="gqa_attn",
    )(q, k, v, kc, vc)


NA_RPB_ROWS = 2 * NA_ROWS


def _na_rpb_pairs(rpb):
    h = rpb.shape[0]
    t = jnp.pad(rpb.astype(F32), ((0, 0), (1, 1), (0, HEAD_DIM - rpb.shape[2])))
    pairs = jnp.concatenate([t[:, :-1], t[:, 1:]], axis=-1)
    assert pairs.shape == (h, NA_RPB_ROWS, PAIR)
    return jnp.roll(pairs, -(NA_COLS - 1), axis=-1)


def _na_build_bias(rpb_ref, bias_sc, kind, g, rows):
    w = GRID_W
    r0 = NA_QROWS * g
    start_row = jnp.clip(r0 - NA_ROWS // 2, 0, rows - NA_KROWS)
    qcol = lax.broadcasted_iota(jnp.int32, (w, PAIR), 0)
    lane = lax.broadcasted_iota(jnp.int32, (w, PAIR), 1)
    kcol = lane % w
    odd = (lane >= w).astype(jnp.int32)
    cs = jnp.clip(qcol - NA_COLS // 2, 0, w - NA_COLS)
    col_ok = (kcol >= cs) & (kcol < cs + NA_COLS)

    def body(t, carry):
        rr = t // (NA_KROWS // 2)
        a2 = t % (NA_KROWS // 2)
        r = r0 + rr
        rs = jnp.clip(r - NA_ROWS // 2, 0, rows - NA_ROWS)
        krow0 = start_row + 2 * a2
        krow = krow0 + odd
        ok = col_ok & (krow >= rs) & (krow < rs + NA_ROWS)
        e = jnp.clip(krow0 - r + NA_ROWS, 0, NA_RPB_ROWS - 1)
        for h in range(4):
            tile = jnp.broadcast_to(rpb_ref[h, pl.ds(e, 1), :], (w, PAIR))
            for bit in range(6):
                tile = jnp.where(((qcol >> bit) & 1) == 1, pltpu.roll(tile, 1 << bit, 1), tile)
            row0 = pl.multiple_of((h % 2) * NA_QROWS * w + rr * w, w)
            bias_sc[kind, h // 2, a2, pl.ds(row0, w), :] = jnp.where(ok, tile, NEG)
        return carry

    lax.fori_loop(0, NA_QROWS * (NA_KROWS // 2), body, 0)


def _na_kernel(q_ref, k_ref, v_ref, kc_ref, vc_ref, rpb_ref, o_ref, bias_sc, *, rows, ngrp):
    groups = rows // NA_QROWS
    tq = NA_QROWS * GRID_W
    span = NA_KROWS * GRID_W
    gs = [pl.program_id(1) * ngrp + j for j in range(ngrp)]
    kinds = [jnp.where(g == 0, 0, jnp.where(g == groups - 1, 2, 1)) for g in gs]
    for g, kind in zip(gs, kinds):
        @pl.when((pl.program_id(0) == 0) & ((g == 0) | (g == 1) | (g == groups - 1)))
        def _():
            _na_build_bias(rpb_ref, bias_sc, kind, g, rows)

    for j, (g, kind) in enumerate(zip(gs, kinds)):
        qrows = slice(j * tq, (j + 1) * tq)
        start = pl.multiple_of(jnp.clip(NA_QROWS * g - NA_ROWS // 2, 0, rows - NA_KROWS) * GRID_W, GRID_W)
        for p in range(2):
            lanes = slice(p * PAIR, (p + 1) * PAIR)
            qs = _stack_heads([q_ref[0, qrows, lanes]])
            kl = k_ref[0, pl.ds(start, span), lanes]
            vl = v_ref[0, pl.ds(start, span), lanes]
            bias = jnp.concatenate([bias_sc[kind, p, a2] for a2 in range(NA_KROWS // 2)], axis=1)
            s_nb = _dot_nt(qs, kl) + bias
            s_ctx = _dot_nt(qs, kc_ref[0, :, lanes])
            m = jnp.maximum(jnp.max(s_nb, axis=-1, keepdims=True), jnp.max(s_ctx, axis=-1, keepdims=True))
            p_nb = jnp.exp(s_nb - m)
            p_ctx = jnp.exp(s_ctx - m)
            denom = jnp.sum(p_nb, axis=-1, keepdims=True) + jnp.sum(p_ctx, axis=-1, keepdims=True)
            o = _dot(p_nb.astype(BF16), vl) + _dot(p_ctx.astype(BF16), vc_ref[0, :, lanes])
            o = o * (1.0 / denom)
            o_ref[0, qrows, lanes] = _unstack_heads(o, 1, tq)[0].astype(o_ref.dtype)


def _na(q, k, v, kc, vc, rpb_pairs, *, ngrp=4):
    b, s, _ = q.shape
    m = kc.shape[1]
    rows = s // GRID_W
    groups = rows // NA_QROWS
    tq = NA_QROWS * GRID_W
    assert rows % NA_QROWS == 0 and groups >= 4 and groups % ngrp == 0
    return pl.pallas_call(
        functools.partial(_na_kernel, rows=rows, ngrp=ngrp),
        grid=(b, groups // ngrp),
        in_specs=[
            pl.BlockSpec((1, ngrp * tq, GROUP_WIDTH), lambda i, g: (i, g, 0)),
            pl.BlockSpec((1, s, GROUP_WIDTH), lambda i, g: (i, 0, 0)),
            pl.BlockSpec((1, s, GROUP_WIDTH), lambda i, g: (i, 0, 0)),
            pl.BlockSpec((1, m, GROUP_WIDTH), lambda i, g: (i, 0, 0)),
            pl.BlockSpec((1, m, GROUP_WIDTH), lambda i, g: (i, 0, 0)),
            pl.BlockSpec((4, NA_RPB_ROWS, PAIR), lambda i, g: (0, 0, 0)),
        ],
        out_specs=pl.BlockSpec((1, ngrp * tq, GROUP_WIDTH), lambda i, g: (i, g, 0)),
        out_shape=jax.ShapeDtypeStruct((b, s, GROUP_WIDTH), BF16),
        scratch_shapes=[pltpu.VMEM((3, 2, NA_KROWS // 2, 2 * tq, PAIR), F32)],
        compiler_params=_cparams("arbitrary", "arbitrary"),
        name="na_attn",
    )(q, k, v, kc, vc, rpb_pairs)


def _ssd_kernel(uf_ref, ub_ref, dtf_ref, dtb_ref, dtbias_ref, alog_ref, dskip_ref, h0f_ref, h0b_ref,
                yf_ref, yb_ref, hf_ref, hb_ref, hf_sc, hb_sc, *, steps, group):
    i = pl.program_id(1)
    q = SSD_CHUNK

    @pl.when(i == 0)
    def _():
        hf_sc[...] = h0f_ref[0]
        hb_sc[...] = h0b_ref[0]

    ii = lax.broadcasted_iota(jnp.int32, (q, q), 0)
    jj = lax.broadcasted_iota(jnp.int32, (q, q), 1)
    lo_lanes = jj < HEAD_DIM
    a_coef = -jnp.exp(alog_ref[...])

    def chunk(u, dt_raw, col0, reverse, h):
        causal = (jj >= ii) if reverse else (jj <= ii)
        dt = _softplus(dt_raw + dtbias_ref[...])
        tri = jnp.where(causal, 1.0, 0.0).astype(BF16)
        a_hi, a_mid, a_lo = _split3(dt * a_coef)
        cum = _dot(tri, a_hi) + _dot(tri, a_mid) + _dot(tri, a_lo)
        cum_t = cum.T
        dt_t = dt.T
        end = cum[0:1, :] if reverse else cum[q - 1:q, :]
        xs = u[:, :2 * PAIR]
        ys, h_out = [], []
        for k in range(2):
            bk = u[:, 2 * PAIR + k * SSD_STATE: 2 * PAIR + (k + 1) * SSD_STATE]
            ck = u[:, 2 * PAIR + 2 * SSD_STATE + k * SSD_STATE: 2 * PAIR + 2 * SSD_STATE + (k + 1) * SSD_STATE]
            xk_b = xs[:, k * PAIR:(k + 1) * PAIR]
            xk = xk_b.astype(F32)
            ck_b = ck
            cb = _dot_nt(ck_b, bk)
            c0 = col0 + 2 * k
            cols = []
            for r in range(2):
                c = c0 + r
                seg = cum[:, c:c + 1] - cum_t[c:c + 1, :]
                lmat = jnp.exp(jnp.where(causal, seg, NEG))
                att = (cb * lmat * dt_t[c:c + 1, :]).astype(BF16)
                cols.append(_dot(att, xk_b))
            y_intra = jnp.where(lo_lanes, cols[0], cols[1])
            e_in = jnp.where(lo_lanes, jnp.exp(cum[:, c0:c0 + 1]), jnp.exp(cum[:, c0 + 1:c0 + 2]))
            h_t = h[k]
            y_state = _dot(ck_b, h_t.astype(BF16)) * e_in
            w0 = jnp.exp(end[:, c0:c0 + 1] - cum[:, c0:c0 + 1]) * dt[:, c0:c0 + 1]
            w1 = jnp.exp(end[:, c0 + 1:c0 + 2] - cum[:, c0 + 1:c0 + 2]) * dt[:, c0 + 1:c0 + 2]
            xw = (xk * jnp.where(lo_lanes, w0, w1)).astype(BF16)
            st = _dot(bk.astype(F32).T.astype(BF16), xw)
            decay = jnp.where(lo_lanes[0:1], jnp.exp(end[:, c0:c0 + 1]), jnp.exp(end[:, c0 + 1:c0 + 2]))
            h_out.append(h_t * decay + st)
            ys.append(y_intra + y_state)
        return ys, xs, h_out

    dskip = dskip_ref[...]
    h = [hf_sc[0], hf_sc[1]]
    for j in range(group):
        rows = slice(j * q, (j + 1) * q)
        ys, xs, h = chunk(uf_ref[0, rows, :], dtf_ref[0, rows, :], 0, False, h)
        for k in range(2):
            lanes = slice(k * PAIR, (k + 1) * PAIR)
            yf_ref[0, rows, lanes] = ys[k] + dskip[:, lanes] * xs[:, lanes].astype(F32)
    hf_sc[0], hf_sc[1] = h

    h = [hb_sc[0], hb_sc[1]]
    for j in reversed(range(group)):
        rows = slice(j * q, (j + 1) * q)
        ys, _, h = chunk(ub_ref[0, rows, :], dtb_ref[0, rows, :], 4, True, h)
        for k in range(2):
            yb_ref[0, rows, k * PAIR:(k + 1) * PAIR] = ys[k]
    hb_sc[0], hb_sc[1] = h

    @pl.when(i == steps - 1)
    def _():
        hf_ref[0] = hf_sc[...]
        hb_ref[0] = hb_sc[...]


def _ssd(u, dt, dt_bias, a_log, dskip, h0f, h0b, *, group=8):
    b, length, ch = u.shape
    nc = length // SSD_CHUNK
    while nc % group:
        group //= 2
    steps = nc // group
    q = group * SSD_CHUNK
    fwd = lambda i, c: (i, c, 0)
    bwd = lambda i, c: (i, steps - 1 - c, 0)
    const = lambda i, c: (0, 0)
    state = lambda i, c: (i, 0, 0, 0)
    state_shape = (b, 2, SSD_STATE, PAIR)
    return pl.pallas_call(
        functools.partial(_ssd_kernel, steps=steps, group=group),
        grid=(b, steps),
        in_specs=[
            pl.BlockSpec((1, q, ch), fwd),
            pl.BlockSpec((1, q, ch), bwd),
            pl.BlockSpec((1, q, DT_PAD), fwd),
            pl.BlockSpec((1, q, DT_PAD), bwd),
            pl.BlockSpec((1, DT_PAD), const),
            pl.BlockSpec((1, DT_PAD), const),
            pl.BlockSpec((1, GROUP_WIDTH), const),
            pl.BlockSpec((1, 2, SSD_STATE, PAIR), state),
            pl.BlockSpec((1, 2, SSD_STATE, PAIR), state),
        ],
        out_specs=[
            pl.BlockSpec((1, q, GROUP_WIDTH), fwd),
            pl.BlockSpec((1, q, GROUP_WIDTH), bwd),
            pl.BlockSpec((1, 2, SSD_STATE, PAIR), state),
            pl.BlockSpec((1, 2, SSD_STATE, PAIR), state),
        ],
        out_shape=[
            jax.ShapeDtypeStruct((b, length, GROUP_WIDTH), F32),
            jax.ShapeDtypeStruct((b, length, GROUP_WIDTH), F32),
            jax.ShapeDtypeStruct(state_shape, F32),
            jax.ShapeDtypeStruct(state_shape, F32),
        ],
        scratch_shapes=[pltpu.VMEM((2, SSD_STATE, PAIR), F32), pltpu.VMEM((2, SSD_STATE, PAIR), F32)],
        compiler_params=_cparams("arbitrary", "arbitrary"),
        name="ssd_scan",
    )(u, u, dt, dt, dt_bias, a_log, dskip, h0f, h0b)


def _head_block_index(start, perm):
    return np.concatenate([np.arange(start + h * HEAD_DIM, start + (h + 1) * HEAD_DIM) for h in perm])


def _inproj_columns():
    cols = [
        _head_block_index(0, HEAD_PERM), np.arange(256, 1536),
        _head_block_index(1544, HEAD_PERM), np.arange(1800, 2824),
    ]
    return np.concatenate(cols), np.arange(1536, 1544)


def _outproj_rows():
    return np.concatenate([
        _head_block_index(0, HEAD_PERM), np.arange(256, 512),
        _head_block_index(512, HEAD_PERM), np.arange(768, 1024),
    ])


def _rope_tables(seq):
    t = np.arange(seq)
    quarter = HEAD_DIM // 4
    inv = ROPE_BASE ** (-jnp.arange(quarter, dtype=F32) / quarter)
    a_row = jnp.asarray(t // GRID_W, F32)[:, None] * inv
    a_col = jnp.asarray(t % GRID_W, F32)[:, None] * inv
    cos = jnp.concatenate([jnp.cos(a_row), jnp.cos(a_row), jnp.cos(a_col), jnp.cos(a_col)], axis=1)
    sin = jnp.concatenate([-jnp.sin(a_row), jnp.sin(a_row), -jnp.sin(a_col), jnp.sin(a_col)], axis=1)
    return jnp.tile(cos, (1, 2)), jnp.tile(sin, (1, 2))


def _pad_lanes(v, width):
    v = v.reshape(1, -1).astype(F32)
    return jnp.pad(v, ((0, 0), (0, width - v.shape[1])))


def _token_tile(n, cap):
    t = cap
    while n % t:
        t //= 2
    return t


def kernel(x, c, ctx, c_ctx, ada_w, ada_b, ln_g, ln_b, ffn1_w_in, ffn1_w_out, mix_w_in, mix_w_out,
           mix_norm_g, swa_sink, ssd_conv_w, ssd_conv_b, ssd_dt_bias, ssd_A_log, ssd_D,
           gqa_q_norm, gqa_k_norm, na_rpb, ffn2_w_in, ffn2_w_out):
    bsz, seq, d = x.shape
    m_ctx = ctx.shape[1]
    depth = ada_w.shape[0]
    alpha = float((2 * depth) ** 0.25)
    mod_rows = 16 * ((bsz + 1 + 15) // 16)
    cc = jnp.concatenate([c, c_ctx[None], jnp.zeros((mod_rows - bsz - 1, d), F32)], axis=0)
    mods = _ada(cc, ada_w, ada_b)

    cos_x, sin_x = _rope_tables(seq)
    tm_x = _token_tile(seq, 512)
    tm_c = _token_tile(m_ctx, 256)
    cos_c = jnp.ones((tm_c, PAIR), F32)
    sin_c = jnp.zeros((tm_c, PAIR), F32)
    blockdiag = jnp.asarray(np.kron(np.eye(2), np.ones((HEAD_DIM, HEAD_DIM))), BF16)
    main_cols, dt_cols = _inproj_columns()
    out_rows = _outproj_rows()

    xt = x.reshape(bsz * seq, d)
    ct = ctx.reshape(bsz * m_ctx, d)
    zero_state = jnp.zeros((bsz, 2, SSD_STATE, PAIR), F32)

    for l in range(depth):
        last = l == depth - 1
        mod_x = mods[l, :bsz].reshape(bsz, N_MOD, d)
        mod_c = mods[l, bsz:bsz + 1].reshape(1, N_MOD, d)
        w1_in, w1_out = ffn1_w_in[l].astype(BF16), ffn1_w_out[l].astype(BF16)
        w2_in, w2_out = ffn2_w_in[l].astype(BF16), ffn2_w_out[l].astype(BF16)
        w_mix = mix_w_in[l]
        w_mix = jnp.concatenate(
            [w_mix[:, main_cols], jnp.pad(w_mix[:, dt_cols], ((0, 0), (0, DT_PAD - dt_cols.size)))],
            axis=1).astype(BF16)
        w_o = mix_w_out[l][out_rows].astype(BF16)
        norm_g = mix_norm_g[l][out_rows].reshape(1, d)
        qg = jnp.tile(gqa_q_norm[l], 2).reshape(1, PAIR)
        kg = jnp.tile(gqa_k_norm[l], 2).reshape(1, PAIR)
        conv_w = jnp.pad(ssd_conv_w[l].reshape(SSD_CONV, SSD_CONV_CH), ((0, 8 - SSD_CONV), (0, 0)))
        conv_b = ssd_conv_b[l].reshape(1, SSD_CONV_CH)
        dt_bias = _pad_lanes(ssd_dt_bias[l], DT_PAD)
        a_log = _pad_lanes(ssd_A_log[l], DT_PAD)
        dskip = jnp.repeat(ssd_D[l], HEAD_DIM).reshape(1, GROUP_WIDTH)
        sink = swa_sink[l].astype(F32)
        rpb_pairs = _na_rpb_pairs(na_rpb[l])

        ffn_x = functools.partial(_ffn, tokens_per_mod=seq, alpha=alpha, tm=tm_x)
        ffn_c = functools.partial(_ffn, tokens_per_mod=bsz * m_ctx, alpha=alpha, tm=tm_c)

        xt = ffn_x(xt, mod_x, (0, 1, 2), w1_in, w1_out, ln_g[l, 0], ln_b[l, 0])
        ct = ffn_c(ct, mod_c, (0, 1, 2), w1_in, w1_out, ln_g[l, 0], ln_b[l, 0])

        px = _inproj(xt, mod_x, w_mix, cos_x, sin_x, qg, kg, blockdiag, conv_w, conv_b,
                     tokens_per_mod=seq, seq_len=seq, tm=tm_x, gqa_q_scale=Q_SCALE * LOG2E)
        pc = _inproj(ct, mod_c, w_mix, cos_c, sin_c, qg, kg, blockdiag, conv_w, conv_b,
                     tokens_per_mod=bsz * m_ctx, seq_len=m_ctx, tm=tm_c, gqa_q_scale=Q_SCALE)
        aq, ak, av, bz, bu, bdt, cq, ck, cv, dq, dk, dv = [
            t.reshape(bsz, seq, t.shape[-1]) for t in px]
        aq_c, ak_c, av_c, bz_c, bu_c, bdt_c, cq_c, ck_c, cv_c, dq_c, dk_c, dv_c = [
            t.reshape(bsz, m_ctx, t.shape[-1]) for t in pc]

        ssd = functools.partial(_ssd, dt_bias=dt_bias, a_log=a_log, dskip=dskip)
        yf_c, yb_c, hf_c, hb_c = ssd(bu_c, bdt_c, h0f=zero_state, h0b=zero_state)
        yf, yb, _, _ = ssd(bu, bdt, h0f=hf_c, h0b=hb_c)

        oa = _swa(sink, aq, ak, av, ak_c, av_c)
        oc = _gqa(cq, ck, cv, ck_c, cv_c)
        od = _na(dq, dk, dv, dk_c, dv_c, rpb_pairs)

        flat = lambda t: t.reshape(-1, t.shape[-1])
        xt = _mixer_ffn(xt, mod_x, flat(oa), flat(yf), flat(yb), flat(bz), flat(oc), flat(od), norm_g, w_o,
                        ln_g[l, 1], ln_b[l, 1], w2_in, w2_out, ln_g[l, 2], ln_b[l, 2],
                        tokens_per_mod=seq, alpha=alpha, tm=tm_x)

        if not last:
            tq_c = _token_tile(m_ctx, 128)
            oa_c = _dense_attn(aq_c, [(ak_c, av_c)], kv_pairs=1, sink=sink, tq=tq_c)
            oc_c = _dense_attn(cq_c, [(ck_c, cv_c)], kv_pairs=1, tq=tq_c)
            od_c = _dense_attn(dq_c, [(dk_c, dv_c)], kv_pairs=2, tq=tq_c)
            ct = _mixer_ffn(ct, mod_c, flat(oa_c), flat(yf_c), flat(yb_c), flat(bz_c), flat(oc_c),
                            flat(od_c), norm_g, w_o, ln_g[l, 1], ln_b[l, 1], w2_in, w2_out,
                            ln_g[l, 2], ln_b[l, 2], tokens_per_mod=bsz * m_ctx, alpha=alpha, tm=tm_c)

    return xt.reshape(bsz, seq, d)
```

```python
import functools

import numpy as np
import jax
import jax.numpy as jnp
from jax import lax
from jax.experimental import pallas as pl
from jax.experimental.pallas import tpu as pltpu

F32 = jnp.float32
BF16 = jnp.bfloat16

HEAD_DIM = 64
PAIR = 2 * HEAD_DIM
GROUP_WIDTH = 256
GRID_W = 64
N_MOD = 9
SWA_WINDOW = 128
SWA_BLOCK = 128
SSD_CHUNK = 128
SSD_STATE = 128
SSD_CONV = 5
CONV_HALO = 8
FFN_SUBTILE = 256
SSD_CONV_CH = 768
NA_ROWS = 8
NA_COLS = 16
NA_QROWS = 4
NA_KROWS = 12
ROPE_BASE = 10000.0
EPS = 1e-5
NEG = -1e30
Q_SCALE = HEAD_DIM ** -0.5
LOG2E = 1.4426950408889634
DT_PAD = 128
DT_ROWS = 16
HEAD_PERM = (0, 2, 1, 3)

VMEM_LIMIT_BYTES = 56 * 1024 * 1024


def _cparams(*sem):
    return pltpu.CompilerParams(dimension_semantics=sem, vmem_limit_bytes=VMEM_LIMIT_BYTES)


def _dot(a, b):
    return jnp.dot(a, b, preferred_element_type=F32)


def _dot_nt(a, b):
    return lax.dot_general(a, b, (((1,), (1,)), ((), ())), preferred_element_type=F32)


def _sigmoid(x):
    return 1.0 / (1.0 + jnp.exp(-x))


def _silu(x):
    return x * _sigmoid(x)


def _softplus(x):
    return jnp.maximum(x, 0.0) + jnp.log(1.0 + jnp.exp(-jnp.abs(x)))


def _split3(a):
    hi = a.astype(BF16)
    r1 = a - hi.astype(F32)
    mid = r1.astype(BF16)
    lo = (r1 - mid.astype(F32)).astype(BF16)
    return hi, mid, lo


def _layer_norm(y, g, b):
    mu = jnp.mean(y, axis=-1, keepdims=True)
    d = y - mu
    var = jnp.mean(d * d, axis=-1, keepdims=True)
    return d * lax.rsqrt(var + EPS) * g + b


def _ada_kernel(c_ref, w_ref, b_ref, o_ref):
    s = _silu(c_ref[...])
    s_hi = s.astype(BF16)
    s_lo = (s - s_hi.astype(F32)).astype(BF16)
    w = w_ref[0]
    w_hi = w.astype(BF16)
    w_lo = (w - w_hi.astype(F32)).astype(BF16)
    o_ref[0] = _dot(s_hi, w_hi) + _dot(s_lo, w_hi) + _dot(s_hi, w_lo) + b_ref[0]


def _ada(cc, ada_w, ada_b):
    depth, d, n = ada_w.shape
    rows = cc.shape[0]
    tn = 1024
    return pl.pallas_call(
        _ada_kernel,
        grid=(depth, n // tn),
        in_specs=[
            pl.BlockSpec((rows, d), lambda l, j: (0, 0)),
            pl.BlockSpec((1, d, tn), lambda l, j: (l, 0, j)),
            pl.BlockSpec((1, 1, tn), lambda l, j: (l, 0, j)),
        ],
        out_specs=pl.BlockSpec((1, rows, tn), lambda l, j: (l, 0, j)),
        out_shape=jax.ShapeDtypeStruct((depth, rows, n), F32),
        compiler_params=_cparams("arbitrary", "arbitrary"),
        name="ada_mod",
    )(cc, ada_w, ada_b.reshape(depth, 1, n))


def _ffn_rows(x, mod_ref, rows, wa_ref, wu_ref, wo_ref, g, b, alpha):
    r_shift, r_scale, r_gate = rows
    h = x * (1.0 + mod_ref[0, r_scale:r_scale + 1, :]) + mod_ref[0, r_shift:r_shift + 1, :]
    h = h.astype(BF16)
    a = _dot(h, wa_ref[...])
    u = _dot(h, wu_ref[...])
    gated = (_silu(a) * u).astype(BF16)
    f = _dot(gated, wo_ref[...])
    y = alpha * x + (0.5 * mod_ref[0, r_gate:r_gate + 1, :]) * f
    return _layer_norm(y, g, b)


def _mixer_rows(x, mod_ref, parts, ng_ref, w_ref, g, b, alpha):
    acc = None
    for gi, y in enumerate(parts):
        rows = slice(gi * GROUP_WIDTH, (gi + 1) * GROUP_WIDTH)
        ms = jnp.mean(y * y, axis=-1, keepdims=True)
        yn = (y * lax.rsqrt(ms + EPS) * ng_ref[:, rows]).astype(BF16)
        term = _dot(yn, w_ref[rows, :])
        acc = term if acc is None else acc + term
    return _layer_norm(alpha * x + mod_ref[0, 5:6, :] * acc, g, b)


def _subtiles(tm):
    sub = FFN_SUBTILE if tm % FFN_SUBTILE == 0 else tm
    return [slice(r0, r0 + sub) for r0 in range(0, tm, sub)]


def _ffn_kernel(x_ref, mod_ref, wa_ref, wu_ref, wo_ref, g_ref, b_ref, o_ref, *, rows, alpha):
    for r in _subtiles(x_ref.shape[0]):
        o_ref[r, :] = _ffn_rows(x_ref[r, :], mod_ref, rows, wa_ref, wu_ref, wo_ref,
                                g_ref[...], b_ref[...], alpha)


def _mixer_ffn_kernel(x_ref, mod_ref, oa_ref, yf_ref, yb_ref, z_ref, oc_ref, od_ref, ng_ref, wm_ref,
                      g1_ref, b1_ref, wa_ref, wu_ref, wo_ref, g2_ref, b2_ref, o_ref, *, alpha):
    for r in _subtiles(x_ref.shape[0]):
        ob = (yf_ref[r, :] + yb_ref[r, :]) * _silu(z_ref[r, :].astype(F32))
        parts = (oa_ref[r, :].astype(F32), ob, oc_ref[r, :].astype(F32), od_ref[r, :].astype(F32))
        x_mid = _mixer_rows(x_ref[r, :], mod_ref, parts, ng_ref, wm_ref, g1_ref[...], b1_ref[...], alpha)
        o_ref[r, :] = _ffn_rows(x_mid, mod_ref, (6, 7, 8), wa_ref, wu_ref, wo_ref,
                                g2_ref[...], b2_ref[...], alpha)


def _layer_weight(shape, layer, col_block=0):
    return pl.BlockSpec((None,) + shape, lambda i: (layer, 0, col_block), pipeline_mode=pl.Buffered(1))


def _ffn(xt, mod, rows, w_in, w_out, layer, ln_g, ln_b, *, tokens_per_mod, alpha, tm):
    t, d = xt.shape
    dff = w_out.shape[1]
    tiles_per_mod = tokens_per_mod // tm
    return pl.pallas_call(
        functools.partial(_ffn_kernel, rows=rows, alpha=alpha),
        grid=(t // tm,),
        in_specs=[
            pl.BlockSpec((tm, d), lambda i: (i, 0)),
            pl.BlockSpec((1, N_MOD, d), lambda i: (i // tiles_per_mod, 0, 0)),
            _layer_weight((d, dff), layer, 0),
            _layer_weight((d, dff), layer, 1),
            _layer_weight((dff, d), layer),
            pl.BlockSpec((1, d), lambda i: (0, 0)),
            pl.BlockSpec((1, d), lambda i: (0, 0)),
        ],
        out_specs=pl.BlockSpec((tm, d), lambda i: (i, 0)),
        out_shape=jax.ShapeDtypeStruct((t, d), F32),
        compiler_params=_cparams("arbitrary"),
        name="ffn",
    )(xt, mod, w_in, w_in, w_out, ln_g.reshape(1, d), ln_b.reshape(1, d))


def _mixer_ffn(xt, mod, oa, yf, yb, z, oc, od, norm_g, w_mix_out, ln1_g, ln1_b, w_in, w_out, layer,
               ln2_g, ln2_b, *, tokens_per_mod, alpha, tm):
    t, d = xt.shape
    dff = w_out.shape[1]
    tiles_per_mod = tokens_per_mod // tm
    row = lambda i: (i, 0)
    const = lambda i: (0, 0)
    grp = pl.BlockSpec((tm, GROUP_WIDTH), row)
    vec = pl.BlockSpec((1, d), const)
    return pl.pallas_call(
        functools.partial(_mixer_ffn_kernel, alpha=alpha),
        grid=(t // tm,),
        in_specs=[
            pl.BlockSpec((tm, d), row),
            pl.BlockSpec((1, N_MOD, d), lambda i: (i // tiles_per_mod, 0, 0)),
            grp, grp, grp, grp, grp, grp,
            vec,
            _layer_weight((d, d), layer),
            vec, vec,
            _layer_weight((d, dff), layer, 0),
            _layer_weight((d, dff), layer, 1),
            _layer_weight((dff, d), layer),
            vec, vec,
        ],
        out_specs=pl.BlockSpec((tm, d), row),
        out_shape=jax.ShapeDtypeStruct((t, d), F32),
        compiler_params=_cparams("arbitrary"),
        name="mixer_ffn",
    )(xt, mod, oa, yf, yb, z, oc, od, norm_g, w_mix_out, ln1_g.reshape(1, d), ln1_b.reshape(1, d),
      w_in, w_in, w_out, ln2_g.reshape(1, d), ln2_b.reshape(1, d))


_P_AQ, _P_AK, _P_AV, _P_Z, _P_XBC = 0, 256, 384, 512, 768
_P_CQ, _P_CK, _P_CV, _P_DQ, _P_DK, _P_DV, _P_DT, _P_END = 1536, 1792, 1920, 2048, 2304, 2560, 2816, 2944


def _rope_pair(t, cos, sin):
    lane = lax.broadcasted_iota(jnp.int32, t.shape, 1)
    up = pltpu.roll(t, HEAD_DIM // 4, 1)
    dn = pltpu.roll(t, PAIR - HEAD_DIM // 4, 1)
    partner = jnp.where((lane % 32) < 16, dn, up)
    return t * cos + partner * sin


def _rms_pair(t, g, blockdiag):
    sq = t * t
    hi = sq.astype(BF16)
    lo = (sq - hi.astype(F32)).astype(BF16)
    ss = _dot(hi, blockdiag) + _dot(lo, blockdiag)
    return t * lax.rsqrt(ss * (1.0 / HEAD_DIM) + EPS) * g


def _inproj_kernel(x_ref, xp_ref, xn_ref, mod_ref, w_ref, cos_ref, sin_ref, qg_ref, kg_ref, bd_ref,
                   cw_ref, cb_ref,
                   aq_ref, ak_ref, av_ref, z_ref, u_ref, dt_ref,
                   cq_ref, ck_ref, cv_ref, dq_ref, dk_ref, dv_ref, ext_sc, *, gqa_q_scale, tiles_per_seq):
    tm = x_ref.shape[0]
    halo = xp_ref.shape[0]
    pos = pl.program_id(0) % tiles_per_seq
    x_ext = jnp.concatenate([xp_ref[...], x_ref[...], xn_ref[...]], axis=0)
    h = (x_ext * (1.0 + mod_ref[0, 4:5, :]) + mod_ref[0, 3:4, :]).astype(BF16)
    y_ext = _dot(h, w_ref[...])
    y = y_ext[halo:halo + tm]
    cos = cos_ref[...]
    sin = sin_ref[...]
    bd = bd_ref[...]

    def pairs(lo, n):
        return [y[:, lo + PAIR * p: lo + PAIR * (p + 1)] for p in range(n)]

    xbc = y_ext[:, _P_XBC:_P_XBC + SSD_CONV_CH]
    ext_sc[:halo] = jnp.where(pos == 0, 0.0, xbc[:halo])
    ext_sc[halo:halo + tm] = xbc[halo:halo + tm]
    ext_sc[halo + tm:] = jnp.where(pos == tiles_per_seq - 1, 0.0, xbc[halo + tm:])
    conv = cb_ref[...]
    for k in range(SSD_CONV):
        lo = halo - SSD_CONV // 2 + k
        conv = conv + cw_ref[k:k + 1, :] * ext_sc[lo:lo + tm, :]
    u_ref[...] = _silu(conv).astype(u_ref.dtype)

    def put(ref, parts, scale=None):
        for p, part in enumerate(parts):
            if scale is not None:
                part = part * scale
            ref[:, PAIR * p: PAIR * (p + 1)] = part.astype(ref.dtype)

    swa = pairs(_P_AQ, 4)
    put(aq_ref, [_rope_pair(t, cos, sin) for t in swa[:2]], Q_SCALE)
    put(ak_ref, [_rope_pair(swa[2], cos, sin)])
    put(av_ref, [swa[3]])
    put(z_ref, pairs(_P_Z, 2))
    put(dt_ref, pairs(_P_DT, 1))
    gqa = pairs(_P_CQ, 4)
    put(cq_ref, [_rope_pair(_rms_pair(t, qg_ref[...], bd), cos, sin) for t in gqa[:2]], gqa_q_scale)
    put(ck_ref, [_rope_pair(_rms_pair(gqa[2], kg_ref[...], bd), cos, sin)])
    put(cv_ref, [gqa[3]])
    na = pairs(_P_DQ, 6)
    put(dq_ref, na[:2], Q_SCALE)
    put(dk_ref, na[2:4])
    put(dv_ref, na[4:])


def _inproj(xt, mod, w, layer, cos, sin, qg, kg, blockdiag, conv_w, conv_b, *, tokens_per_mod, seq_len, tm,
            gqa_q_scale):
    t, d = xt.shape
    tiles_per_mod = tokens_per_mod // tm
    table_tiles = cos.shape[0] // tm
    widths = (256, 128, 128, 256, SSD_CONV_CH, DT_PAD, 256, 128, 128, 256, 256, 256)
    dtypes = (BF16,) * 5 + (F32,) + (BF16,) * 6
    per = tm // CONV_HALO
    last_halo = t // CONV_HALO - 1
    row = lambda i: (i, 0)
    const = lambda i: (0, 0)
    return pl.pallas_call(
        functools.partial(_inproj_kernel, gqa_q_scale=gqa_q_scale, tiles_per_seq=seq_len // tm),
        grid=(t // tm,),
        in_specs=[
            pl.BlockSpec((tm, d), row),
            pl.BlockSpec((CONV_HALO, d), lambda i: (jnp.maximum(i * per - 1, 0), 0)),
            pl.BlockSpec((CONV_HALO, d), lambda i: (jnp.minimum((i + 1) * per, last_halo), 0)),
            pl.BlockSpec((1, N_MOD, d), lambda i: (i // tiles_per_mod, 0, 0)),
            _layer_weight((d, _P_END), layer),
            pl.BlockSpec((tm, PAIR), lambda i: (i % table_tiles, 0)),
            pl.BlockSpec((tm, PAIR), lambda i: (i % table_tiles, 0)),
            pl.BlockSpec((1, PAIR), const),
            pl.BlockSpec((1, PAIR), const),
            pl.BlockSpec((PAIR, PAIR), const),
            pl.BlockSpec((8, SSD_CONV_CH), const),
            pl.BlockSpec((1, SSD_CONV_CH), const),
        ],
        out_specs=[pl.BlockSpec((tm, wd), row) for wd in widths],
        out_shape=[jax.ShapeDtypeStruct((t, wd), dt) for wd, dt in zip(widths, dtypes)],
        scratch_shapes=[pltpu.VMEM((tm + 2 * CONV_HALO, SSD_CONV_CH), F32)],
        compiler_params=_cparams("arbitrary"),
        name="inproj",
    )(xt, xt, xt, mod, w, cos, sin, qg, kg, blockdiag, conv_w, conv_b)


def _stack_heads(q_pairs):
    lane = lax.broadcasted_iota(jnp.int32, q_pairs[0].shape, 1)
    lo = lane < HEAD_DIM
    zero = jnp.zeros_like(q_pairs[0])
    blocks = []
    for qp in q_pairs:
        blocks.append(jnp.where(lo, qp, zero))
        blocks.append(jnp.where(lo, zero, qp))
    return jnp.concatenate(blocks, axis=0)


def _unstack_heads(o, n_pairs, tq):
    lane = lax.broadcasted_iota(jnp.int32, (tq, PAIR), 1)
    lo = lane < HEAD_DIM
    return [jnp.where(lo, o[2 * p * tq:(2 * p + 1) * tq], o[(2 * p + 1) * tq:(2 * p + 2) * tq])
            for p in range(n_pairs)]


def _sink_column(sink_ref, tq):
    blk = lax.broadcasted_iota(jnp.int32, (4 * tq, 1), 0) // tq
    col = jnp.full((4 * tq, 1), sink_ref[HEAD_PERM[3]], F32)
    for b in (2, 1, 0):
        col = jnp.where(blk == b, sink_ref[HEAD_PERM[b]], col)
    return col


def _swa_kernel(sink_ref, q_ref, k_ref, v_ref, kc_ref, vc_ref, o_ref, *, seq, nblk):
    blk = SWA_BLOCK
    span = 3 * blk
    sk = _sink_column(sink_ref, blk)
    for j in range(nblk):
        n = pl.program_id(1) * nblk + j
        rows = slice(j * blk, (j + 1) * blk)
        start = pl.multiple_of(jnp.clip((n - 1) * blk, 0, seq - span), blk)
        kl = k_ref[0, pl.ds(start, span), :]
        vl = v_ref[0, pl.ds(start, span), :]
        q = q_ref[0, rows, :]
        qs = _stack_heads([q[:, :PAIR], q[:, PAIR:]])
        s_loc = _dot_nt(qs, kl)
        s_ctx = _dot_nt(qs, kc_ref[0])
        qpos = n * blk + lax.broadcasted_iota(jnp.int32, s_loc.shape, 0) % blk
        kpos = start + lax.broadcasted_iota(jnp.int32, s_loc.shape, 1)
        s_loc = jnp.where(jnp.abs(kpos - qpos) <= SWA_WINDOW, s_loc, NEG)
        m = jnp.maximum(jnp.maximum(jnp.max(s_loc, axis=-1, keepdims=True),
                                    jnp.max(s_ctx, axis=-1, keepdims=True)), sk)
        p_loc = jnp.exp(s_loc - m)
        p_ctx = jnp.exp(s_ctx - m)
        denom = (jnp.sum(p_loc, axis=-1, keepdims=True) + jnp.sum(p_ctx, axis=-1, keepdims=True)
                 + jnp.exp(sk - m))
        o = _dot(p_loc.astype(BF16), vl) + _dot(p_ctx.astype(BF16), vc_ref[0])
        o = o * (1.0 / denom)
        o_a, o_b = _unstack_heads(o, 2, blk)
        o_ref[0, rows, :PAIR] = o_a.astype(o_ref.dtype)
        o_ref[0, rows, PAIR:] = o_b.astype(o_ref.dtype)


def _swa(sink, q, k, v, kc, vc, *, nblk=8):
    b, s, _ = q.shape
    m = kc.shape[1]
    blk = SWA_BLOCK * nblk
    assert s % blk == 0 and s >= 3 * SWA_BLOCK
    return pl.pallas_call(
        functools.partial(_swa_kernel, seq=s, nblk=nblk),
        grid=(b, s // blk),
        in_specs=[
            pl.BlockSpec(memory_space=pltpu.SMEM),
            pl.BlockSpec((1, blk, GROUP_WIDTH), lambda i, n: (i, n, 0)),
            pl.BlockSpec((1, s, PAIR), lambda i, n: (i, 0, 0)),
            pl.BlockSpec((1, s, PAIR), lambda i, n: (i, 0, 0)),
            pl.BlockSpec((1, m, PAIR), lambda i, n: (i, 0, 0)),
            pl.BlockSpec((1, m, PAIR), lambda i, n: (i, 0, 0)),
        ],
        out_specs=pl.BlockSpec((1, blk, GROUP_WIDTH), lambda i, n: (i, n, 0)),
        out_shape=jax.ShapeDtypeStruct((b, s, GROUP_WIDTH), BF16),
        compiler_params=_cparams("arbitrary", "arbitrary"),
        name="swa_attn",
    )(sink, q, k, v, kc, vc)


def _dense_attn_kernel(*refs, kv_pairs, n_src, has_sink, tq, tk):
    refs = list(refs)
    sink_ref = refs.pop(0) if has_sink else None
    q_ref = refs.pop(0)
    srcs = [(refs[2 * i], refs[2 * i + 1]) for i in range(n_src)]
    o_ref = refs[2 * n_src]
    q = q_ref[0]
    if kv_pairs == 1:
        units = [([q[:, :PAIR], q[:, PAIR:]], 0)]
    else:
        units = [([q[:, :PAIR]], 0), ([q[:, PAIR:]], 1)]
    outs = []
    for q_pairs, kv in units:
        qs = _stack_heads(q_pairs)
        nrow = qs.shape[0]
        if has_sink:
            m = _sink_column(sink_ref, tq)
            l = jnp.ones((nrow, 1), F32)
        else:
            m = jnp.full((nrow, 1), NEG, F32)
            l = jnp.zeros((nrow, 1), F32)
        acc = jnp.zeros((nrow, PAIR), F32)
        for k_ref, v_ref in srcs:
            nk = k_ref.shape[1]
            step = min(tk, nk)
            for c in range(nk // step):
                kch = k_ref[0, c * step:(c + 1) * step, kv * PAIR:(kv + 1) * PAIR]
                vch = v_ref[0, c * step:(c + 1) * step, kv * PAIR:(kv + 1) * PAIR]
                s = _dot_nt(qs, kch)
                m_new = jnp.maximum(m, jnp.max(s, axis=-1, keepdims=True))
                a = jnp.exp(m - m_new)
                p = jnp.exp(s - m_new)
                l = a * l + jnp.sum(p, axis=-1, keepdims=True)
                acc = a * acc + _dot(p.astype(BF16), vch)
                m = m_new
        o = acc * (1.0 / l)
        outs += _unstack_heads(o, len(q_pairs), tq)
    o_ref[0, :, :PAIR] = outs[0].astype(o_ref.dtype)
    o_ref[0, :, PAIR:] = outs[1].astype(o_ref.dtype)


def _dense_attn(q, srcs, *, kv_pairs, sink=None, tq=128, tk=512):
    b, s, _ = q.shape
    kvw = kv_pairs * PAIR
    has_sink = sink is not None
    in_specs, args = [], []
    if has_sink:
        in_specs.append(pl.BlockSpec(memory_space=pltpu.SMEM))
        args.append(sink)
    in_specs.append(pl.BlockSpec((1, tq, GROUP_WIDTH), lambda i, n: (i, n, 0)))
    args.append(q)
    for k, v in srcs:
        nk = k.shape[1]
        assert nk % min(tk, nk) == 0
        in_specs += [pl.BlockSpec((1, nk, kvw), lambda i, n: (i, 0, 0))] * 2
        args += [k, v]
    return pl.pallas_call(
        functools.partial(_dense_attn_kernel, kv_pairs=kv_pairs, n_src=len(srcs),
                          has_sink=has_sink, tq=tq, tk=tk),
        grid=(b, s // tq),
        in_specs=in_specs,
        out_specs=pl.BlockSpec((1, tq, GROUP_WIDTH), lambda i, n: (i, n, 0)),
        out_shape=jax.ShapeDtypeStruct((b, s, GROUP_WIDTH), BF16),
        compiler_params=_cparams("arbitrary", "arbitrary"),
        name="dense_attn",
    )(*args)


GQA_EXTRA_ROWS = 16


def _gqa_kernel(q_ref, k_ref, v_ref, kc_ref, vc_ref, o_ref, vt_sc, *, tq, tk):
    n = pl.program_id(1)
    seq = k_ref.shape[1]
    m_ctx = kc_ref.shape[1]

    @pl.when(n == 0)
    def _():
        for c in range(seq // PAIR):
            rows = slice(c * PAIR, (c + 1) * PAIR)
            vt_sc[:PAIR, rows] = v_ref[0, rows, :].astype(F32).T.astype(BF16)
        for c in range(m_ctx // PAIR):
            rows = slice(c * PAIR, (c + 1) * PAIR)
            vt_sc[:PAIR, seq + c * PAIR: seq + (c + 1) * PAIR] = vc_ref[0, rows, :].astype(F32).T.astype(BF16)
        ones_row = lax.broadcasted_iota(jnp.int32, (GQA_EXTRA_ROWS, seq + m_ctx), 0) == 0
        vt_sc[PAIR:, :] = jnp.where(ones_row, 1.0, 0.0).astype(BF16)

    q = q_ref[0]
    qs = _stack_heads([q[:, :PAIR], q[:, PAIR:]])
    chunks = [(k_ref, c * tk, tk, c * tk) for c in range(seq // tk)]
    chunks += [(kc_ref, c * min(tk, m_ctx), min(tk, m_ctx), seq + c * min(tk, m_ctx))
               for c in range(m_ctx // min(tk, m_ctx))]
    nrow = 4 * tq
    m = jnp.full((1, nrow), NEG, F32)
    acc = jnp.zeros((PAIR + GQA_EXTRA_ROWS, nrow), F32)
    for ref, lo, size, col in chunks:
        s_t = _dot_nt(ref[0, lo:lo + size, :], qs)
        m_new = jnp.maximum(m, jnp.max(s_t, axis=0, keepdims=True))
        p_t = jnp.exp2((s_t - m_new).astype(BF16))
        acc = jnp.exp2(m - m_new) * acc + _dot(vt_sc[:, col:col + size], p_t)
        m = m_new
    o_t = acc[:PAIR] * (1.0 / acc[PAIR:PAIR + 1])
    lo_rows = lax.broadcasted_iota(jnp.int32, (PAIR, tq), 0) < HEAD_DIM
    for p in range(2):
        pair_t = jnp.where(lo_rows, o_t[:, 2 * p * tq:(2 * p + 1) * tq],
                           o_t[:, (2 * p + 1) * tq:(2 * p + 2) * tq])
        o_ref[0, :, p * PAIR:(p + 1) * PAIR] = pair_t.T.astype(o_ref.dtype)


def _gqa(q, k, v, kc, vc, *, tq=256, tk=1024):
    b, s, _ = q.shape
    m = kc.shape[1]
    tq = _token_tile(s, tq)
    assert s % tk == 0 and s % PAIR == 0 and m % PAIR == 0 and m % min(tk, m) == 0
    full = lambda i, n: (i, 0, 0)
    return pl.pallas_call(
        functools.partial(_gqa_kernel, tq=tq, tk=tk),
        grid=(b, s // tq),
        in_specs=[
            pl.BlockSpec((1, tq, GROUP_WIDTH), lambda i, n: (i, n, 0)),
            pl.BlockSpec((1, s, PAIR), full),
            pl.BlockSpec((1, s, PAIR), full),
            pl.BlockSpec((1, m, PAIR), full),
            pl.BlockSpec((1, m, PAIR), full),
        ],
        out_specs=pl.BlockSpec((1, tq, GROUP_WIDTH), lambda i, n: (i, n, 0)),
        out_shape=jax.ShapeDtypeStruct((b, s, GROUP_WIDTH), BF16),
        scratch_shapes=[pltpu.VMEM((PAIR + GQA_EXTRA_ROWS, s + m), BF16)],
        compiler_params=_cparams("arbitrary", "arbitrary"),
        name="gqa_attn",
    )(q, k, v, kc, vc)


NA_RPB_ROWS = 2 * NA_ROWS


def _na_rpb_pairs(rpb):
    h = rpb.shape[0]
    t = jnp.pad(rpb.astype(F32), ((0, 0), (1, 1), (0, HEAD_DIM - rpb.shape[2])))
    pairs = jnp.concatenate([t[:, :-1], t[:, 1:]], axis=-1)
    assert pairs.shape == (h, NA_RPB_ROWS, PAIR)
    return jnp.roll(pairs, -(NA_COLS - 1), axis=-1)


def _na_build_bias(rpb_ref, bias_sc, kind, g, rows):
    w = GRID_W
    r0 = NA_QROWS * g
    start_row = jnp.clip(r0 - NA_ROWS // 2, 0, rows - NA_KROWS)
    qcol = lax.broadcasted_iota(jnp.int32, (w, PAIR), 0)
    lane = lax.broadcasted_iota(jnp.int32, (w, PAIR), 1)
    kcol = lane % w
    odd = (lane >= w).astype(jnp.int32)
    cs = jnp.clip(qcol - NA_COLS // 2, 0, w - NA_COLS)
    col_ok = (kcol >= cs) & (kcol < cs + NA_COLS)

    def body(t, carry):
        rr = t // (NA_KROWS // 2)
        a2 = t % (NA_KROWS // 2)
        r = r0 + rr
        rs = jnp.clip(r - NA_ROWS // 2, 0, rows - NA_ROWS)
        krow0 = start_row + 2 * a2
        krow = krow0 + odd
        ok = col_ok & (krow >= rs) & (krow < rs + NA_ROWS)
        e = jnp.clip(krow0 - r + NA_ROWS, 0, NA_RPB_ROWS - 1)
        for h in range(4):
            tile = jnp.broadcast_to(rpb_ref[h, pl.ds(e, 1), :], (w, PAIR))
            for bit in range(6):
                tile = jnp.where(((qcol >> bit) & 1) == 1, pltpu.roll(tile, 1 << bit, 1), tile)
            row0 = pl.multiple_of((h % 2) * NA_QROWS * w + rr * w, w)
            bias_sc[kind, h // 2, a2, pl.ds(row0, w), :] = jnp.where(ok, tile, NEG)
        return carry

    lax.fori_loop(0, NA_QROWS * (NA_KROWS // 2), body, 0)


def _na_kernel(q_ref, k_ref, v_ref, kc_ref, vc_ref, rpb_ref, o_ref, bias_sc, *, rows, ngrp):
    groups = rows // NA_QROWS
    tq = NA_QROWS * GRID_W
    span = NA_KROWS * GRID_W
    gs = [pl.program_id(1) * ngrp + j for j in range(ngrp)]
    kinds = [jnp.where(g == 0, 0, jnp.where(g == groups - 1, 2, 1)) for g in gs]
    for g, kind in zip(gs, kinds):
        @pl.when((pl.program_id(0) == 0) & ((g == 0) | (g == 1) | (g == groups - 1)))
        def _():
            _na_build_bias(rpb_ref, bias_sc, kind, g, rows)

    for j, (g, kind) in enumerate(zip(gs, kinds)):
        qrows = slice(j * tq, (j + 1) * tq)
        start = pl.multiple_of(jnp.clip(NA_QROWS * g - NA_ROWS // 2, 0, rows - NA_KROWS) * GRID_W, GRID_W)
        for p in range(2):
            lanes = slice(p * PAIR, (p + 1) * PAIR)
            qs = _stack_heads([q_ref[0, qrows, lanes]])
            kl = k_ref[0, pl.ds(start, span), lanes]
            vl = v_ref[0, pl.ds(start, span), lanes]
            bias = jnp.concatenate([bias_sc[kind, p, a2] for a2 in range(NA_KROWS // 2)], axis=1)
            s_nb = _dot_nt(qs, kl) + bias
            s_ctx = _dot_nt(qs, kc_ref[0, :, lanes])
            m = jnp.maximum(jnp.max(s_nb, axis=-1, keepdims=True), jnp.max(s_ctx, axis=-1, keepdims=True))
            p_nb = jnp.exp(s_nb - m)
            p_ctx = jnp.exp(s_ctx - m)
            denom = jnp.sum(p_nb, axis=-1, keepdims=True) + jnp.sum(p_ctx, axis=-1, keepdims=True)
            o = _dot(p_nb.astype(BF16), vl) + _dot(p_ctx.astype(BF16), vc_ref[0, :, lanes])
            o = o * (1.0 / denom)
            o_ref[0, qrows, lanes] = _unstack_heads(o, 1, tq)[0].astype(o_ref.dtype)


def _na(q, k, v, kc, vc, rpb_pairs, *, ngrp=4):
    b, s, _ = q.shape
    m = kc.shape[1]
    rows = s // GRID_W
    groups = rows // NA_QROWS
    tq = NA_QROWS * GRID_W
    assert rows % NA_QROWS == 0 and groups >= 4 and groups % ngrp == 0
    return pl.pallas_call(
        functools.partial(_na_kernel, rows=rows, ngrp=ngrp),
        grid=(b, groups // ngrp),
        in_specs=[
            pl.BlockSpec((1, ngrp * tq, GROUP_WIDTH), lambda i, g: (i, g, 0)),
            pl.BlockSpec((1, s, GROUP_WIDTH), lambda i, g: (i, 0, 0)),
            pl.BlockSpec((1, s, GROUP_WIDTH), lambda i, g: (i, 0, 0)),
            pl.BlockSpec((1, m, GROUP_WIDTH), lambda i, g: (i, 0, 0)),
            pl.BlockSpec((1, m, GROUP_WIDTH), lambda i, g: (i, 0, 0)),
            pl.BlockSpec((4, NA_RPB_ROWS, PAIR), lambda i, g: (0, 0, 0)),
        ],
        out_specs=pl.BlockSpec((1, ngrp * tq, GROUP_WIDTH), lambda i, g: (i, g, 0)),
        out_shape=jax.ShapeDtypeStruct((b, s, GROUP_WIDTH), BF16),
        scratch_shapes=[pltpu.VMEM((3, 2, NA_KROWS // 2, 2 * tq, PAIR), F32)],
        compiler_params=_cparams("arbitrary", "arbitrary"),
        name="na_attn",
    )(q, k, v, kc, vc, rpb_pairs)


def _ssd_kernel(uf_ref, ub_ref, dtf_ref, dtb_ref, dttf_ref, dttb_ref, dtbias_ref, alog_ref,
                dtbias_col_ref, alog_col_ref, dskip_ref, h0f_ref, h0b_ref,
                yf_ref, yb_ref, hf_ref, hb_ref, hf_sc, hb_sc, *, steps, group):
    i = pl.program_id(1)
    q = SSD_CHUNK

    @pl.when(i == 0)
    def _():
        hf_sc[...] = h0f_ref[0]
        hb_sc[...] = h0b_ref[0]

    ii = lax.broadcasted_iota(jnp.int32, (q, q), 0)
    jj = lax.broadcasted_iota(jnp.int32, (q, q), 1)
    lo_lanes = jj < HEAD_DIM
    a_coef = -jnp.exp(alog_ref[...])
    a_coef_col = -jnp.exp(alog_col_ref[...])

    def chunk(u, dt_raw, dt_raw_t, col0, reverse, h):
        causal = (jj >= ii) if reverse else (jj <= ii)
        dt = _softplus(dt_raw + dtbias_ref[...])
        dt_t = _softplus(dt_raw_t + dtbias_col_ref[...])
        tri = jnp.where(causal, 1.0, 0.0).astype(BF16)
        tri_t = jnp.where((ii >= jj) if reverse else (ii <= jj), 1.0, 0.0).astype(BF16)
        a_hi, a_mid, a_lo = _split3(dt * a_coef)
        cum = _dot(tri, a_hi) + _dot(tri, a_mid) + _dot(tri, a_lo)
        a_hi, a_mid, a_lo = _split3(dt_t * a_coef_col)
        cum_t = _dot(a_hi, tri_t) + _dot(a_mid, tri_t) + _dot(a_lo, tri_t)
        end = cum[0:1, :] if reverse else cum[q - 1:q, :]
        xs = u[:, :2 * PAIR]
        ys, h_out = [], []
        for k in range(2):
            bk = u[:, 2 * PAIR + k * SSD_STATE: 2 * PAIR + (k + 1) * SSD_STATE]
            ck = u[:, 2 * PAIR + 2 * SSD_STATE + k * SSD_STATE: 2 * PAIR + 2 * SSD_STATE + (k + 1) * SSD_STATE]
            xk_b = xs[:, k * PAIR:(k + 1) * PAIR]
            xk = xk_b.astype(F32)
            ck_b = ck
            cb = _dot_nt(ck_b, bk)
            c0 = col0 + 2 * k
            cols = []
            for r in range(2):
                c = c0 + r
                seg = cum[:, c:c + 1] - cum_t[c:c + 1, :]
                lmat = jnp.exp(jnp.where(causal, seg, NEG))
                att = (cb * lmat * dt_t[c:c + 1, :]).astype(BF16)
                cols.append(_dot(att, xk_b))
            y_intra = jnp.where(lo_lanes, cols[0], cols[1])
            e_in = jnp.where(lo_lanes, jnp.exp(cum[:, c0:c0 + 1]), jnp.exp(cum[:, c0 + 1:c0 + 2]))
            h_t = h[k]
            y_state = _dot(ck_b, h_t.astype(BF16)) * e_in
            w0 = jnp.exp(end[:, c0:c0 + 1] - cum[:, c0:c0 + 1]) * dt[:, c0:c0 + 1]
            w1 = jnp.exp(end[:, c0 + 1:c0 + 2] - cum[:, c0 + 1:c0 + 2]) * dt[:, c0 + 1:c0 + 2]
            xw = (xk * jnp.where(lo_lanes, w0, w1)).astype(BF16)
            st = _dot(bk.astype(F32).T.astype(BF16), xw)
            decay = jnp.where(lo_lanes[0:1], jnp.exp(end[:, c0:c0 + 1]), jnp.exp(end[:, c0 + 1:c0 + 2]))
            h_out.append(h_t * decay + st)
            ys.append(y_intra + y_state)
        return ys, xs, h_out

    dskip = dskip_ref[...]
    h = [hf_sc[0], hf_sc[1]]
    for j in range(group):
        rows = slice(j * q, (j + 1) * q)
        ys, xs, h = chunk(uf_ref[0, rows, :], dtf_ref[0, rows, :], dttf_ref[0, :, rows], 0, False, h)
        for k in range(2):
            lanes = slice(k * PAIR, (k + 1) * PAIR)
            yf_ref[0, rows, lanes] = ys[k] + dskip[:, lanes] * xs[:, lanes].astype(F32)
    hf_sc[0], hf_sc[1] = h

    h = [hb_sc[0], hb_sc[1]]
    for j in reversed(range(group)):
        rows = slice(j * q, (j + 1) * q)
        ys, _, h = chunk(ub_ref[0, rows, :], dtb_ref[0, rows, :], dttb_ref[0, :, rows], 4, True, h)
        for k in range(2):
            yb_ref[0, rows, k * PAIR:(k + 1) * PAIR] = ys[k]
    hb_sc[0], hb_sc[1] = h

    @pl.when(i == steps - 1)
    def _():
        hf_ref[0] = hf_sc[...]
        hb_ref[0] = hb_sc[...]


def _ssd(u, dt, dt_bias, a_log, dskip, h0f, h0b, *, group=8):
    b, length, ch = u.shape
    nc = length // SSD_CHUNK
    while nc % group:
        group //= 2
    steps = nc // group
    q = group * SSD_CHUNK
    dt_t = jnp.swapaxes(dt[:, :, :DT_ROWS], 1, 2)
    column = lambda v: jnp.broadcast_to(v[0, :DT_ROWS, None], (DT_ROWS, DT_PAD))
    fwd = lambda i, c: (i, c, 0)
    bwd = lambda i, c: (i, steps - 1 - c, 0)
    fwd_t = lambda i, c: (i, 0, c)
    bwd_t = lambda i, c: (i, 0, steps - 1 - c)
    const = lambda i, c: (0, 0)
    state = lambda i, c: (i, 0, 0, 0)
    state_shape = (b, 2, SSD_STATE, PAIR)
    return pl.pallas_call(
        functools.partial(_ssd_kernel, steps=steps, group=group),
        grid=(b, steps),
        in_specs=[
            pl.BlockSpec((1, q, ch), fwd),
            pl.BlockSpec((1, q, ch), bwd),
            pl.BlockSpec((1, q, DT_PAD), fwd),
            pl.BlockSpec((1, q, DT_PAD), bwd),
            pl.BlockSpec((1, DT_ROWS, q), fwd_t),
            pl.BlockSpec((1, DT_ROWS, q), bwd_t),
            pl.BlockSpec((1, DT_PAD), const),
            pl.BlockSpec((1, DT_PAD), const),
            pl.BlockSpec((DT_ROWS, DT_PAD), const),
            pl.BlockSpec((DT_ROWS, DT_PAD), const),
            pl.BlockSpec((1, GROUP_WIDTH), const),
            pl.BlockSpec((1, 2, SSD_STATE, PAIR), state),
            pl.BlockSpec((1, 2, SSD_STATE, PAIR), state),
        ],
        out_specs=[
            pl.BlockSpec((1, q, GROUP_WIDTH), fwd),
            pl.BlockSpec((1, q, GROUP_WIDTH), bwd),
            pl.BlockSpec((1, 2, SSD_STATE, PAIR), state),
            pl.BlockSpec((1, 2, SSD_STATE, PAIR), state),
        ],
        out_shape=[
            jax.ShapeDtypeStruct((b, length, GROUP_WIDTH), F32),
            jax.ShapeDtypeStruct((b, length, GROUP_WIDTH), F32),
            jax.ShapeDtypeStruct(state_shape, F32),
            jax.ShapeDtypeStruct(state_shape, F32),
        ],
        scratch_shapes=[pltpu.VMEM((2, SSD_STATE, PAIR), F32), pltpu.VMEM((2, SSD_STATE, PAIR), F32)],
        compiler_params=_cparams("arbitrary", "arbitrary"),
        name="ssd_scan",
    )(u, u, dt, dt, dt_t, dt_t, dt_bias, a_log, column(dt_bias), column(a_log), dskip, h0f, h0b)


def _head_blocks(start, perm):
    return [(start + h * HEAD_DIM, start + (h + 1) * HEAD_DIM) for h in perm]


def _take_blocks(a, axis, blocks):
    return jnp.concatenate([lax.slice_in_dim(a, lo, hi, axis=axis) for lo, hi in blocks], axis=axis)


_INPROJ_BLOCKS = (_head_blocks(0, HEAD_PERM) + [(256, 1536)] + _head_blocks(1544, HEAD_PERM)
                  + [(1800, 2824), (1536, 1544)])
_OUTPROJ_BLOCKS = (_head_blocks(0, HEAD_PERM) + [(256, 512)] + _head_blocks(512, HEAD_PERM)
                   + [(768, 1024)])


def _rope_tables(seq):
    t = np.arange(seq)
    quarter = HEAD_DIM // 4
    inv = ROPE_BASE ** (-jnp.arange(quarter, dtype=F32) / quarter)
    a_row = jnp.asarray(t // GRID_W, F32)[:, None] * inv
    a_col = jnp.asarray(t % GRID_W, F32)[:, None] * inv
    cos = jnp.concatenate([jnp.cos(a_row), jnp.cos(a_row), jnp.cos(a_col), jnp.cos(a_col)], axis=1)
    sin = jnp.concatenate([-jnp.sin(a_row), jnp.sin(a_row), -jnp.sin(a_col), jnp.sin(a_col)], axis=1)
    return jnp.tile(cos, (1, 2)), jnp.tile(sin, (1, 2))


def _pad_lanes(v, width):
    v = v.reshape(1, -1).astype(F32)
    return jnp.pad(v, ((0, 0), (0, width - v.shape[1])))


def _token_tile(n, cap):
    t = cap
    while n % t:
        t //= 2
    return t


def kernel(x, c, ctx, c_ctx, ada_w, ada_b, ln_g, ln_b, ffn1_w_in, ffn1_w_out, mix_w_in, mix_w_out,
           mix_norm_g, swa_sink, ssd_conv_w, ssd_conv_b, ssd_dt_bias, ssd_A_log, ssd_D,
           gqa_q_norm, gqa_k_norm, na_rpb, ffn2_w_in, ffn2_w_out):
    bsz, seq, d = x.shape
    m_ctx = ctx.shape[1]
    depth = ada_w.shape[0]
    alpha = float((2 * depth) ** 0.25)
    mod_rows = 16 * ((bsz + 1 + 15) // 16)
    cc = jnp.concatenate([c, c_ctx[None], jnp.zeros((mod_rows - bsz - 1, d), F32)], axis=0)
    mods = _ada(cc, ada_w, ada_b)

    cos_x, sin_x = _rope_tables(seq)
    tm_x = _token_tile(seq, 512)
    tm_c = _token_tile(m_ctx, 256)
    cos_c = jnp.ones((tm_c, PAIR), F32)
    sin_c = jnp.zeros((tm_c, PAIR), F32)
    blockdiag = jnp.asarray(np.kron(np.eye(2), np.ones((HEAD_DIM, HEAD_DIM))), BF16)
    w1_in, w1_out = ffn1_w_in.astype(BF16), ffn1_w_out.astype(BF16)
    w2_in, w2_out = ffn2_w_in.astype(BF16), ffn2_w_out.astype(BF16)
    w_mix = _take_blocks(mix_w_in, 2, _INPROJ_BLOCKS).astype(BF16)
    w_mix = jnp.pad(w_mix, ((0, 0), (0, 0), (0, _P_END - w_mix.shape[2])))
    w_o = _take_blocks(mix_w_out, 1, _OUTPROJ_BLOCKS).astype(BF16)
    norm_g_all = _take_blocks(mix_norm_g, 1, _OUTPROJ_BLOCKS)

    xt = x.reshape(bsz * seq, d)
    ct = ctx.reshape(bsz * m_ctx, d)
    zero_state = jnp.zeros((bsz, 2, SSD_STATE, PAIR), F32)

    for l in range(depth):
        last = l == depth - 1
        mod_x = mods[l, :bsz].reshape(bsz, N_MOD, d)
        mod_c = mods[l, bsz:bsz + 1].reshape(1, N_MOD, d)
        norm_g = norm_g_all[l].reshape(1, d)
        qg = jnp.tile(gqa_q_norm[l], 2).reshape(1, PAIR)
        kg = jnp.tile(gqa_k_norm[l], 2).reshape(1, PAIR)
        conv_w = jnp.pad(ssd_conv_w[l].reshape(SSD_CONV, SSD_CONV_CH), ((0, 8 - SSD_CONV), (0, 0)))
        conv_b = ssd_conv_b[l].reshape(1, SSD_CONV_CH)
        dt_bias = _pad_lanes(ssd_dt_bias[l], DT_PAD)
        a_log = _pad_lanes(ssd_A_log[l], DT_PAD)
        dskip = jnp.repeat(ssd_D[l], HEAD_DIM).reshape(1, GROUP_WIDTH)
        sink = swa_sink[l].astype(F32)
        rpb_pairs = _na_rpb_pairs(na_rpb[l])

        ffn_x = functools.partial(_ffn, tokens_per_mod=seq, alpha=alpha, tm=tm_x)
        ffn_c = functools.partial(_ffn, tokens_per_mod=bsz * m_ctx, alpha=alpha, tm=tm_c)

        xt = ffn_x(xt, mod_x, (0, 1, 2), w1_in, w1_out, l, ln_g[l, 0], ln_b[l, 0])
        ct = ffn_c(ct, mod_c, (0, 1, 2), w1_in, w1_out, l, ln_g[l, 0], ln_b[l, 0])

        px = _inproj(xt, mod_x, w_mix, l, cos_x, sin_x, qg, kg, blockdiag, conv_w, conv_b,
                     tokens_per_mod=seq, seq_len=seq, tm=tm_x, gqa_q_scale=Q_SCALE * LOG2E)
        pc = _inproj(ct, mod_c, w_mix, l, cos_c, sin_c, qg, kg, blockdiag, conv_w, conv_b,
                     tokens_per_mod=bsz * m_ctx, seq_len=m_ctx, tm=tm_c, gqa_q_scale=Q_SCALE)
        aq, ak, av, bz, bu, bdt, cq, ck, cv, dq, dk, dv = [
            t.reshape(bsz, seq, t.shape[-1]) for t in px]
        aq_c, ak_c, av_c, bz_c, bu_c, bdt_c, cq_c, ck_c, cv_c, dq_c, dk_c, dv_c = [
            t.reshape(bsz, m_ctx, t.shape[-1]) for t in pc]

        ssd = functools.partial(_ssd, dt_bias=dt_bias, a_log=a_log, dskip=dskip)
        yf_c, yb_c, hf_c, hb_c = ssd(bu_c, bdt_c, h0f=zero_state, h0b=zero_state)
        yf, yb, _, _ = ssd(bu, bdt, h0f=hf_c, h0b=hb_c)

        oa = _swa(sink, aq, ak, av, ak_c, av_c)
        oc = _gqa(cq, ck, cv, ck_c, cv_c)
        od = _na(dq, dk, dv, dk_c, dv_c, rpb_pairs)

        flat = lambda t: t.reshape(-1, t.shape[-1])
        xt = _mixer_ffn(xt, mod_x, flat(oa), flat(yf), flat(yb), flat(bz), flat(oc), flat(od), norm_g, w_o,
                        ln_g[l, 1], ln_b[l, 1], w2_in, w2_out, l, ln_g[l, 2], ln_b[l, 2],
                        tokens_per_mod=seq, alpha=alpha, tm=tm_x)

        if not last:
            tq_c = _token_tile(m_ctx, 128)
            oa_c = _dense_attn(aq_c, [(ak_c, av_c)], kv_pairs=1, sink=sink, tq=tq_c)
            oc_c = _dense_attn(cq_c, [(ck_c, cv_c)], kv_pairs=1, tq=tq_c)
            od_c = _dense_attn(dq_c, [(dk_c, dv_c)], kv_pairs=2, tq=tq_c)
            ct = _mixer_ffn(ct, mod_c, flat(oa_c), flat(yf_c), flat(yb_c), flat(bz_c), flat(oc_c),
                            flat(od_c), norm_g, w_o, ln_g[l, 1], ln_b[l, 1], w2_in, w2_out, l,
                            ln_g[l, 2], ln_b[l, 2], tokens_per_mod=bsz * m_ctx, alpha=alpha, tm=tm_c)

    return xt.reshape(bsz, seq, d)
```

```python
import functools

import numpy as np
import jax
import jax.numpy as jnp
from jax import lax
from jax.experimental import pallas as pl
from jax.experimental.pallas import tpu as pltpu

F32 = jnp.float32
BF16 = jnp.bfloat16

HEAD_DIM = 64
PAIR = 2 * HEAD_DIM
GROUP_WIDTH = 256
GRID_W = 64
N_MOD = 9
SWA_WINDOW = 128
SWA_BLOCK = 128
SSD_CHUNK = 128
SSD_STATE = 128
SSD_CONV = 5
CONV_HALO = 8
FFN_SUBTILE = 256
SSD_CONV_CH = 768
NA_ROWS = 8
NA_COLS = 16
NA_QROWS = 4
NA_KROWS = 12
ROPE_BASE = 10000.0
EPS = 1e-5
NEG = -1e30
Q_SCALE = HEAD_DIM ** -0.5
LOG2E = 1.4426950408889634
DT_PAD = 128
DT_ROWS = 16
HEAD_PERM = (0, 2, 1, 3)

VMEM_LIMIT_BYTES = 56 * 1024 * 1024


def _cparams(*sem):
    return pltpu.CompilerParams(dimension_semantics=sem, vmem_limit_bytes=VMEM_LIMIT_BYTES)


def _dot(a, b):
    return jnp.dot(a, b, preferred_element_type=F32)


def _dot_nt(a, b):
    return lax.dot_general(a, b, (((1,), (1,)), ((), ())), preferred_element_type=F32)


def _sigmoid(x):
    return 1.0 / (1.0 + jnp.exp(-x))


def _silu(x):
    return x * _sigmoid(x)


def _softplus(x):
    return jnp.maximum(x, 0.0) + jnp.log(1.0 + jnp.exp(-jnp.abs(x)))


def _split3(a):
    hi = a.astype(BF16)
    r1 = a - hi.astype(F32)
    mid = r1.astype(BF16)
    lo = (r1 - mid.astype(F32)).astype(BF16)
    return hi, mid, lo


def _layer_norm(y, g, b):
    mu = jnp.mean(y, axis=-1, keepdims=True)
    d = y - mu
    var = jnp.mean(d * d, axis=-1, keepdims=True)
    return d * lax.rsqrt(var + EPS) * g + b


def _ada_kernel(c_ref, w_ref, b_ref, o_ref):
    s = _silu(c_ref[...])
    s_hi = s.astype(BF16)
    s_lo = (s - s_hi.astype(F32)).astype(BF16)
    w = w_ref[0]
    w_hi = w.astype(BF16)
    w_lo = (w - w_hi.astype(F32)).astype(BF16)
    o_ref[0] = _dot(s_hi, w_hi) + _dot(s_lo, w_hi) + _dot(s_hi, w_lo) + b_ref[0]


def _ada(cc, ada_w, ada_b):
    depth, d, n = ada_w.shape
    rows = cc.shape[0]
    tn = 1024
    return pl.pallas_call(
        _ada_kernel,
        grid=(depth, n // tn),
        in_specs=[
            pl.BlockSpec((rows, d), lambda l, j: (0, 0)),
            pl.BlockSpec((1, d, tn), lambda l, j: (l, 0, j)),
            pl.BlockSpec((1, 1, tn), lambda l, j: (l, 0, j)),
        ],
        out_specs=pl.BlockSpec((1, rows, tn), lambda l, j: (l, 0, j)),
        out_shape=jax.ShapeDtypeStruct((depth, rows, n), F32),
        compiler_params=_cparams("arbitrary", "arbitrary"),
        name="ada_mod",
    )(cc, ada_w, ada_b.reshape(depth, 1, n))


def _ffn_rows(x, mod_ref, rows, wa_ref, wu_ref, wo_ref, g, b, alpha):
    r_shift, r_scale, r_gate = rows
    h = x * (1.0 + mod_ref[0, r_scale:r_scale + 1, :]) + mod_ref[0, r_shift:r_shift + 1, :]
    h = h.astype(BF16)
    a = _dot(h, wa_ref[...])
    u = _dot(h, wu_ref[...])
    gated = (_silu(a) * u).astype(BF16)
    f = _dot(gated, wo_ref[...])
    y = alpha * x + (0.5 * mod_ref[0, r_gate:r_gate + 1, :]) * f
    return _layer_norm(y, g, b)


def _mixer_rows(x, mod_ref, parts, ng_ref, w_ref, g, b, alpha):
    acc = None
    for gi, y in enumerate(parts):
        rows = slice(gi * GROUP_WIDTH, (gi + 1) * GROUP_WIDTH)
        ms = jnp.mean(y * y, axis=-1, keepdims=True)
        yn = (y * lax.rsqrt(ms + EPS) * ng_ref[:, rows]).astype(BF16)
        term = _dot(yn, w_ref[rows, :])
        acc = term if acc is None else acc + term
    return _layer_norm(alpha * x + mod_ref[0, 5:6, :] * acc, g, b)


def _subtiles(tm):
    sub = FFN_SUBTILE if tm % FFN_SUBTILE == 0 else tm
    return [slice(r0, r0 + sub) for r0 in range(0, tm, sub)]


def _ffn_kernel(x_ref, mod_ref, wa_ref, wu_ref, wo_ref, g_ref, b_ref, o_ref, *, rows, alpha):
    for r in _subtiles(x_ref.shape[0]):
        o_ref[r, :] = _ffn_rows(x_ref[r, :], mod_ref, rows, wa_ref, wu_ref, wo_ref,
                                g_ref[...], b_ref[...], alpha)


def _mixer_ffn_kernel(x_ref, mod_ref, oa_ref, yf_ref, yb_ref, z_ref, oc_ref, od_ref, ng_ref, wm_ref,
                      g1_ref, b1_ref, wa_ref, wu_ref, wo_ref, g2_ref, b2_ref, o_ref, *, alpha):
    for r in _subtiles(x_ref.shape[0]):
        ob = (yf_ref[r, :] + yb_ref[r, :]) * _silu(z_ref[r, :].astype(F32))
        parts = (oa_ref[r, :].astype(F32), ob, oc_ref[r, :].astype(F32), od_ref[r, :].astype(F32))
        x_mid = _mixer_rows(x_ref[r, :], mod_ref, parts, ng_ref, wm_ref, g1_ref[...], b1_ref[...], alpha)
        o_ref[r, :] = _ffn_rows(x_mid, mod_ref, (6, 7, 8), wa_ref, wu_ref, wo_ref,
                                g2_ref[...], b2_ref[...], alpha)


def _layer_weight(shape, layer, col_block=0):
    return pl.BlockSpec((None,) + shape, lambda i: (layer, 0, col_block), pipeline_mode=pl.Buffered(1))


def _ffn(xt, mod, rows, w_in, w_out, layer, ln_g, ln_b, *, tokens_per_mod, alpha, tm):
    t, d = xt.shape
    dff = w_out.shape[1]
    tiles_per_mod = tokens_per_mod // tm
    return pl.pallas_call(
        functools.partial(_ffn_kernel, rows=rows, alpha=alpha),
        grid=(t // tm,),
        in_specs=[
            pl.BlockSpec((tm, d), lambda i: (i, 0)),
            pl.BlockSpec((1, N_MOD, d), lambda i: (i // tiles_per_mod, 0, 0)),
            _layer_weight((d, dff), layer, 0),
            _layer_weight((d, dff), layer, 1),
            _layer_weight((dff, d), layer),
            pl.BlockSpec((1, d), lambda i: (0, 0)),
            pl.BlockSpec((1, d), lambda i: (0, 0)),
        ],
        out_specs=pl.BlockSpec((tm, d), lambda i: (i, 0)),
        out_shape=jax.ShapeDtypeStruct((t, d), F32),
        compiler_params=_cparams("arbitrary"),
        name="ffn",
    )(xt, mod, w_in, w_in, w_out, ln_g.reshape(1, d), ln_b.reshape(1, d))


def _mixer_ffn(xt, mod, oa, yf, yb, z, oc, od, norm_g, w_mix_out, ln1_g, ln1_b, w_in, w_out, layer,
               ln2_g, ln2_b, *, tokens_per_mod, alpha, tm):
    t, d = xt.shape
    dff = w_out.shape[1]
    tiles_per_mod = tokens_per_mod // tm
    row = lambda i: (i, 0)
    const = lambda i: (0, 0)
    grp = pl.BlockSpec((tm, GROUP_WIDTH), row)
    vec = pl.BlockSpec((1, d), const)
    return pl.pallas_call(
        functools.partial(_mixer_ffn_kernel, alpha=alpha),
        grid=(t // tm,),
        in_specs=[
            pl.BlockSpec((tm, d), row),
            pl.BlockSpec((1, N_MOD, d), lambda i: (i // tiles_per_mod, 0, 0)),
            grp, grp, grp, grp, grp, grp,
            vec,
            _layer_weight((d, d), layer),
            vec, vec,
            _layer_weight((d, dff), layer, 0),
            _layer_weight((d, dff), layer, 1),
            _layer_weight((dff, d), layer),
            vec, vec,
        ],
        out_specs=pl.BlockSpec((tm, d), row),
        out_shape=jax.ShapeDtypeStruct((t, d), F32),
        compiler_params=_cparams("arbitrary"),
        name="mixer_ffn",
    )(xt, mod, oa, yf, yb, z, oc, od, norm_g, w_mix_out, ln1_g.reshape(1, d), ln1_b.reshape(1, d),
      w_in, w_in, w_out, ln2_g.reshape(1, d), ln2_b.reshape(1, d))


_P_AQ, _P_AK, _P_AV, _P_Z, _P_XBC = 0, 256, 384, 512, 768
_P_CQ, _P_CK, _P_CV, _P_DQ, _P_DK, _P_DV, _P_DT, _P_END = 1536, 1792, 1920, 2048, 2304, 2560, 2816, 2944


def _rope_pair(t, cos, sin):
    lane = lax.broadcasted_iota(jnp.int32, t.shape, 1)
    up = pltpu.roll(t, HEAD_DIM // 4, 1)
    dn = pltpu.roll(t, PAIR - HEAD_DIM // 4, 1)
    partner = jnp.where((lane % 32) < 16, dn, up)
    return t * cos + partner * sin


def _rms_pair(t, g, blockdiag):
    sq = t * t
    hi = sq.astype(BF16)
    lo = (sq - hi.astype(F32)).astype(BF16)
    ss = _dot(hi, blockdiag) + _dot(lo, blockdiag)
    return t * lax.rsqrt(ss * (1.0 / HEAD_DIM) + EPS) * g


def _inproj_kernel(x_ref, xp_ref, xn_ref, mod_ref, w_ref, cos_ref, sin_ref, qg_ref, kg_ref, bd_ref,
                   cw_ref, cb_ref,
                   aq_ref, ak_ref, av_ref, z_ref, u_ref, dt_ref,
                   cq_ref, ck_ref, cv_ref, dq_ref, dk_ref, dv_ref, ext_sc, *, gqa_q_scale, tiles_per_seq):
    tm = x_ref.shape[0]
    halo = xp_ref.shape[0]
    pos = pl.program_id(0) % tiles_per_seq
    x_ext = jnp.concatenate([xp_ref[...], x_ref[...], xn_ref[...]], axis=0)
    h = (x_ext * (1.0 + mod_ref[0, 4:5, :]) + mod_ref[0, 3:4, :]).astype(BF16)
    y_ext = _dot(h, w_ref[...])
    y = y_ext[halo:halo + tm]
    cos = cos_ref[...]
    sin = sin_ref[...]
    bd = bd_ref[...]

    def pairs(lo, n):
        return [y[:, lo + PAIR * p: lo + PAIR * (p + 1)] for p in range(n)]

    xbc = y_ext[:, _P_XBC:_P_XBC + SSD_CONV_CH]
    ext_sc[:halo] = jnp.where(pos == 0, 0.0, xbc[:halo])
    ext_sc[halo:halo + tm] = xbc[halo:halo + tm]
    ext_sc[halo + tm:] = jnp.where(pos == tiles_per_seq - 1, 0.0, xbc[halo + tm:])
    conv = cb_ref[...]
    for k in range(SSD_CONV):
        lo = halo - SSD_CONV // 2 + k
        conv = conv + cw_ref[k:k + 1, :] * ext_sc[lo:lo + tm, :]
    u_ref[...] = _silu(conv).astype(u_ref.dtype)

    def put(ref, parts, scale=None):
        for p, part in enumerate(parts):
            if scale is not None:
                part = part * scale
            ref[:, PAIR * p: PAIR * (p + 1)] = part.astype(ref.dtype)

    swa = pairs(_P_AQ, 4)
    put(aq_ref, [_rope_pair(t, cos, sin) for t in swa[:2]], Q_SCALE)
    put(ak_ref, [_rope_pair(swa[2], cos, sin)])
    put(av_ref, [swa[3]])
    put(z_ref, pairs(_P_Z, 2))
    put(dt_ref, pairs(_P_DT, 1))
    gqa = pairs(_P_CQ, 4)
    put(cq_ref, [_rope_pair(_rms_pair(t, qg_ref[...], bd), cos, sin) for t in gqa[:2]], gqa_q_scale)
    put(ck_ref, [_rope_pair(_rms_pair(gqa[2], kg_ref[...], bd), cos, sin)])
    put(cv_ref, [gqa[3]])
    na = pairs(_P_DQ, 6)
    put(dq_ref, na[:2], Q_SCALE)
    put(dk_ref, na[2:4])
    put(dv_ref, na[4:])


def _inproj(xt, mod, w, layer, cos, sin, qg, kg, blockdiag, conv_w, conv_b, *, tokens_per_mod, seq_len, tm,
            gqa_q_scale):
    t, d = xt.shape
    tiles_per_mod = tokens_per_mod // tm
    table_tiles = cos.shape[0] // tm
    widths = (256, 128, 128, 256, SSD_CONV_CH, DT_PAD, 256, 128, 128, 256, 256, 256)
    dtypes = (BF16,) * 5 + (F32,) + (BF16,) * 6
    per = tm // CONV_HALO
    last_halo = t // CONV_HALO - 1
    row = lambda i: (i, 0)
    const = lambda i: (0, 0)
    return pl.pallas_call(
        functools.partial(_inproj_kernel, gqa_q_scale=gqa_q_scale, tiles_per_seq=seq_len // tm),
        grid=(t // tm,),
        in_specs=[
            pl.BlockSpec((tm, d), row),
            pl.BlockSpec((CONV_HALO, d), lambda i: (jnp.maximum(i * per - 1, 0), 0)),
            pl.BlockSpec((CONV_HALO, d), lambda i: (jnp.minimum((i + 1) * per, last_halo), 0)),
            pl.BlockSpec((1, N_MOD, d), lambda i: (i // tiles_per_mod, 0, 0)),
            _layer_weight((d, _P_END), layer),
            pl.BlockSpec((tm, PAIR), lambda i: (i % table_tiles, 0)),
            pl.BlockSpec((tm, PAIR), lambda i: (i % table_tiles, 0)),
            pl.BlockSpec((1, PAIR), const),
            pl.BlockSpec((1, PAIR), const),
            pl.BlockSpec((PAIR, PAIR), const),
            pl.BlockSpec((8, SSD_CONV_CH), const),
            pl.BlockSpec((1, SSD_CONV_CH), const),
        ],
        out_specs=[pl.BlockSpec((tm, wd), row) for wd in widths],
        out_shape=[jax.ShapeDtypeStruct((t, wd), dt) for wd, dt in zip(widths, dtypes)],
        scratch_shapes=[pltpu.VMEM((tm + 2 * CONV_HALO, SSD_CONV_CH), F32)],
        compiler_params=_cparams("arbitrary"),
        name="inproj",
    )(xt, xt, xt, mod, w, cos, sin, qg, kg, blockdiag, conv_w, conv_b)


def _stack_heads(q_pairs):
    lane = lax.broadcasted_iota(jnp.int32, q_pairs[0].shape, 1)
    lo = lane < HEAD_DIM
    zero = jnp.zeros_like(q_pairs[0])
    blocks = []
    for qp in q_pairs:
        blocks.append(jnp.where(lo, qp, zero))
        blocks.append(jnp.where(lo, zero, qp))
    return jnp.concatenate(blocks, axis=0)


def _unstack_heads(o, n_pairs, tq):
    lane = lax.broadcasted_iota(jnp.int32, (tq, PAIR), 1)
    lo = lane < HEAD_DIM
    return [jnp.where(lo, o[2 * p * tq:(2 * p + 1) * tq], o[(2 * p + 1) * tq:(2 * p + 2) * tq])
            for p in range(n_pairs)]


def _sink_column(sink_ref, tq):
    blk = lax.broadcasted_iota(jnp.int32, (4 * tq, 1), 0) // tq
    col = jnp.full((4 * tq, 1), sink_ref[HEAD_PERM[3]], F32)
    for b in (2, 1, 0):
        col = jnp.where(blk == b, sink_ref[HEAD_PERM[b]], col)
    return col


def _swa_kernel(sink_ref, q_ref, k_ref, v_ref, kc_ref, vc_ref, o_ref, *, seq, nblk):
    blk = SWA_BLOCK
    span = 3 * blk
    sk = _sink_column(sink_ref, blk)
    for j in range(nblk):
        n = pl.program_id(1) * nblk + j
        rows = slice(j * blk, (j + 1) * blk)
        start = pl.multiple_of(jnp.clip((n - 1) * blk, 0, seq - span), blk)
        kl = k_ref[0, pl.ds(start, span), :]
        vl = v_ref[0, pl.ds(start, span), :]
        q = q_ref[0, rows, :]
        qs = _stack_heads([q[:, :PAIR], q[:, PAIR:]])
        s_loc = _dot_nt(qs, kl)
        s_ctx = _dot_nt(qs, kc_ref[0])
        qpos = n * blk + lax.broadcasted_iota(jnp.int32, s_loc.shape, 0) % blk
        kpos = start + lax.broadcasted_iota(jnp.int32, s_loc.shape, 1)
        s_loc = jnp.where(jnp.abs(kpos - qpos) <= SWA_WINDOW, s_loc, NEG)
        m = jnp.maximum(jnp.maximum(jnp.max(s_loc, axis=-1, keepdims=True),
                                    jnp.max(s_ctx, axis=-1, keepdims=True)), sk)
        p_loc = jnp.exp(s_loc - m)
        p_ctx = jnp.exp(s_ctx - m)
        denom = (jnp.sum(p_loc, axis=-1, keepdims=True) + jnp.sum(p_ctx, axis=-1, keepdims=True)
                 + jnp.exp(sk - m))
        o = _dot(p_loc.astype(BF16), vl) + _dot(p_ctx.astype(BF16), vc_ref[0])
        o = o * (1.0 / denom)
        o_a, o_b = _unstack_heads(o, 2, blk)
        o_ref[0, rows, :PAIR] = o_a.astype(o_ref.dtype)
        o_ref[0, rows, PAIR:] = o_b.astype(o_ref.dtype)


def _swa(sink, q, k, v, kc, vc, *, nblk=8):
    b, s, _ = q.shape
    m = kc.shape[1]
    blk = SWA_BLOCK * nblk
    assert s % blk == 0 and s >= 3 * SWA_BLOCK
    return pl.pallas_call(
        functools.partial(_swa_kernel, seq=s, nblk=nblk),
        grid=(b, s // blk),
        in_specs=[
            pl.BlockSpec(memory_space=pltpu.SMEM),
            pl.BlockSpec((1, blk, GROUP_WIDTH), lambda i, n: (i, n, 0)),
            pl.BlockSpec((1, s, PAIR), lambda i, n: (i, 0, 0)),
            pl.BlockSpec((1, s, PAIR), lambda i, n: (i, 0, 0)),
            pl.BlockSpec((1, m, PAIR), lambda i, n: (i, 0, 0)),
            pl.BlockSpec((1, m, PAIR), lambda i, n: (i, 0, 0)),
        ],
        out_specs=pl.BlockSpec((1, blk, GROUP_WIDTH), lambda i, n: (i, n, 0)),
        out_shape=jax.ShapeDtypeStruct((b, s, GROUP_WIDTH), BF16),
        compiler_params=_cparams("arbitrary", "arbitrary"),
        name="swa_attn",
    )(sink, q, k, v, kc, vc)


def _dense_attn_kernel(*refs, kv_pairs, n_src, has_sink, tq, tk):
    refs = list(refs)
    sink_ref = refs.pop(0) if has_sink else None
    q_ref = refs.pop(0)
    srcs = [(refs[2 * i], refs[2 * i + 1]) for i in range(n_src)]
    o_ref = refs[2 * n_src]
    q = q_ref[0]
    if kv_pairs == 1:
        units = [([q[:, :PAIR], q[:, PAIR:]], 0)]
    else:
        units = [([q[:, :PAIR]], 0), ([q[:, PAIR:]], 1)]
    outs = []
    for q_pairs, kv in units:
        qs = _stack_heads(q_pairs)
        nrow = qs.shape[0]
        if has_sink:
            m = _sink_column(sink_ref, tq)
            l = jnp.ones((nrow, 1), F32)
        else:
            m = jnp.full((nrow, 1), NEG, F32)
            l = jnp.zeros((nrow, 1), F32)
        acc = jnp.zeros((nrow, PAIR), F32)
        for k_ref, v_ref in srcs:
            nk = k_ref.shape[1]
            step = min(tk, nk)
            for c in range(nk // step):
                kch = k_ref[0, c * step:(c + 1) * step, kv * PAIR:(kv + 1) * PAIR]
                vch = v_ref[0, c * step:(c + 1) * step, kv * PAIR:(kv + 1) * PAIR]
                s = _dot_nt(qs, kch)
                m_new = jnp.maximum(m, jnp.max(s, axis=-1, keepdims=True))
                a = jnp.exp(m - m_new)
                p = jnp.exp(s - m_new)
                l = a * l + jnp.sum(p, axis=-1, keepdims=True)
                acc = a * acc + _dot(p.astype(BF16), vch)
                m = m_new
        o = acc * (1.0 / l)
        outs += _unstack_heads(o, len(q_pairs), tq)
    o_ref[0, :, :PAIR] = outs[0].astype(o_ref.dtype)
    o_ref[0, :, PAIR:] = outs[1].astype(o_ref.dtype)


def _dense_attn(q, srcs, *, kv_pairs, sink=None, tq=128, tk=512):
    b, s, _ = q.shape
    kvw = kv_pairs * PAIR
    has_sink = sink is not None
    in_specs, args = [], []
    if has_sink:
        in_specs.append(pl.BlockSpec(memory_space=pltpu.SMEM))
        args.append(sink)
    in_specs.append(pl.BlockSpec((1, tq, GROUP_WIDTH), lambda i, n: (i, n, 0)))
    args.append(q)
    for k, v in srcs:
        nk = k.shape[1]
        assert nk % min(tk, nk) == 0
        in_specs += [pl.BlockSpec((1, nk, kvw), lambda i, n: (i, 0, 0))] * 2
        args += [k, v]
    return pl.pallas_call(
        functools.partial(_dense_attn_kernel, kv_pairs=kv_pairs, n_src=len(srcs),
                          has_sink=has_sink, tq=tq, tk=tk),
        grid=(b, s // tq),
        in_specs=in_specs,
        out_specs=pl.BlockSpec((1, tq, GROUP_WIDTH), lambda i, n: (i, n, 0)),
        out_shape=jax.ShapeDtypeStruct((b, s, GROUP_WIDTH), BF16),
        compiler_params=_cparams("arbitrary", "arbitrary"),
        name="dense_attn",
    )(*args)


GQA_EXTRA_ROWS = 16


def _gqa_kernel(q_ref, k_ref, v_ref, kc_ref, vc_ref, o_ref, vt_sc, *, tq, tk):
    n = pl.program_id(1)
    seq = k_ref.shape[1]
    m_ctx = kc_ref.shape[1]

    @pl.when(n == 0)
    def _():
        for c in range(seq // PAIR):
            rows = slice(c * PAIR, (c + 1) * PAIR)
            vt_sc[:PAIR, rows] = v_ref[0, rows, :].astype(F32).T.astype(BF16)
        for c in range(m_ctx // PAIR):
            rows = slice(c * PAIR, (c + 1) * PAIR)
            vt_sc[:PAIR, seq + c * PAIR: seq + (c + 1) * PAIR] = vc_ref[0, rows, :].astype(F32).T.astype(BF16)
        ones_row = lax.broadcasted_iota(jnp.int32, (GQA_EXTRA_ROWS, seq + m_ctx), 0) == 0
        vt_sc[PAIR:, :] = jnp.where(ones_row, 1.0, 0.0).astype(BF16)

    q = q_ref[0]
    qs = _stack_heads([q[:, :PAIR], q[:, PAIR:]])
    chunks = [(k_ref, c * tk, tk, c * tk) for c in range(seq // tk)]
    chunks += [(kc_ref, c * min(tk, m_ctx), min(tk, m_ctx), seq + c * min(tk, m_ctx))
               for c in range(m_ctx // min(tk, m_ctx))]
    nrow = 4 * tq
    m = jnp.full((1, nrow), NEG, F32)
    acc = jnp.zeros((PAIR + GQA_EXTRA_ROWS, nrow), F32)
    for ref, lo, size, col in chunks:
        s_t = _dot_nt(ref[0, lo:lo + size, :], qs)
        m_new = jnp.maximum(m, jnp.max(s_t, axis=0, keepdims=True))
        p_t = jnp.exp2((s_t - m_new).astype(BF16))
        acc = jnp.exp2(m - m_new) * acc + _dot(vt_sc[:, col:col + size], p_t)
        m = m_new
    o_t = acc[:PAIR] * (1.0 / acc[PAIR:PAIR + 1])
    lo_rows = lax.broadcasted_iota(jnp.int32, (PAIR, tq), 0) < HEAD_DIM
    for p in range(2):
        pair_t = jnp.where(lo_rows, o_t[:, 2 * p * tq:(2 * p + 1) * tq],
                           o_t[:, (2 * p + 1) * tq:(2 * p + 2) * tq])
        o_ref[0, :, p * PAIR:(p + 1) * PAIR] = pair_t.T.astype(o_ref.dtype)


def _gqa(q, k, v, kc, vc, *, tq=1024, tk=1024):
    b, s, _ = q.shape
    m = kc.shape[1]
    tq = _token_tile(s, tq)
    assert s % tk == 0 and s % PAIR == 0 and m % PAIR == 0 and m % min(tk, m) == 0
    full = lambda i, n: (i, 0, 0)
    return pl.pallas_call(
        functools.partial(_gqa_kernel, tq=tq, tk=tk),
        grid=(b, s // tq),
        in_specs=[
            pl.BlockSpec((1, tq, GROUP_WIDTH), lambda i, n: (i, n, 0)),
            pl.BlockSpec((1, s, PAIR), full),
            pl.BlockSpec((1, s, PAIR), full),
            pl.BlockSpec((1, m, PAIR), full),
            pl.BlockSpec((1, m, PAIR), full),
        ],
        out_specs=pl.BlockSpec((1, tq, GROUP_WIDTH), lambda i, n: (i, n, 0)),
        out_shape=jax.ShapeDtypeStruct((b, s, GROUP_WIDTH), BF16),
        scratch_shapes=[pltpu.VMEM((PAIR + GQA_EXTRA_ROWS, s + m), BF16)],
        compiler_params=_cparams("arbitrary", "arbitrary"),
        name="gqa_attn",
    )(q, k, v, kc, vc)


NA_RPB_ROWS = 2 * NA_ROWS


def _na_rpb_pairs(rpb):
    h = rpb.shape[0]
    t = jnp.pad(rpb.astype(F32), ((0, 0), (1, 1), (0, HEAD_DIM - rpb.shape[2])))
    pairs = jnp.concatenate([t[:, :-1], t[:, 1:]], axis=-1)
    assert pairs.shape == (h, NA_RPB_ROWS, PAIR)
    return jnp.roll(pairs, -(NA_COLS - 1), axis=-1)


def _na_build_bias(rpb_ref, bias_sc, kind, g, rows):
    w = GRID_W
    r0 = NA_QROWS * g
    start_row = jnp.clip(r0 - NA_ROWS // 2, 0, rows - NA_KROWS)
    qcol = lax.broadcasted_iota(jnp.int32, (w, PAIR), 0)
    lane = lax.broadcasted_iota(jnp.int32, (w, PAIR), 1)
    kcol = lane % w
    odd = (lane >= w).astype(jnp.int32)
    cs = jnp.clip(qcol - NA_COLS // 2, 0, w - NA_COLS)
    col_ok = (kcol >= cs) & (kcol < cs + NA_COLS)

    def body(t, carry):
        rr = t // (NA_KROWS // 2)
        a2 = t % (NA_KROWS // 2)
        r = r0 + rr
        rs = jnp.clip(r - NA_ROWS // 2, 0, rows - NA_ROWS)
        krow0 = start_row + 2 * a2
        krow = krow0 + odd
        ok = col_ok & (krow >= rs) & (krow < rs + NA_ROWS)
        e = jnp.clip(krow0 - r + NA_ROWS, 0, NA_RPB_ROWS - 1)
        for h in range(4):
            tile = jnp.broadcast_to(rpb_ref[h, pl.ds(e, 1), :], (w, PAIR))
            for bit in range(6):
                tile = jnp.where(((qcol >> bit) & 1) == 1, pltpu.roll(tile, 1 << bit, 1), tile)
            row0 = pl.multiple_of((h % 2) * NA_QROWS * w + rr * w, w)
            bias_sc[kind, h // 2, a2, pl.ds(row0, w), :] = jnp.where(ok, tile, NEG)
        return carry

    lax.fori_loop(0, NA_QROWS * (NA_KROWS // 2), body, 0)


def _na_kernel(q_ref, k_ref, v_ref, kc_ref, vc_ref, rpb_ref, o_ref, bias_sc, *, rows, ngrp):
    groups = rows // NA_QROWS
    tq = NA_QROWS * GRID_W
    span = NA_KROWS * GRID_W
    gs = [pl.program_id(1) * ngrp + j for j in range(ngrp)]
    kinds = [jnp.where(g == 0, 0, jnp.where(g == groups - 1, 2, 1)) for g in gs]
    for g, kind in zip(gs, kinds):
        @pl.when((pl.program_id(0) == 0) & ((g == 0) | (g == 1) | (g == groups - 1)))
        def _():
            _na_build_bias(rpb_ref, bias_sc, kind, g, rows)

    for j, (g, kind) in enumerate(zip(gs, kinds)):
        qrows = slice(j * tq, (j + 1) * tq)
        start = pl.multiple_of(jnp.clip(NA_QROWS * g - NA_ROWS // 2, 0, rows - NA_KROWS) * GRID_W, GRID_W)
        for p in range(2):
            lanes = slice(p * PAIR, (p + 1) * PAIR)
            qs = _stack_heads([q_ref[0, qrows, lanes]])
            kl = k_ref[0, pl.ds(start, span), lanes]
            vl = v_ref[0, pl.ds(start, span), lanes]
            bias = jnp.concatenate([bias_sc[kind, p, a2] for a2 in range(NA_KROWS // 2)], axis=1)
            s_nb = _dot_nt(qs, kl) + bias
            s_ctx = _dot_nt(qs, kc_ref[0, :, lanes])
            m = jnp.maximum(jnp.max(s_nb, axis=-1, keepdims=True), jnp.max(s_ctx, axis=-1, keepdims=True))
            p_nb = jnp.exp(s_nb - m)
            p_ctx = jnp.exp(s_ctx - m)
            denom = jnp.sum(p_nb, axis=-1, keepdims=True) + jnp.sum(p_ctx, axis=-1, keepdims=True)
            o = _dot(p_nb.astype(BF16), vl) + _dot(p_ctx.astype(BF16), vc_ref[0, :, lanes])
            o = o * (1.0 / denom)
            o_ref[0, qrows, lanes] = _unstack_heads(o, 1, tq)[0].astype(o_ref.dtype)


def _na(q, k, v, kc, vc, rpb_pairs, *, ngrp=4):
    b, s, _ = q.shape
    m = kc.shape[1]
    rows = s // GRID_W
    groups = rows // NA_QROWS
    tq = NA_QROWS * GRID_W
    assert rows % NA_QROWS == 0 and groups >= 4 and groups % ngrp == 0
    return pl.pallas_call(
        functools.partial(_na_kernel, rows=rows, ngrp=ngrp),
        grid=(b, groups // ngrp),
        in_specs=[
            pl.BlockSpec((1, ngrp * tq, GROUP_WIDTH), lambda i, g: (i, g, 0)),
            pl.BlockSpec((1, s, GROUP_WIDTH), lambda i, g: (i, 0, 0)),
            pl.BlockSpec((1, s, GROUP_WIDTH), lambda i, g: (i, 0, 0)),
            pl.BlockSpec((1, m, GROUP_WIDTH), lambda i, g: (i, 0, 0)),
            pl.BlockSpec((1, m, GROUP_WIDTH), lambda i, g: (i, 0, 0)),
            pl.BlockSpec((4, NA_RPB_ROWS, PAIR), lambda i, g: (0, 0, 0)),
        ],
        out_specs=pl.BlockSpec((1, ngrp * tq, GROUP_WIDTH), lambda i, g: (i, g, 0)),
        out_shape=jax.ShapeDtypeStruct((b, s, GROUP_WIDTH), BF16),
        scratch_shapes=[pltpu.VMEM((3, 2, NA_KROWS // 2, 2 * tq, PAIR), F32)],
        compiler_params=_cparams("arbitrary", "arbitrary"),
        name="na_attn",
    )(q, k, v, kc, vc, rpb_pairs)


def _ssd_kernel(uf_ref, ub_ref, dtf_ref, dtb_ref, dttf_ref, dttb_ref, dtbias_ref, alog_ref,
                dtbias_col_ref, alog_col_ref, dskip_ref, h0f_ref, h0b_ref,
                yf_ref, yb_ref, hf_ref, hb_ref, hf_sc, hb_sc, *, steps, group):
    i = pl.program_id(1)
    q = SSD_CHUNK

    @pl.when(i == 0)
    def _():
        hf_sc[...] = h0f_ref[0]
        hb_sc[...] = h0b_ref[0]

    ii = lax.broadcasted_iota(jnp.int32, (q, q), 0)
    jj = lax.broadcasted_iota(jnp.int32, (q, q), 1)
    lo_lanes = jj < HEAD_DIM
    a_coef = -jnp.exp(alog_ref[...])
    a_coef_col = -jnp.exp(alog_col_ref[...])

    def chunk(u, dt_raw, dt_raw_t, col0, reverse, h):
        causal = (jj >= ii) if reverse else (jj <= ii)
        dt = _softplus(dt_raw + dtbias_ref[...])
        dt_t = _softplus(dt_raw_t + dtbias_col_ref[...])
        tri = jnp.where(causal, 1.0, 0.0).astype(BF16)
        tri_t = jnp.where((ii >= jj) if reverse else (ii <= jj), 1.0, 0.0).astype(BF16)
        a_hi, a_mid, a_lo = _split3(dt * a_coef)
        cum = _dot(tri, a_hi) + _dot(tri, a_mid) + _dot(tri, a_lo)
        a_hi, a_mid, a_lo = _split3(dt_t * a_coef_col)
        cum_t = _dot(a_hi, tri_t) + _dot(a_mid, tri_t) + _dot(a_lo, tri_t)
        end = cum[0:1, :] if reverse else cum[q - 1:q, :]
        xs = u[:, :2 * PAIR]
        ys, h_out = [], []
        for k in range(2):
            bk = u[:, 2 * PAIR + k * SSD_STATE: 2 * PAIR + (k + 1) * SSD_STATE]
            ck = u[:, 2 * PAIR + 2 * SSD_STATE + k * SSD_STATE: 2 * PAIR + 2 * SSD_STATE + (k + 1) * SSD_STATE]
            xk_b = xs[:, k * PAIR:(k + 1) * PAIR]
            xk = xk_b.astype(F32)
            ck_b = ck
            cb = _dot_nt(ck_b, bk)
            c0 = col0 + 2 * k
            cols = []
            for r in range(2):
                c = c0 + r
                seg = cum[:, c:c + 1] - cum_t[c:c + 1, :]
                lmat = jnp.exp(jnp.where(causal, seg, NEG))
                att = (cb * lmat * dt_t[c:c + 1, :]).astype(BF16)
                cols.append(_dot(att, xk_b))
            y_intra = jnp.where(lo_lanes, cols[0], cols[1])
            e_in = jnp.where(lo_lanes, jnp.exp(cum[:, c0:c0 + 1]), jnp.exp(cum[:, c0 + 1:c0 + 2]))
            h_t = h[k]
            y_state = _dot(ck_b, h_t.astype(BF16)) * e_in
            w0 = jnp.exp(end[:, c0:c0 + 1] - cum[:, c0:c0 + 1]) * dt[:, c0:c0 + 1]
            w1 = jnp.exp(end[:, c0 + 1:c0 + 2] - cum[:, c0 + 1:c0 + 2]) * dt[:, c0 + 1:c0 + 2]
            xw = (xk * jnp.where(lo_lanes, w0, w1)).astype(BF16)
            st = _dot(bk.astype(F32).T.astype(BF16), xw)
            decay = jnp.where(lo_lanes[0:1], jnp.exp(end[:, c0:c0 + 1]), jnp.exp(end[:, c0 + 1:c0 + 2]))
            h_out.append(h_t * decay + st)
            ys.append(y_intra + y_state)
        return ys, xs, h_out

    dskip = dskip_ref[...]
    h = [hf_sc[0], hf_sc[1]]
    for j in range(group):
        rows = slice(j * q, (j + 1) * q)
        ys, xs, h = chunk(uf_ref[0, rows, :], dtf_ref[0, rows, :], dttf_ref[0, :, rows], 0, False, h)
        for k in range(2):
            lanes = slice(k * PAIR, (k + 1) * PAIR)
            yf_ref[0, rows, lanes] = ys[k] + dskip[:, lanes] * xs[:, lanes].astype(F32)
    hf_sc[0], hf_sc[1] = h

    h = [hb_sc[0], hb_sc[1]]
    for j in reversed(range(group)):
        rows = slice(j * q, (j + 1) * q)
        ys, _, h = chunk(ub_ref[0, rows, :], dtb_ref[0, rows, :], dttb_ref[0, :, rows], 4, True, h)
        for k in range(2):
            yb_ref[0, rows, k * PAIR:(k + 1) * PAIR] = ys[k]
    hb_sc[0], hb_sc[1] = h

    @pl.when(i == steps - 1)
    def _():
        hf_ref[0] = hf_sc[...]
        hb_ref[0] = hb_sc[...]


def _ssd(u, dt, dt_bias, a_log, dskip, h0f, h0b, *, group=8):
    b, length, ch = u.shape
    nc = length // SSD_CHUNK
    while nc % group:
        group //= 2
    steps = nc // group
    q = group * SSD_CHUNK
    dt_t = jnp.swapaxes(dt[:, :, :DT_ROWS], 1, 2)
    column = lambda v: jnp.broadcast_to(v[0, :DT_ROWS, None], (DT_ROWS, DT_PAD))
    fwd = lambda i, c: (i, c, 0)
    bwd = lambda i, c: (i, steps - 1 - c, 0)
    fwd_t = lambda i, c: (i, 0, c)
    bwd_t = lambda i, c: (i, 0, steps - 1 - c)
    const = lambda i, c: (0, 0)
    state = lambda i, c: (i, 0, 0, 0)
    state_shape = (b, 2, SSD_STATE, PAIR)
    return pl.pallas_call(
        functools.partial(_ssd_kernel, steps=steps, group=group),
        grid=(b, steps),
        in_specs=[
            pl.BlockSpec((1, q, ch), fwd),
            pl.BlockSpec((1, q, ch), bwd),
            pl.BlockSpec((1, q, DT_PAD), fwd),
            pl.BlockSpec((1, q, DT_PAD), bwd),
            pl.BlockSpec((1, DT_ROWS, q), fwd_t),
            pl.BlockSpec((1, DT_ROWS, q), bwd_t),
            pl.BlockSpec((1, DT_PAD), const),
            pl.BlockSpec((1, DT_PAD), const),
            pl.BlockSpec((DT_ROWS, DT_PAD), const),
            pl.BlockSpec((DT_ROWS, DT_PAD), const),
            pl.BlockSpec((1, GROUP_WIDTH), const),
            pl.BlockSpec((1, 2, SSD_STATE, PAIR), state),
            pl.BlockSpec((1, 2, SSD_STATE, PAIR), state),
        ],
        out_specs=[
            pl.BlockSpec((1, q, GROUP_WIDTH), fwd),
            pl.BlockSpec((1, q, GROUP_WIDTH), bwd),
            pl.BlockSpec((1, 2, SSD_STATE, PAIR), state),
            pl.BlockSpec((1, 2, SSD_STATE, PAIR), state),
        ],
        out_shape=[
            jax.ShapeDtypeStruct((b, length, GROUP_WIDTH), F32),
            jax.ShapeDtypeStruct((b, length, GROUP_WIDTH), F32),
            jax.ShapeDtypeStruct(state_shape, F32),
            jax.ShapeDtypeStruct(state_shape, F32),
        ],
        scratch_shapes=[pltpu.VMEM((2, SSD_STATE, PAIR), F32), pltpu.VMEM((2, SSD_STATE, PAIR), F32)],
        compiler_params=_cparams("arbitrary", "arbitrary"),
        name="ssd_scan",
    )(u, u, dt, dt, dt_t, dt_t, dt_bias, a_log, column(dt_bias), column(a_log), dskip, h0f, h0b)


def _head_blocks(start, perm):
    return [(start + h * HEAD_DIM, start + (h + 1) * HEAD_DIM) for h in perm]


def _take_blocks(a, axis, blocks):
    return jnp.concatenate([lax.slice_in_dim(a, lo, hi, axis=axis) for lo, hi in blocks], axis=axis)


_INPROJ_BLOCKS = (_head_blocks(0, HEAD_PERM) + [(256, 1536)] + _head_blocks(1544, HEAD_PERM)
                  + [(1800, 2824), (1536, 1544)])
_OUTPROJ_BLOCKS = (_head_blocks(0, HEAD_PERM) + [(256, 512)] + _head_blocks(512, HEAD_PERM)
                   + [(768, 1024)])


def _rope_tables(seq):
    t = np.arange(seq)
    quarter = HEAD_DIM // 4
    inv = ROPE_BASE ** (-jnp.arange(quarter, dtype=F32) / quarter)
    a_row = jnp.asarray(t // GRID_W, F32)[:, None] * inv
    a_col = jnp.asarray(t % GRID_W, F32)[:, None] * inv
    cos = jnp.concatenate([jnp.cos(a_row), jnp.cos(a_row), jnp.cos(a_col), jnp.cos(a_col)], axis=1)
    sin = jnp.concatenate([-jnp.sin(a_row), jnp.sin(a_row), -jnp.sin(a_col), jnp.sin(a_col)], axis=1)
    return jnp.tile(cos, (1, 2)), jnp.tile(sin, (1, 2))


def _pad_lanes(v, width):
    v = v.reshape(1, -1).astype(F32)
    return jnp.pad(v, ((0, 0), (0, width - v.shape[1])))


def _token_tile(n, cap):
    t = cap
    while n % t:
        t //= 2
    return t


def kernel(x, c, ctx, c_ctx, ada_w, ada_b, ln_g, ln_b, ffn1_w_in, ffn1_w_out, mix_w_in, mix_w_out,
           mix_norm_g, swa_sink, ssd_conv_w, ssd_conv_b, ssd_dt_bias, ssd_A_log, ssd_D,
           gqa_q_norm, gqa_k_norm, na_rpb, ffn2_w_in, ffn2_w_out):
    bsz, seq, d = x.shape
    m_ctx = ctx.shape[1]
    depth = ada_w.shape[0]
    alpha = float((2 * depth) ** 0.25)
    mod_rows = 16 * ((bsz + 1 + 15) // 16)
    cc = jnp.concatenate([c, c_ctx[None], jnp.zeros((mod_rows - bsz - 1, d), F32)], axis=0)
    mods = _ada(cc, ada_w, ada_b)

    cos_x, sin_x = _rope_tables(seq)
    tm_x = _token_tile(seq, 1024)
    tm_p = _token_tile(seq, 1024)
    tm_c = _token_tile(m_ctx, 256)
    cos_c = jnp.ones((tm_c, PAIR), F32)
    sin_c = jnp.zeros((tm_c, PAIR), F32)
    blockdiag = jnp.asarray(np.kron(np.eye(2), np.ones((HEAD_DIM, HEAD_DIM))), BF16)
    w1_in, w1_out = ffn1_w_in.astype(BF16), ffn1_w_out.astype(BF16)
    w2_in, w2_out = ffn2_w_in.astype(BF16), ffn2_w_out.astype(BF16)
    w_mix = _take_blocks(mix_w_in, 2, _INPROJ_BLOCKS).astype(BF16)
    w_mix = jnp.pad(w_mix, ((0, 0), (0, 0), (0, _P_END - w_mix.shape[2])))
    w_o = _take_blocks(mix_w_out, 1, _OUTPROJ_BLOCKS).astype(BF16)
    norm_g_all = _take_blocks(mix_norm_g, 1, _OUTPROJ_BLOCKS)

    xt = x.reshape(bsz * seq, d)
    ct = ctx.reshape(bsz * m_ctx, d)
    zero_state = jnp.zeros((bsz, 2, SSD_STATE, PAIR), F32)

    for l in range(depth):
        last = l == depth - 1
        mod_x = mods[l, :bsz].reshape(bsz, N_MOD, d)
        mod_c = mods[l, bsz:bsz + 1].reshape(1, N_MOD, d)
        norm_g = norm_g_all[l].reshape(1, d)
        qg = jnp.tile(gqa_q_norm[l], 2).reshape(1, PAIR)
        kg = jnp.tile(gqa_k_norm[l], 2).reshape(1, PAIR)
        conv_w = jnp.pad(ssd_conv_w[l].reshape(SSD_CONV, SSD_CONV_CH), ((0, 8 - SSD_CONV), (0, 0)))
        conv_b = ssd_conv_b[l].reshape(1, SSD_CONV_CH)
        dt_bias = _pad_lanes(ssd_dt_bias[l], DT_PAD)
        a_log = _pad_lanes(ssd_A_log[l], DT_PAD)
        dskip = jnp.repeat(ssd_D[l], HEAD_DIM).reshape(1, GROUP_WIDTH)
        sink = swa_sink[l].astype(F32)
        rpb_pairs = _na_rpb_pairs(na_rpb[l])

        ffn_x = functools.partial(_ffn, tokens_per_mod=seq, alpha=alpha, tm=tm_x)
        ffn_c = functools.partial(_ffn, tokens_per_mod=bsz * m_ctx, alpha=alpha, tm=tm_c)

        xt = ffn_x(xt, mod_x, (0, 1, 2), w1_in, w1_out, l, ln_g[l, 0], ln_b[l, 0])
        ct = ffn_c(ct, mod_c, (0, 1, 2), w1_in, w1_out, l, ln_g[l, 0], ln_b[l, 0])

        px = _inproj(xt, mod_x, w_mix, l, cos_x, sin_x, qg, kg, blockdiag, conv_w, conv_b,
                     tokens_per_mod=seq, seq_len=seq, tm=tm_p, gqa_q_scale=Q_SCALE * LOG2E)
        pc = _inproj(ct, mod_c, w_mix, l, cos_c, sin_c, qg, kg, blockdiag, conv_w, conv_b,
                     tokens_per_mod=bsz * m_ctx, seq_len=m_ctx, tm=tm_c, gqa_q_scale=Q_SCALE)
        aq, ak, av, bz, bu, bdt, cq, ck, cv, dq, dk, dv = [
            t.reshape(bsz, seq, t.shape[-1]) for t in px]
        aq_c, ak_c, av_c, bz_c, bu_c, bdt_c, cq_c, ck_c, cv_c, dq_c, dk_c, dv_c = [
            t.reshape(bsz, m_ctx, t.shape[-1]) for t in pc]

        ssd = functools.partial(_ssd, dt_bias=dt_bias, a_log=a_log, dskip=dskip)
        yf_c, yb_c, hf_c, hb_c = ssd(bu_c, bdt_c, h0f=zero_state, h0b=zero_state)
        yf, yb, _, _ = ssd(bu, bdt, h0f=hf_c, h0b=hb_c)

        oa = _swa(sink, aq, ak, av, ak_c, av_c)
        oc = _gqa(cq, ck, cv, ck_c, cv_c)
        od = _na(dq, dk, dv, dk_c, dv_c, rpb_pairs)

        flat = lambda t: t.reshape(-1, t.shape[-1])
        xt = _mixer_ffn(xt, mod_x, flat(oa), flat(yf), flat(yb), flat(bz), flat(oc), flat(od), norm_g, w_o,
                        ln_g[l, 1], ln_b[l, 1], w2_in, w2_out, l, ln_g[l, 2], ln_b[l, 2],
                        tokens_per_mod=seq, alpha=alpha, tm=tm_x)

        if not last:
            tq_c = _token_tile(m_ctx, 128)
            oa_c = _dense_attn(aq_c, [(ak_c, av_c)], kv_pairs=1, sink=sink, tq=tq_c)
            oc_c = _dense_attn(cq_c, [(ck_c, cv_c)], kv_pairs=1, tq=tq_c)
            od_c = _dense_attn(dq_c, [(dk_c, dv_c)], kv_pairs=2, tq=tq_c)
            ct = _mixer_ffn(ct, mod_c, flat(oa_c), flat(yf_c), flat(yb_c), flat(bz_c), flat(oc_c),
                            flat(od_c), norm_g, w_o, ln_g[l, 1], ln_b[l, 1], w2_in, w2_out, l,
                            ln_g[l, 2], ln_b[l, 2], tokens_per_mod=bsz * m_ctx, alpha=alpha, tm=tm_c)

    return xt.reshape(bsz, seq, d)
```

```python
import functools

import numpy as np
import jax
import jax.numpy as jnp
from jax import lax
from jax.experimental import pallas as pl
from jax.experimental.pallas import tpu as pltpu

F32 = jnp.float32
BF16 = jnp.bfloat16

HEAD_DIM = 64
PAIR = 2 * HEAD_DIM
GROUP_WIDTH = 256
GRID_W = 64
N_MOD = 9
SWA_WINDOW = 128
SWA_BLOCK = 128
SSD_CHUNK = 128
SSD_STATE = 128
SSD_CONV = 5
CONV_HALO = 8
FFN_SUBTILE = 256
SSD_CONV_CH = 768
NA_ROWS = 8
NA_COLS = 16
NA_QROWS = 4
NA_KROWS = 12
ROPE_BASE = 10000.0
EPS = 1e-5
NEG = -1e30
Q_SCALE = HEAD_DIM ** -0.5
LOG2E = 1.4426950408889634
DT_PAD = 128
DT_ROWS = 16
HEAD_PERM = (0, 2, 1, 3)

VMEM_LIMIT_BYTES = 56 * 1024 * 1024


def _cparams(*sem):
    return pltpu.CompilerParams(dimension_semantics=sem, vmem_limit_bytes=VMEM_LIMIT_BYTES)


def _dot(a, b):
    return jnp.dot(a, b, preferred_element_type=F32)


def _dot_nt(a, b):
    return lax.dot_general(a, b, (((1,), (1,)), ((), ())), preferred_element_type=F32)


def _sigmoid(x):
    return 1.0 / (1.0 + jnp.exp(-x))


def _silu(x):
    return x * _sigmoid(x)


def _softplus(x):
    return jnp.maximum(x, 0.0) + jnp.log(1.0 + jnp.exp(-jnp.abs(x)))


def _split3(a):
    hi = a.astype(BF16)
    r1 = a - hi.astype(F32)
    mid = r1.astype(BF16)
    lo = (r1 - mid.astype(F32)).astype(BF16)
    return hi, mid, lo


def _layer_norm(y, g, b):
    mu = jnp.mean(y, axis=-1, keepdims=True)
    d = y - mu
    var = jnp.mean(d * d, axis=-1, keepdims=True)
    return d * lax.rsqrt(var + EPS) * g + b


def _ada_kernel(c_ref, w_ref, b_ref, o_ref):
    s = _silu(c_ref[...])
    s_hi = s.astype(BF16)
    s_lo = (s - s_hi.astype(F32)).astype(BF16)
    w = w_ref[0]
    w_hi = w.astype(BF16)
    w_lo = (w - w_hi.astype(F32)).astype(BF16)
    o_ref[0] = _dot(s_hi, w_hi) + _dot(s_lo, w_hi) + _dot(s_hi, w_lo) + b_ref[0]


def _ada(cc, ada_w, ada_b):
    depth, d, n = ada_w.shape
    rows = cc.shape[0]
    tn = 1024
    return pl.pallas_call(
        _ada_kernel,
        grid=(depth, n // tn),
        in_specs=[
            pl.BlockSpec((rows, d), lambda l, j: (0, 0)),
            pl.BlockSpec((1, d, tn), lambda l, j: (l, 0, j)),
            pl.BlockSpec((1, 1, tn), lambda l, j: (l, 0, j)),
        ],
        out_specs=pl.BlockSpec((1, rows, tn), lambda l, j: (l, 0, j)),
        out_shape=jax.ShapeDtypeStruct((depth, rows, n), F32),
        compiler_params=_cparams("arbitrary", "arbitrary"),
        name="ada_mod",
    )(cc, ada_w, ada_b.reshape(depth, 1, n))


def _ffn_rows(x, mod_ref, rows, wa_ref, wu_ref, wo_ref, g, b, alpha):
    r_shift, r_scale, r_gate = rows
    h = x * (1.0 + mod_ref[0, r_scale:r_scale + 1, :]) + mod_ref[0, r_shift:r_shift + 1, :]
    h = h.astype(BF16)
    a = _dot(h, wa_ref[...])
    u = _dot(h, wu_ref[...])
    gated = (_silu(a) * u).astype(BF16)
    f = _dot(gated, wo_ref[...])
    y = alpha * x + (0.5 * mod_ref[0, r_gate:r_gate + 1, :]) * f
    return _layer_norm(y, g, b)


def _mixer_rows(x, mod_ref, parts, ng_ref, w_ref, g, b, alpha):
    acc = None
    for gi, y in enumerate(parts):
        rows = slice(gi * GROUP_WIDTH, (gi + 1) * GROUP_WIDTH)
        ms = jnp.mean(y * y, axis=-1, keepdims=True)
        yn = (y * lax.rsqrt(ms + EPS) * ng_ref[:, rows]).astype(BF16)
        term = _dot(yn, w_ref[rows, :])
        acc = term if acc is None else acc + term
    return _layer_norm(alpha * x + mod_ref[0, 5:6, :] * acc, g, b)


def _subtiles(tm):
    sub = FFN_SUBTILE if tm % FFN_SUBTILE == 0 else tm
    return [slice(r0, r0 + sub) for r0 in range(0, tm, sub)]


def _ffn_kernel(x_ref, mod_ref, wa_ref, wu_ref, wo_ref, g_ref, b_ref, o_ref, *, rows, alpha):
    for r in _subtiles(x_ref.shape[0]):
        o_ref[r, :] = _ffn_rows(x_ref[r, :], mod_ref, rows, wa_ref, wu_ref, wo_ref,
                                g_ref[...], b_ref[...], alpha)


def _mixer_ffn_kernel(x_ref, mod_ref, oa_ref, yf_ref, yb_ref, z_ref, oc_ref, od_ref, ng_ref, wm_ref,
                      g1_ref, b1_ref, wa_ref, wu_ref, wo_ref, g2_ref, b2_ref, o_ref, *, alpha):
    for r in _subtiles(x_ref.shape[0]):
        ob = (yf_ref[r, :] + yb_ref[r, :]) * _silu(z_ref[r, :].astype(F32))
        parts = (oa_ref[r, :].astype(F32), ob, oc_ref[r, :].astype(F32), od_ref[r, :].astype(F32))
        x_mid = _mixer_rows(x_ref[r, :], mod_ref, parts, ng_ref, wm_ref, g1_ref[...], b1_ref[...], alpha)
        o_ref[r, :] = _ffn_rows(x_mid, mod_ref, (6, 7, 8), wa_ref, wu_ref, wo_ref,
                                g2_ref[...], b2_ref[...], alpha)


def _layer_weight(shape, layer, col_block=0):
    return pl.BlockSpec((None,) + shape, lambda i: (layer, 0, col_block), pipeline_mode=pl.Buffered(1))


def _ffn(xt, mod, rows, w_in, w_out, layer, ln_g, ln_b, *, tokens_per_mod, alpha, tm):
    t, d = xt.shape
    dff = w_out.shape[1]
    tiles_per_mod = tokens_per_mod // tm
    return pl.pallas_call(
        functools.partial(_ffn_kernel, rows=rows, alpha=alpha),
        grid=(t // tm,),
        in_specs=[
            pl.BlockSpec((tm, d), lambda i: (i, 0)),
            pl.BlockSpec((1, N_MOD, d), lambda i: (i // tiles_per_mod, 0, 0)),
            _layer_weight((d, dff), layer, 0),
            _layer_weight((d, dff), layer, 1),
            _layer_weight((dff, d), layer),
            pl.BlockSpec((1, d), lambda i: (0, 0)),
            pl.BlockSpec((1, d), lambda i: (0, 0)),
        ],
        out_specs=pl.BlockSpec((tm, d), lambda i: (i, 0)),
        out_shape=jax.ShapeDtypeStruct((t, d), F32),
        compiler_params=_cparams("arbitrary"),
        name="ffn",
    )(xt, mod, w_in, w_in, w_out, ln_g.reshape(1, d), ln_b.reshape(1, d))


def _mixer_ffn(xt, mod, oa, yf, yb, z, oc, od, norm_g, w_mix_out, ln1_g, ln1_b, w_in, w_out, layer,
               ln2_g, ln2_b, *, tokens_per_mod, alpha, tm):
    t, d = xt.shape
    dff = w_out.shape[1]
    tiles_per_mod = tokens_per_mod // tm
    row = lambda i: (i, 0)
    const = lambda i: (0, 0)
    grp = pl.BlockSpec((tm, GROUP_WIDTH), row)
    vec = pl.BlockSpec((1, d), const)
    return pl.pallas_call(
        functools.partial(_mixer_ffn_kernel, alpha=alpha),
        grid=(t // tm,),
        in_specs=[
            pl.BlockSpec((tm, d), row),
            pl.BlockSpec((1, N_MOD, d), lambda i: (i // tiles_per_mod, 0, 0)),
            grp, grp, grp, grp, grp, grp,
            vec,
            _layer_weight((d, d), layer),
            vec, vec,
            _layer_weight((d, dff), layer, 0),
            _layer_weight((d, dff), layer, 1),
            _layer_weight((dff, d), layer),
            vec, vec,
        ],
        out_specs=pl.BlockSpec((tm, d), row),
        out_shape=jax.ShapeDtypeStruct((t, d), F32),
        compiler_params=_cparams("arbitrary"),
        name="mixer_ffn",
    )(xt, mod, oa, yf, yb, z, oc, od, norm_g, w_mix_out, ln1_g.reshape(1, d), ln1_b.reshape(1, d),
      w_in, w_in, w_out, ln2_g.reshape(1, d), ln2_b.reshape(1, d))


_P_AQ, _P_AK, _P_AV, _P_Z, _P_XBC = 0, 256, 384, 512, 768
_P_CQ, _P_CK, _P_CV, _P_DQ, _P_DK, _P_DV, _P_DT, _P_END = 1536, 1792, 1920, 2048, 2304, 2560, 2816, 2944


def _rope_pair(t, cos, sin):
    lane = lax.broadcasted_iota(jnp.int32, t.shape, 1)
    up = pltpu.roll(t, HEAD_DIM // 4, 1)
    dn = pltpu.roll(t, PAIR - HEAD_DIM // 4, 1)
    partner = jnp.where((lane % 32) < 16, dn, up)
    return t * cos + partner * sin


def _rms_pair(t, g, blockdiag):
    sq = t * t
    hi = sq.astype(BF16)
    lo = (sq - hi.astype(F32)).astype(BF16)
    ss = _dot(hi, blockdiag) + _dot(lo, blockdiag)
    return t * lax.rsqrt(ss * (1.0 / HEAD_DIM) + EPS) * g


def _inproj_kernel(x_ref, xp_ref, xn_ref, mod_ref, w_ref, cos_ref, sin_ref, qg_ref, kg_ref, bd_ref,
                   cw_ref, cb_ref,
                   aq_ref, ak_ref, av_ref, z_ref, u_ref, dt_ref,
                   cq_ref, ck_ref, cv_ref, dq_ref, dk_ref, dv_ref, ext_sc, *, gqa_q_scale, tiles_per_seq):
    tm = x_ref.shape[0]
    halo = xp_ref.shape[0]
    pos = pl.program_id(0) % tiles_per_seq
    x_ext = jnp.concatenate([xp_ref[...], x_ref[...], xn_ref[...]], axis=0)
    h = (x_ext * (1.0 + mod_ref[0, 4:5, :]) + mod_ref[0, 3:4, :]).astype(BF16)
    y_ext = _dot(h, w_ref[...])
    y = y_ext[halo:halo + tm]
    cos = cos_ref[...]
    sin = sin_ref[...]
    bd = bd_ref[...]

    def pairs(lo, n):
        return [y[:, lo + PAIR * p: lo + PAIR * (p + 1)] for p in range(n)]

    xbc = y_ext[:, _P_XBC:_P_XBC + SSD_CONV_CH]
    ext_sc[:halo] = jnp.where(pos == 0, 0.0, xbc[:halo])
    ext_sc[halo:halo + tm] = xbc[halo:halo + tm]
    ext_sc[halo + tm:] = jnp.where(pos == tiles_per_seq - 1, 0.0, xbc[halo + tm:])
    conv = cb_ref[...]
    for k in range(SSD_CONV):
        lo = halo - SSD_CONV // 2 + k
        conv = conv + cw_ref[k:k + 1, :] * ext_sc[lo:lo + tm, :]
    u_ref[...] = _silu(conv).astype(u_ref.dtype)

    def put(ref, parts, scale=None):
        for p, part in enumerate(parts):
            if scale is not None:
                part = part * scale
            ref[:, PAIR * p: PAIR * (p + 1)] = part.astype(ref.dtype)

    swa = pairs(_P_AQ, 4)
    put(aq_ref, [_rope_pair(t, cos, sin) for t in swa[:2]], Q_SCALE)
    put(ak_ref, [_rope_pair(swa[2], cos, sin)])
    put(av_ref, [swa[3]])
    put(z_ref, pairs(_P_Z, 2))
    put(dt_ref, pairs(_P_DT, 1))
    gqa = pairs(_P_CQ, 4)
    put(cq_ref, [_rope_pair(_rms_pair(t, qg_ref[...], bd), cos, sin) for t in gqa[:2]], gqa_q_scale)
    put(ck_ref, [_rope_pair(_rms_pair(gqa[2], kg_ref[...], bd), cos, sin)])
    put(cv_ref, [gqa[3]])
    na = pairs(_P_DQ, 6)
    put(dq_ref, na[:2], Q_SCALE)
    put(dk_ref, na[2:4])
    put(dv_ref, na[4:])


def _inproj(xt, mod, w, layer, cos, sin, qg, kg, blockdiag, conv_w, conv_b, *, tokens_per_mod, seq_len, tm,
            gqa_q_scale):
    t, d = xt.shape
    tiles_per_mod = tokens_per_mod // tm
    table_tiles = cos.shape[0] // tm
    widths = (256, 128, 128, 256, SSD_CONV_CH, DT_PAD, 256, 128, 128, 256, 256, 256)
    dtypes = (BF16,) * 5 + (F32,) + (BF16,) * 6
    per = tm // CONV_HALO
    last_halo = t // CONV_HALO - 1
    row = lambda i: (i, 0)
    const = lambda i: (0, 0)
    return pl.pallas_call(
        functools.partial(_inproj_kernel, gqa_q_scale=gqa_q_scale, tiles_per_seq=seq_len // tm),
        grid=(t // tm,),
        in_specs=[
            pl.BlockSpec((tm, d), row),
            pl.BlockSpec((CONV_HALO, d), lambda i: (jnp.maximum(i * per - 1, 0), 0)),
            pl.BlockSpec((CONV_HALO, d), lambda i: (jnp.minimum((i + 1) * per, last_halo), 0)),
            pl.BlockSpec((1, N_MOD, d), lambda i: (i // tiles_per_mod, 0, 0)),
            _layer_weight((d, _P_END), layer),
            pl.BlockSpec((tm, PAIR), lambda i: (i % table_tiles, 0)),
            pl.BlockSpec((tm, PAIR), lambda i: (i % table_tiles, 0)),
            pl.BlockSpec((1, PAIR), const),
            pl.BlockSpec((1, PAIR), const),
            pl.BlockSpec((PAIR, PAIR), const),
            pl.BlockSpec((8, SSD_CONV_CH), const),
            pl.BlockSpec((1, SSD_CONV_CH), const),
        ],
        out_specs=[pl.BlockSpec((tm, wd), row) for wd in widths],
        out_shape=[jax.ShapeDtypeStruct((t, wd), dt) for wd, dt in zip(widths, dtypes)],
        scratch_shapes=[pltpu.VMEM((tm + 2 * CONV_HALO, SSD_CONV_CH), F32)],
        compiler_params=_cparams("arbitrary"),
        name="inproj",
    )(xt, xt, xt, mod, w, cos, sin, qg, kg, blockdiag, conv_w, conv_b)


def _stack_heads(q_pairs):
    lane = lax.broadcasted_iota(jnp.int32, q_pairs[0].shape, 1)
    lo = lane < HEAD_DIM
    zero = jnp.zeros_like(q_pairs[0])
    blocks = []
    for qp in q_pairs:
        blocks.append(jnp.where(lo, qp, zero))
        blocks.append(jnp.where(lo, zero, qp))
    return jnp.concatenate(blocks, axis=0)


def _unstack_heads(o, n_pairs, tq):
    lane = lax.broadcasted_iota(jnp.int32, (tq, PAIR), 1)
    lo = lane < HEAD_DIM
    return [jnp.where(lo, o[2 * p * tq:(2 * p + 1) * tq], o[(2 * p + 1) * tq:(2 * p + 2) * tq])
            for p in range(n_pairs)]


def _sink_column(sink_ref, tq):
    blk = lax.broadcasted_iota(jnp.int32, (4 * tq, 1), 0) // tq
    col = jnp.full((4 * tq, 1), sink_ref[HEAD_PERM[3]], F32)
    for b in (2, 1, 0):
        col = jnp.where(blk == b, sink_ref[HEAD_PERM[b]], col)
    return col


def _swa_kernel(sink_ref, q_ref, k_ref, v_ref, kc_ref, vc_ref, o_ref, *, seq, nblk):
    blk = SWA_BLOCK
    span = 3 * blk
    sk = _sink_column(sink_ref, blk)
    delta = (lax.broadcasted_iota(jnp.int32, (blk, span), 1)
             - lax.broadcasted_iota(jnp.int32, (blk, span), 0))
    for j in range(nblk):
        n = pl.program_id(1) * nblk + j
        rows = slice(j * blk, (j + 1) * blk)
        start = pl.multiple_of(jnp.clip((n - 1) * blk, 0, seq - span), blk)
        kl = k_ref[0, pl.ds(start, span), :]
        vl = v_ref[0, pl.ds(start, span), :]
        q = q_ref[0, rows, :]
        qs = _stack_heads([q[:, :PAIR], q[:, PAIR:]])
        s_loc = _dot_nt(qs, kl)
        s_ctx = _dot_nt(qs, kc_ref[0])
        in_window = jnp.abs(delta + (start - n * blk)) <= SWA_WINDOW
        s_loc = jnp.where(in_window[None], s_loc.reshape(4, blk, span), NEG).reshape(4 * blk, span)
        m = jnp.maximum(jnp.maximum(jnp.max(s_loc, axis=-1, keepdims=True),
                                    jnp.max(s_ctx, axis=-1, keepdims=True)), sk)
        p_loc = jnp.exp(s_loc - m)
        p_ctx = jnp.exp(s_ctx - m)
        denom = (jnp.sum(p_loc, axis=-1, keepdims=True) + jnp.sum(p_ctx, axis=-1, keepdims=True)
                 + jnp.exp(sk - m))
        o = _dot(p_loc.astype(BF16), vl) + _dot(p_ctx.astype(BF16), vc_ref[0])
        o = o * (1.0 / denom)
        o_a, o_b = _unstack_heads(o, 2, blk)
        o_ref[0, rows, :PAIR] = o_a.astype(o_ref.dtype)
        o_ref[0, rows, PAIR:] = o_b.astype(o_ref.dtype)


def _swa(sink, q, k, v, kc, vc, *, nblk=8):
    b, s, _ = q.shape
    m = kc.shape[1]
    blk = SWA_BLOCK * nblk
    assert s % blk == 0 and s >= 3 * SWA_BLOCK
    return pl.pallas_call(
        functools.partial(_swa_kernel, seq=s, nblk=nblk),
        grid=(b, s // blk),
        in_specs=[
            pl.BlockSpec(memory_space=pltpu.SMEM),
            pl.BlockSpec((1, blk, GROUP_WIDTH), lambda i, n: (i, n, 0)),
            pl.BlockSpec((1, s, PAIR), lambda i, n: (i, 0, 0)),
            pl.BlockSpec((1, s, PAIR), lambda i, n: (i, 0, 0)),
            pl.BlockSpec((1, m, PAIR), lambda i, n: (i, 0, 0)),
            pl.BlockSpec((1, m, PAIR), lambda i, n: (i, 0, 0)),
        ],
        out_specs=pl.BlockSpec((1, blk, GROUP_WIDTH), lambda i, n: (i, n, 0)),
        out_shape=jax.ShapeDtypeStruct((b, s, GROUP_WIDTH), BF16),
        compiler_params=_cparams("arbitrary", "arbitrary"),
        name="swa_attn",
    )(sink, q, k, v, kc, vc)


def _dense_attn_kernel(*refs, kv_pairs, n_src, has_sink, tq, tk):
    refs = list(refs)
    sink_ref = refs.pop(0) if has_sink else None
    q_ref = refs.pop(0)
    srcs = [(refs[2 * i], refs[2 * i + 1]) for i in range(n_src)]
    o_ref = refs[2 * n_src]
    q = q_ref[0]
    if kv_pairs == 1:
        units = [([q[:, :PAIR], q[:, PAIR:]], 0)]
    else:
        units = [([q[:, :PAIR]], 0), ([q[:, PAIR:]], 1)]
    outs = []
    for q_pairs, kv in units:
        qs = _stack_heads(q_pairs)
        nrow = qs.shape[0]
        if has_sink:
            m = _sink_column(sink_ref, tq)
            l = jnp.ones((nrow, 1), F32)
        else:
            m = jnp.full((nrow, 1), NEG, F32)
            l = jnp.zeros((nrow, 1), F32)
        acc = jnp.zeros((nrow, PAIR), F32)
        for k_ref, v_ref in srcs:
            nk = k_ref.shape[1]
            step = min(tk, nk)
            for c in range(nk // step):
                kch = k_ref[0, c * step:(c + 1) * step, kv * PAIR:(kv + 1) * PAIR]
                vch = v_ref[0, c * step:(c + 1) * step, kv * PAIR:(kv + 1) * PAIR]
                s = _dot_nt(qs, kch)
                m_new = jnp.maximum(m, jnp.max(s, axis=-1, keepdims=True))
                a = jnp.exp(m - m_new)
                p = jnp.exp(s - m_new)
                l = a * l + jnp.sum(p, axis=-1, keepdims=True)
                acc = a * acc + _dot(p.astype(BF16), vch)
                m = m_new
        o = acc * (1.0 / l)
        outs += _unstack_heads(o, len(q_pairs), tq)
    o_ref[0, :, :PAIR] = outs[0].astype(o_ref.dtype)
    o_ref[0, :, PAIR:] = outs[1].astype(o_ref.dtype)


def _dense_attn(q, srcs, *, kv_pairs, sink=None, tq=128, tk=512):
    b, s, _ = q.shape
    kvw = kv_pairs * PAIR
    has_sink = sink is not None
    in_specs, args = [], []
    if has_sink:
        in_specs.append(pl.BlockSpec(memory_space=pltpu.SMEM))
        args.append(sink)
    in_specs.append(pl.BlockSpec((1, tq, GROUP_WIDTH), lambda i, n: (i, n, 0)))
    args.append(q)
    for k, v in srcs:
        nk = k.shape[1]
        assert nk % min(tk, nk) == 0
        in_specs += [pl.BlockSpec((1, nk, kvw), lambda i, n: (i, 0, 0))] * 2
        args += [k, v]
    return pl.pallas_call(
        functools.partial(_dense_attn_kernel, kv_pairs=kv_pairs, n_src=len(srcs),
                          has_sink=has_sink, tq=tq, tk=tk),
        grid=(b, s // tq),
        in_specs=in_specs,
        out_specs=pl.BlockSpec((1, tq, GROUP_WIDTH), lambda i, n: (i, n, 0)),
        out_shape=jax.ShapeDtypeStruct((b, s, GROUP_WIDTH), BF16),
        compiler_params=_cparams("arbitrary", "arbitrary"),
        name="dense_attn",
    )(*args)


GQA_EXTRA_ROWS = 16


def _gqa_kernel(q_ref, k_ref, v_ref, kc_ref, vc_ref, o_ref, vt_sc, *, tq, tk):
    n = pl.program_id(1)
    seq = k_ref.shape[1]
    m_ctx = kc_ref.shape[1]

    @pl.when(n == 0)
    def _():
        for c in range(seq // PAIR):
            rows = slice(c * PAIR, (c + 1) * PAIR)
            vt_sc[:PAIR, rows] = v_ref[0, rows, :].astype(F32).T.astype(BF16)
        for c in range(m_ctx // PAIR):
            rows = slice(c * PAIR, (c + 1) * PAIR)
            vt_sc[:PAIR, seq + c * PAIR: seq + (c + 1) * PAIR] = vc_ref[0, rows, :].astype(F32).T.astype(BF16)
        ones_row = lax.broadcasted_iota(jnp.int32, (GQA_EXTRA_ROWS, seq + m_ctx), 0) == 0
        vt_sc[PAIR:, :] = jnp.where(ones_row, 1.0, 0.0).astype(BF16)

    q = q_ref[0]
    qs = _stack_heads([q[:, :PAIR], q[:, PAIR:]])
    chunks = [(k_ref, c * tk, tk, c * tk) for c in range(seq // tk)]
    chunks += [(kc_ref, c * min(tk, m_ctx), min(tk, m_ctx), seq + c * min(tk, m_ctx))
               for c in range(m_ctx // min(tk, m_ctx))]
    nrow = 4 * tq
    m = jnp.full((1, nrow), NEG, F32)
    acc = jnp.zeros((PAIR + GQA_EXTRA_ROWS, nrow), F32)
    for ref, lo, size, col in chunks:
        s_t = _dot_nt(ref[0, lo:lo + size, :], qs)
        m_new = jnp.maximum(m, jnp.max(s_t, axis=0, keepdims=True))
        p_t = jnp.exp2((s_t - m_new).astype(BF16))
        acc = jnp.exp2(m - m_new) * acc + _dot(vt_sc[:, col:col + size], p_t)
        m = m_new
    o_t = acc[:PAIR] * (1.0 / acc[PAIR:PAIR + 1])
    lo_rows = lax.broadcasted_iota(jnp.int32, (PAIR, tq), 0) < HEAD_DIM
    for p in range(2):
        pair_t = jnp.where(lo_rows, o_t[:, 2 * p * tq:(2 * p + 1) * tq],
                           o_t[:, (2 * p + 1) * tq:(2 * p + 2) * tq])
        o_ref[0, :, p * PAIR:(p + 1) * PAIR] = pair_t.T.astype(o_ref.dtype)


def _gqa(q, k, v, kc, vc, *, tq=1024, tk=1024):
    b, s, _ = q.shape
    m = kc.shape[1]
    tq = _token_tile(s, tq)
    assert s % tk == 0 and s % PAIR == 0 and m % PAIR == 0 and m % min(tk, m) == 0
    full = lambda i, n: (i, 0, 0)
    return pl.pallas_call(
        functools.partial(_gqa_kernel, tq=tq, tk=tk),
        grid=(b, s // tq),
        in_specs=[
            pl.BlockSpec((1, tq, GROUP_WIDTH), lambda i, n: (i, n, 0)),
            pl.BlockSpec((1, s, PAIR), full),
            pl.BlockSpec((1, s, PAIR), full),
            pl.BlockSpec((1, m, PAIR), full),
            pl.BlockSpec((1, m, PAIR), full),
        ],
        out_specs=pl.BlockSpec((1, tq, GROUP_WIDTH), lambda i, n: (i, n, 0)),
        out_shape=jax.ShapeDtypeStruct((b, s, GROUP_WIDTH), BF16),
        scratch_shapes=[pltpu.VMEM((PAIR + GQA_EXTRA_ROWS, s + m), BF16)],
        compiler_params=_cparams("arbitrary", "arbitrary"),
        name="gqa_attn",
    )(q, k, v, kc, vc)


NA_RPB_ROWS = 2 * NA_ROWS


def _na_rpb_pairs(rpb):
    h = rpb.shape[0]
    t = jnp.pad(rpb.astype(F32), ((0, 0), (1, 1), (0, HEAD_DIM - rpb.shape[2])))
    pairs = jnp.concatenate([t[:, :-1], t[:, 1:]], axis=-1)
    assert pairs.shape == (h, NA_RPB_ROWS, PAIR)
    return jnp.roll(pairs, -(NA_COLS - 1), axis=-1)


def _na_build_bias(rpb_ref, bias_sc, kind, g, rows):
    w = GRID_W
    r0 = NA_QROWS * g
    start_row = jnp.clip(r0 - NA_ROWS // 2, 0, rows - NA_KROWS)
    qcol = lax.broadcasted_iota(jnp.int32, (w, PAIR), 0)
    lane = lax.broadcasted_iota(jnp.int32, (w, PAIR), 1)
    kcol = lane % w
    odd = (lane >= w).astype(jnp.int32)
    cs = jnp.clip(qcol - NA_COLS // 2, 0, w - NA_COLS)
    col_ok = (kcol >= cs) & (kcol < cs + NA_COLS)

    def body(t, carry):
        rr = t // (NA_KROWS // 2)
        a2 = t % (NA_KROWS // 2)
        r = r0 + rr
        rs = jnp.clip(r - NA_ROWS // 2, 0, rows - NA_ROWS)
        krow0 = start_row + 2 * a2
        krow = krow0 + odd
        ok = col_ok & (krow >= rs) & (krow < rs + NA_ROWS)
        e = jnp.clip(krow0 - r + NA_ROWS, 0, NA_RPB_ROWS - 1)
        for h in range(4):
            tile = jnp.broadcast_to(rpb_ref[h, pl.ds(e, 1), :], (w, PAIR))
            for bit in range(6):
                tile = jnp.where(((qcol >> bit) & 1) == 1, pltpu.roll(tile, 1 << bit, 1), tile)
            row0 = pl.multiple_of((h % 2) * NA_QROWS * w + rr * w, w)
            bias_sc[kind, h // 2, a2, pl.ds(row0, w), :] = jnp.where(ok, tile, NEG)
        return carry

    lax.fori_loop(0, NA_QROWS * (NA_KROWS // 2), body, 0)


def _na_kernel(q_ref, k_ref, v_ref, kc_ref, vc_ref, rpb_ref, o_ref, bias_sc, *, rows, ngrp):
    groups = rows // NA_QROWS
    tq = NA_QROWS * GRID_W
    span = NA_KROWS * GRID_W
    gs = [pl.program_id(1) * ngrp + j for j in range(ngrp)]
    kinds = [jnp.where(g == 0, 0, jnp.where(g == groups - 1, 2, 1)) for g in gs]
    for g, kind in zip(gs, kinds):
        @pl.when((pl.program_id(0) == 0) & ((g == 0) | (g == 1) | (g == groups - 1)))
        def _():
            _na_build_bias(rpb_ref, bias_sc, kind, g, rows)

    for j, (g, kind) in enumerate(zip(gs, kinds)):
        qrows = slice(j * tq, (j + 1) * tq)
        start = pl.multiple_of(jnp.clip(NA_QROWS * g - NA_ROWS // 2, 0, rows - NA_KROWS) * GRID_W, GRID_W)
        for p in range(2):
            lanes = slice(p * PAIR, (p + 1) * PAIR)
            qs = _stack_heads([q_ref[0, qrows, lanes]])
            kl = k_ref[0, pl.ds(start, span), lanes]
            vl = v_ref[0, pl.ds(start, span), lanes]
            bias = jnp.concatenate([bias_sc[kind, p, a2] for a2 in range(NA_KROWS // 2)], axis=1)
            s_nb = _dot_nt(qs, kl) + bias
            s_ctx = _dot_nt(qs, kc_ref[0, :, lanes])
            m = jnp.maximum(jnp.max(s_nb, axis=-1, keepdims=True), jnp.max(s_ctx, axis=-1, keepdims=True))
            p_nb = jnp.exp(s_nb - m)
            p_ctx = jnp.exp(s_ctx - m)
            denom = jnp.sum(p_nb, axis=-1, keepdims=True) + jnp.sum(p_ctx, axis=-1, keepdims=True)
            o = _dot(p_nb.astype(BF16), vl) + _dot(p_ctx.astype(BF16), vc_ref[0, :, lanes])
            o = o * (1.0 / denom)
            o_ref[0, qrows, lanes] = _unstack_heads(o, 1, tq)[0].astype(o_ref.dtype)


def _na(q, k, v, kc, vc, rpb_pairs, *, ngrp=8):
    b, s, _ = q.shape
    m = kc.shape[1]
    rows = s // GRID_W
    groups = rows // NA_QROWS
    tq = NA_QROWS * GRID_W
    assert rows % NA_QROWS == 0 and groups >= 4 and groups % ngrp == 0
    return pl.pallas_call(
        functools.partial(_na_kernel, rows=rows, ngrp=ngrp),
        grid=(b, groups // ngrp),
        in_specs=[
            pl.BlockSpec((1, ngrp * tq, GROUP_WIDTH), lambda i, g: (i, g, 0)),
            pl.BlockSpec((1, s, GROUP_WIDTH), lambda i, g: (i, 0, 0)),
            pl.BlockSpec((1, s, GROUP_WIDTH), lambda i, g: (i, 0, 0)),
            pl.BlockSpec((1, m, GROUP_WIDTH), lambda i, g: (i, 0, 0)),
            pl.BlockSpec((1, m, GROUP_WIDTH), lambda i, g: (i, 0, 0)),
            pl.BlockSpec((4, NA_RPB_ROWS, PAIR), lambda i, g: (0, 0, 0)),
        ],
        out_specs=pl.BlockSpec((1, ngrp * tq, GROUP_WIDTH), lambda i, g: (i, g, 0)),
        out_shape=jax.ShapeDtypeStruct((b, s, GROUP_WIDTH), BF16),
        scratch_shapes=[pltpu.VMEM((3, 2, NA_KROWS // 2, 2 * tq, PAIR), F32)],
        compiler_params=_cparams("arbitrary", "arbitrary"),
        name="na_attn",
    )(q, k, v, kc, vc, rpb_pairs)


def _ssd_kernel(uf_ref, ub_ref, dtf_ref, dtb_ref, dttf_ref, dttb_ref, dtbias_ref, alog_ref,
                dtbias_col_ref, alog_col_ref, dskip_ref, h0f_ref, h0b_ref,
                yf_ref, yb_ref, hf_ref, hb_ref, hf_sc, hb_sc, *, steps, group):
    i = pl.program_id(1)
    q = SSD_CHUNK

    @pl.when(i == 0)
    def _():
        hf_sc[...] = h0f_ref[0]
        hb_sc[...] = h0b_ref[0]

    ii = lax.broadcasted_iota(jnp.int32, (q, q), 0)
    jj = lax.broadcasted_iota(jnp.int32, (q, q), 1)
    lo_lanes = jj < HEAD_DIM
    a_coef = -jnp.exp(alog_ref[...])
    a_coef_col = -jnp.exp(alog_col_ref[...])

    def chunk(u, dt_raw, dt_raw_t, col0, reverse, h):
        causal = (jj >= ii) if reverse else (jj <= ii)
        dt = _softplus(dt_raw + dtbias_ref[...])
        dt_t = _softplus(dt_raw_t + dtbias_col_ref[...])
        tri = jnp.where(causal, 1.0, 0.0).astype(BF16)
        tri_t = jnp.where((ii >= jj) if reverse else (ii <= jj), 1.0, 0.0).astype(BF16)
        a_hi, a_mid, a_lo = _split3(dt * a_coef)
        cum = _dot(tri, a_hi) + _dot(tri, a_mid) + _dot(tri, a_lo)
        a_hi, a_mid, a_lo = _split3(dt_t * a_coef_col)
        cum_t = _dot(a_hi, tri_t) + _dot(a_mid, tri_t) + _dot(a_lo, tri_t)
        end = cum[0:1, :] if reverse else cum[q - 1:q, :]
        xs = u[:, :2 * PAIR]
        ys, h_out = [], []
        for k in range(2):
            bk = u[:, 2 * PAIR + k * SSD_STATE: 2 * PAIR + (k + 1) * SSD_STATE]
            ck = u[:, 2 * PAIR + 2 * SSD_STATE + k * SSD_STATE: 2 * PAIR + 2 * SSD_STATE + (k + 1) * SSD_STATE]
            xk_b = xs[:, k * PAIR:(k + 1) * PAIR]
            xk = xk_b.astype(F32)
            ck_b = ck
            cb = _dot_nt(ck_b, bk)
            c0 = col0 + 2 * k
            cols = []
            for r in range(2):
                c = c0 + r
                seg = cum[:, c:c + 1] - cum_t[c:c + 1, :]
                lmat = jnp.exp(jnp.where(causal, seg, NEG))
                att = (cb * lmat * dt_t[c:c + 1, :]).astype(BF16)
                cols.append(_dot(att, xk_b))
            y_intra = jnp.where(lo_lanes, cols[0], cols[1])
            e_in = jnp.where(lo_lanes, jnp.exp(cum[:, c0:c0 + 1]), jnp.exp(cum[:, c0 + 1:c0 + 2]))
            h_t = h[k]
            y_state = _dot(ck_b, h_t.astype(BF16)) * e_in
            w0 = jnp.exp(end[:, c0:c0 + 1] - cum[:, c0:c0 + 1]) * dt[:, c0:c0 + 1]
            w1 = jnp.exp(end[:, c0 + 1:c0 + 2] - cum[:, c0 + 1:c0 + 2]) * dt[:, c0 + 1:c0 + 2]
            xw = (xk * jnp.where(lo_lanes, w0, w1)).astype(BF16)
            st = _dot(bk.astype(F32).T.astype(BF16), xw)
            decay = jnp.where(lo_lanes[0:1], jnp.exp(end[:, c0:c0 + 1]), jnp.exp(end[:, c0 + 1:c0 + 2]))
            h_out.append(h_t * decay + st)
            ys.append(y_intra + y_state)
        return ys, xs, h_out

    dskip = dskip_ref[...]
    h = [hf_sc[0], hf_sc[1]]
    for j in range(group):
        rows = slice(j * q, (j + 1) * q)
        ys, xs, h = chunk(uf_ref[0, rows, :], dtf_ref[0, rows, :], dttf_ref[0, :, rows], 0, False, h)
        for k in range(2):
            lanes = slice(k * PAIR, (k + 1) * PAIR)
            yf_ref[0, rows, lanes] = ys[k] + dskip[:, lanes] * xs[:, lanes].astype(F32)
    hf_sc[0], hf_sc[1] = h

    h = [hb_sc[0], hb_sc[1]]
    for j in reversed(range(group)):
        rows = slice(j * q, (j + 1) * q)
        ys, _, h = chunk(ub_ref[0, rows, :], dtb_ref[0, rows, :], dttb_ref[0, :, rows], 4, True, h)
        for k in range(2):
            yb_ref[0, rows, k * PAIR:(k + 1) * PAIR] = ys[k]
    hb_sc[0], hb_sc[1] = h

    @pl.when(i == steps - 1)
    def _():
        hf_ref[0] = hf_sc[...]
        hb_ref[0] = hb_sc[...]


def _ssd(u, dt, dt_bias, a_log, dskip, h0f, h0b, *, group=8):
    b, length, ch = u.shape
    nc = length // SSD_CHUNK
    while nc % group:
        group //= 2
    steps = nc // group
    q = group * SSD_CHUNK
    dt_t = jnp.swapaxes(dt[:, :, :DT_ROWS], 1, 2)
    column = lambda v: jnp.broadcast_to(v[0, :DT_ROWS, None], (DT_ROWS, DT_PAD))
    fwd = lambda i, c: (i, c, 0)
    bwd = lambda i, c: (i, steps - 1 - c, 0)
    fwd_t = lambda i, c: (i, 0, c)
    bwd_t = lambda i, c: (i, 0, steps - 1 - c)
    const = lambda i, c: (0, 0)
    state = lambda i, c: (i, 0, 0, 0)
    state_shape = (b, 2, SSD_STATE, PAIR)
    return pl.pallas_call(
        functools.partial(_ssd_kernel, steps=steps, group=group),
        grid=(b, steps),
        in_specs=[
            pl.BlockSpec((1, q, ch), fwd),
            pl.BlockSpec((1, q, ch), bwd),
            pl.BlockSpec((1, q, DT_PAD), fwd),
            pl.BlockSpec((1, q, DT_PAD), bwd),
            pl.BlockSpec((1, DT_ROWS, q), fwd_t),
            pl.BlockSpec((1, DT_ROWS, q), bwd_t),
            pl.BlockSpec((1, DT_PAD), const),
            pl.BlockSpec((1, DT_PAD), const),
            pl.BlockSpec((DT_ROWS, DT_PAD), const),
            pl.BlockSpec((DT_ROWS, DT_PAD), const),
            pl.BlockSpec((1, GROUP_WIDTH), const),
            pl.BlockSpec((1, 2, SSD_STATE, PAIR), state),
            pl.BlockSpec((1, 2, SSD_STATE, PAIR), state),
        ],
        out_specs=[
            pl.BlockSpec((1, q, GROUP_WIDTH), fwd),
            pl.BlockSpec((1, q, GROUP_WIDTH), bwd),
            pl.BlockSpec((1, 2, SSD_STATE, PAIR), state),
            pl.BlockSpec((1, 2, SSD_STATE, PAIR), state),
        ],
        out_shape=[
            jax.ShapeDtypeStruct((b, length, GROUP_WIDTH), F32),
            jax.ShapeDtypeStruct((b, length, GROUP_WIDTH), F32),
            jax.ShapeDtypeStruct(state_shape, F32),
            jax.ShapeDtypeStruct(state_shape, F32),
        ],
        scratch_shapes=[pltpu.VMEM((2, SSD_STATE, PAIR), F32), pltpu.VMEM((2, SSD_STATE, PAIR), F32)],
        compiler_params=_cparams("arbitrary", "arbitrary"),
        name="ssd_scan",
    )(u, u, dt, dt, dt_t, dt_t, dt_bias, a_log, column(dt_bias), column(a_log), dskip, h0f, h0b)


def _head_blocks(start, perm):
    return [(start + h * HEAD_DIM, start + (h + 1) * HEAD_DIM) for h in perm]


def _take_blocks(a, axis, blocks):
    return jnp.concatenate([lax.slice_in_dim(a, lo, hi, axis=axis) for lo, hi in blocks], axis=axis)


_INPROJ_BLOCKS = (_head_blocks(0, HEAD_PERM) + [(256, 1536)] + _head_blocks(1544, HEAD_PERM)
                  + [(1800, 2824), (1536, 1544)])
_OUTPROJ_BLOCKS = (_head_blocks(0, HEAD_PERM) + [(256, 512)] + _head_blocks(512, HEAD_PERM)
                   + [(768, 1024)])


def _rope_tables(seq):
    t = np.arange(seq)
    quarter = HEAD_DIM // 4
    inv = ROPE_BASE ** (-jnp.arange(quarter, dtype=F32) / quarter)
    a_row = jnp.asarray(t // GRID_W, F32)[:, None] * inv
    a_col = jnp.asarray(t % GRID_W, F32)[:, None] * inv
    cos = jnp.concatenate([jnp.cos(a_row), jnp.cos(a_row), jnp.cos(a_col), jnp.cos(a_col)], axis=1)
    sin = jnp.concatenate([-jnp.sin(a_row), jnp.sin(a_row), -jnp.sin(a_col), jnp.sin(a_col)], axis=1)
    return jnp.tile(cos, (1, 2)), jnp.tile(sin, (1, 2))


def _pad_lanes(v, width):
    v = v.reshape(1, -1).astype(F32)
    return jnp.pad(v, ((0, 0), (0, width - v.shape[1])))


def _token_tile(n, cap):
    t = cap
    while n % t:
        t //= 2
    return t


def kernel(x, c, ctx, c_ctx, ada_w, ada_b, ln_g, ln_b, ffn1_w_in, ffn1_w_out, mix_w_in, mix_w_out,
           mix_norm_g, swa_sink, ssd_conv_w, ssd_conv_b, ssd_dt_bias, ssd_A_log, ssd_D,
           gqa_q_norm, gqa_k_norm, na_rpb, ffn2_w_in, ffn2_w_out):
    bsz, seq, d = x.shape
    m_ctx = ctx.shape[1]
    depth = ada_w.shape[0]
    alpha = float((2 * depth) ** 0.25)
    mod_rows = 16 * ((bsz + 1 + 15) // 16)
    cc = jnp.concatenate([c, c_ctx[None], jnp.zeros((mod_rows - bsz - 1, d), F32)], axis=0)
    mods = _ada(cc, ada_w, ada_b)

    cos_x, sin_x = _rope_tables(seq)
    tm_x = _token_tile(seq, 1024)
    tm_p = _token_tile(seq, 1024)
    tm_c = _token_tile(m_ctx, 256)
    tm_cf = _token_tile(bsz * m_ctx, 1024)
    cos_c = jnp.ones((tm_c, PAIR), F32)
    sin_c = jnp.zeros((tm_c, PAIR), F32)
    blockdiag = jnp.asarray(np.kron(np.eye(2), np.ones((HEAD_DIM, HEAD_DIM))), BF16)
    w1_in, w1_out = ffn1_w_in.astype(BF16), ffn1_w_out.astype(BF16)
    w2_in, w2_out = ffn2_w_in.astype(BF16), ffn2_w_out.astype(BF16)
    w_mix = _take_blocks(mix_w_in, 2, _INPROJ_BLOCKS).astype(BF16)
    w_mix = jnp.pad(w_mix, ((0, 0), (0, 0), (0, _P_END - w_mix.shape[2])))
    w_o = _take_blocks(mix_w_out, 1, _OUTPROJ_BLOCKS).astype(BF16)
    norm_g_all = _take_blocks(mix_norm_g, 1, _OUTPROJ_BLOCKS)

    xt = x.reshape(bsz * seq, d)
    ct = ctx.reshape(bsz * m_ctx, d)
    zero_state = jnp.zeros((bsz, 2, SSD_STATE, PAIR), F32)

    for l in range(depth):
        last = l == depth - 1
        mod_x = mods[l, :bsz].reshape(bsz, N_MOD, d)
        mod_c = mods[l, bsz:bsz + 1].reshape(1, N_MOD, d)
        norm_g = norm_g_all[l].reshape(1, d)
        qg = jnp.tile(gqa_q_norm[l], 2).reshape(1, PAIR)
        kg = jnp.tile(gqa_k_norm[l], 2).reshape(1, PAIR)
        conv_w = jnp.pad(ssd_conv_w[l].reshape(SSD_CONV, SSD_CONV_CH), ((0, 8 - SSD_CONV), (0, 0)))
        conv_b = ssd_conv_b[l].reshape(1, SSD_CONV_CH)
        dt_bias = _pad_lanes(ssd_dt_bias[l], DT_PAD)
        a_log = _pad_lanes(ssd_A_log[l], DT_PAD)
        dskip = jnp.repeat(ssd_D[l], HEAD_DIM).reshape(1, GROUP_WIDTH)
        sink = swa_sink[l].astype(F32)
        rpb_pairs = _na_rpb_pairs(na_rpb[l])

        ffn_x = functools.partial(_ffn, tokens_per_mod=seq, alpha=alpha, tm=tm_x)
        ffn_c = functools.partial(_ffn, tokens_per_mod=bsz * m_ctx, alpha=alpha, tm=tm_cf)

        xt = ffn_x(xt, mod_x, (0, 1, 2), w1_in, w1_out, l, ln_g[l, 0], ln_b[l, 0])
        ct = ffn_c(ct, mod_c, (0, 1, 2), w1_in, w1_out, l, ln_g[l, 0], ln_b[l, 0])

        px = _inproj(xt, mod_x, w_mix, l, cos_x, sin_x, qg, kg, blockdiag, conv_w, conv_b,
                     tokens_per_mod=seq, seq_len=seq, tm=tm_p, gqa_q_scale=Q_SCALE * LOG2E)
        pc = _inproj(ct, mod_c, w_mix, l, cos_c, sin_c, qg, kg, blockdiag, conv_w, conv_b,
                     tokens_per_mod=bsz * m_ctx, seq_len=m_ctx, tm=tm_c, gqa_q_scale=Q_SCALE)
        aq, ak, av, bz, bu, bdt, cq, ck, cv, dq, dk, dv = [
            t.reshape(bsz, seq, t.shape[-1]) for t in px]
        aq_c, ak_c, av_c, bz_c, bu_c, bdt_c, cq_c, ck_c, cv_c, dq_c, dk_c, dv_c = [
            t.reshape(bsz, m_ctx, t.shape[-1]) for t in pc]

        ssd = functools.partial(_ssd, dt_bias=dt_bias, a_log=a_log, dskip=dskip)
        yf_c, yb_c, hf_c, hb_c = ssd(bu_c, bdt_c, h0f=zero_state, h0b=zero_state)
        yf, yb, _, _ = ssd(bu, bdt, h0f=hf_c, h0b=hb_c)

        oa = _swa(sink, aq, ak, av, ak_c, av_c)
        oc = _gqa(cq, ck, cv, ck_c, cv_c)
        od = _na(dq, dk, dv, dk_c, dv_c, rpb_pairs)

        flat = lambda t: t.reshape(-1, t.shape[-1])
        xt = _mixer_ffn(xt, mod_x, flat(oa), flat(yf), flat(yb), flat(bz), flat(oc), flat(od), norm_g, w_o,
                        ln_g[l, 1], ln_b[l, 1], w2_in, w2_out, l, ln_g[l, 2], ln_b[l, 2],
                        tokens_per_mod=seq, alpha=alpha, tm=tm_x)

        if not last:
            tq_c = _token_tile(m_ctx, 128)
            oa_c = _dense_attn(aq_c, [(ak_c, av_c)], kv_pairs=1, sink=sink, tq=tq_c)
            oc_c = _dense_attn(cq_c, [(ck_c, cv_c)], kv_pairs=1, tq=tq_c)
            od_c = _dense_attn(dq_c, [(dk_c, dv_c)], kv_pairs=2, tq=tq_c)
            ct = _mixer_ffn(ct, mod_c, flat(oa_c), flat(yf_c), flat(yb_c), flat(bz_c), flat(oc_c),
                            flat(od_c), norm_g, w_o, ln_g[l, 1], ln_b[l, 1], w2_in, w2_out, l,
                            ln_g[l, 2], ln_b[l, 2], tokens_per_mod=bsz * m_ctx, alpha=alpha, tm=tm_cf)

    return xt.reshape(bsz, seq, d)
```

```python
import functools

import numpy as np
import jax
import jax.numpy as jnp
from jax import lax
from jax.experimental import pallas as pl
from jax.experimental.pallas import tpu as pltpu

F32 = jnp.float32
BF16 = jnp.bfloat16

HEAD_DIM = 64
PAIR = 2 * HEAD_DIM
GROUP_WIDTH = 256
GRID_W = 64
N_MOD = 9
SWA_WINDOW = 128
SWA_BLOCK = 128
SSD_CHUNK = 128
SSD_STATE = 128
SSD_CONV = 5
CONV_HALO = 8
FFN_SUBTILE = 256
SSD_CONV_CH = 768
NA_ROWS = 8
NA_COLS = 16
NA_QROWS = 4
NA_KROWS = 12
ROPE_BASE = 10000.0
EPS = 1e-5
NEG = -1e30
Q_SCALE = HEAD_DIM ** -0.5
LOG2E = 1.4426950408889634
DT_PAD = 128
DT_ROWS = 16
HEAD_PERM = (0, 2, 1, 3)

VMEM_LIMIT_BYTES = 56 * 1024 * 1024


def _cparams(*sem):
    return pltpu.CompilerParams(dimension_semantics=sem, vmem_limit_bytes=VMEM_LIMIT_BYTES)


def _dot(a, b):
    return jnp.dot(a, b, preferred_element_type=F32)


def _dot_nt(a, b):
    return lax.dot_general(a, b, (((1,), (1,)), ((), ())), preferred_element_type=F32)


def _sigmoid(x):
    return 1.0 / (1.0 + jnp.exp(-x))


def _silu(x):
    return x * _sigmoid(x)


def _softplus(x):
    return jnp.maximum(x, 0.0) + jnp.log(1.0 + jnp.exp(-jnp.abs(x)))


def _split3(a):
    hi = a.astype(BF16)
    r1 = a - hi.astype(F32)
    mid = r1.astype(BF16)
    lo = (r1 - mid.astype(F32)).astype(BF16)
    return hi, mid, lo


def _layer_norm(y, g, b):
    mu = jnp.mean(y, axis=-1, keepdims=True)
    d = y - mu
    var = jnp.mean(d * d, axis=-1, keepdims=True)
    return d * lax.rsqrt(var + EPS) * g + b


def _ada_kernel(c_ref, w_ref, b_ref, o_ref):
    s = _silu(c_ref[...])
    s_hi = s.astype(BF16)
    s_lo = (s - s_hi.astype(F32)).astype(BF16)
    w = w_ref[0]
    w_hi = w.astype(BF16)
    w_lo = (w - w_hi.astype(F32)).astype(BF16)
    o_ref[0] = _dot(s_hi, w_hi) + _dot(s_lo, w_hi) + _dot(s_hi, w_lo) + b_ref[0]


def _ada(cc, ada_w, ada_b):
    depth, d, n = ada_w.shape
    rows = cc.shape[0]
    tn = 1024
    return pl.pallas_call(
        _ada_kernel,
        grid=(depth, n // tn),
        in_specs=[
            pl.BlockSpec((rows, d), lambda l, j: (0, 0)),
            pl.BlockSpec((1, d, tn), lambda l, j: (l, 0, j)),
            pl.BlockSpec((1, 1, tn), lambda l, j: (l, 0, j)),
        ],
        out_specs=pl.BlockSpec((1, rows, tn), lambda l, j: (l, 0, j)),
        out_shape=jax.ShapeDtypeStruct((depth, rows, n), F32),
        compiler_params=_cparams("arbitrary", "arbitrary"),
        name="ada_mod",
    )(cc, ada_w, ada_b.reshape(depth, 1, n))


def _ffn_rows(x, mod_ref, rows, wa_ref, wu_ref, wo_ref, g, b, alpha):
    r_shift, r_scale, r_gate = rows
    h = x * (1.0 + mod_ref[0, r_scale:r_scale + 1, :]) + mod_ref[0, r_shift:r_shift + 1, :]
    h = h.astype(BF16)
    a = _dot(h, wa_ref[...])
    u = _dot(h, wu_ref[...])
    gated = (_silu(a) * u).astype(BF16)
    f = _dot(gated, wo_ref[...])
    y = alpha * x + (0.5 * mod_ref[0, r_gate:r_gate + 1, :]) * f
    return _layer_norm(y, g, b)


def _mixer_rows(x, mod_ref, parts, ng_ref, w_ref, g, b, alpha):
    acc = None
    for gi, y in enumerate(parts):
        rows = slice(gi * GROUP_WIDTH, (gi + 1) * GROUP_WIDTH)
        ms = jnp.mean(y * y, axis=-1, keepdims=True)
        yn = (y * lax.rsqrt(ms + EPS) * ng_ref[:, rows]).astype(BF16)
        term = _dot(yn, w_ref[rows, :])
        acc = term if acc is None else acc + term
    return _layer_norm(alpha * x + mod_ref[0, 5:6, :] * acc, g, b)


def _subtiles(tm):
    sub = FFN_SUBTILE if tm % FFN_SUBTILE == 0 else tm
    return [slice(r0, r0 + sub) for r0 in range(0, tm, sub)]


def _ffn_kernel(x_ref, mod_ref, wa_ref, wu_ref, wo_ref, g_ref, b_ref, o_ref, *, rows, alpha):
    for r in _subtiles(x_ref.shape[0]):
        o_ref[r, :] = _ffn_rows(x_ref[r, :], mod_ref, rows, wa_ref, wu_ref, wo_ref,
                                g_ref[...], b_ref[...], alpha)


def _mixer_ffn_kernel(x_ref, mod_ref, oa_ref, yf_ref, yb_ref, z_ref, oc_ref, od_ref, ng_ref, wm_ref,
                      g1_ref, b1_ref, wa_ref, wu_ref, wo_ref, g2_ref, b2_ref, o_ref, *, alpha):
    for r in _subtiles(x_ref.shape[0]):
        ob = (yf_ref[r, :] + yb_ref[r, :]) * _silu(z_ref[r, :].astype(F32))
        parts = (oa_ref[r, :].astype(F32), ob, oc_ref[r, :].astype(F32), od_ref[r, :].astype(F32))
        x_mid = _mixer_rows(x_ref[r, :], mod_ref, parts, ng_ref, wm_ref, g1_ref[...], b1_ref[...], alpha)
        o_ref[r, :] = _ffn_rows(x_mid, mod_ref, (6, 7, 8), wa_ref, wu_ref, wo_ref,
                                g2_ref[...], b2_ref[...], alpha)


def _layer_weight(shape, layer, col_block=0):
    return pl.BlockSpec((None,) + shape, lambda i: (layer, 0, col_block), pipeline_mode=pl.Buffered(1))


def _ffn(xt, mod, rows, w_in, w_out, layer, ln_g, ln_b, *, tokens_per_mod, alpha, tm):
    t, d = xt.shape
    dff = w_out.shape[1]
    tiles_per_mod = tokens_per_mod // tm
    return pl.pallas_call(
        functools.partial(_ffn_kernel, rows=rows, alpha=alpha),
        grid=(t // tm,),
        in_specs=[
            pl.BlockSpec((tm, d), lambda i: (i, 0)),
            pl.BlockSpec((1, N_MOD, d), lambda i: (i // tiles_per_mod, 0, 0)),
            _layer_weight((d, dff), layer, 0),
            _layer_weight((d, dff), layer, 1),
            _layer_weight((dff, d), layer),
            pl.BlockSpec((1, d), lambda i: (0, 0)),
            pl.BlockSpec((1, d), lambda i: (0, 0)),
        ],
        out_specs=pl.BlockSpec((tm, d), lambda i: (i, 0)),
        out_shape=jax.ShapeDtypeStruct((t, d), F32),
        compiler_params=_cparams("arbitrary"),
        name="ffn",
    )(xt, mod, w_in, w_in, w_out, ln_g.reshape(1, d), ln_b.reshape(1, d))


def _mixer_ffn(xt, mod, oa, yf, yb, z, oc, od, norm_g, w_mix_out, ln1_g, ln1_b, w_in, w_out, layer,
               ln2_g, ln2_b, *, tokens_per_mod, alpha, tm):
    t, d = xt.shape
    dff = w_out.shape[1]
    tiles_per_mod = tokens_per_mod // tm
    row = lambda i: (i, 0)
    const = lambda i: (0, 0)
    grp = pl.BlockSpec((tm, GROUP_WIDTH), row)
    vec = pl.BlockSpec((1, d), const)
    return pl.pallas_call(
        functools.partial(_mixer_ffn_kernel, alpha=alpha),
        grid=(t // tm,),
        in_specs=[
            pl.BlockSpec((tm, d), row),
            pl.BlockSpec((1, N_MOD, d), lambda i: (i // tiles_per_mod, 0, 0)),
            grp, grp, grp, grp, grp, grp,
            vec,
            _layer_weight((d, d), layer),
            vec, vec,
            _layer_weight((d, dff), layer, 0),
            _layer_weight((d, dff), layer, 1),
            _layer_weight((dff, d), layer),
            vec, vec,
        ],
        out_specs=pl.BlockSpec((tm, d), row),
        out_shape=jax.ShapeDtypeStruct((t, d), F32),
        compiler_params=_cparams("arbitrary"),
        name="mixer_ffn",
    )(xt, mod, oa, yf, yb, z, oc, od, norm_g, w_mix_out, ln1_g.reshape(1, d), ln1_b.reshape(1, d),
      w_in, w_in, w_out, ln2_g.reshape(1, d), ln2_b.reshape(1, d))


_P_AQ, _P_AK, _P_AV, _P_Z, _P_XBC = 0, 256, 384, 512, 768
_P_CQ, _P_CK, _P_CV, _P_DQ, _P_DK, _P_DV, _P_DT, _P_END = 1536, 1792, 1920, 2048, 2304, 2560, 2816, 2944


def _rope_pair(t, cos, sin):
    lane = lax.broadcasted_iota(jnp.int32, t.shape, 1)
    up = pltpu.roll(t, HEAD_DIM // 4, 1)
    dn = pltpu.roll(t, PAIR - HEAD_DIM // 4, 1)
    partner = jnp.where((lane % 32) < 16, dn, up)
    return t * cos + partner * sin


def _rms_pair(t, g, blockdiag):
    sq = t * t
    hi = sq.astype(BF16)
    lo = (sq - hi.astype(F32)).astype(BF16)
    ss = _dot(hi, blockdiag) + _dot(lo, blockdiag)
    return t * lax.rsqrt(ss * (1.0 / HEAD_DIM) + EPS) * g


def _inproj_kernel(x_ref, xp_ref, xn_ref, mod_ref, w_ref, cos_ref, sin_ref, qg_ref, kg_ref, bd_ref,
                   cw_ref, cb_ref,
                   aq_ref, ak_ref, av_ref, z_ref, u_ref, dt_ref,
                   cq_ref, ck_ref, cv_ref, dq_ref, dk_ref, dv_ref, ext_sc, *, gqa_q_scale, tiles_per_seq):
    tm = x_ref.shape[0]
    halo = xp_ref.shape[0]
    pos = pl.program_id(0) % tiles_per_seq
    x_ext = jnp.concatenate([xp_ref[...], x_ref[...], xn_ref[...]], axis=0)
    h = (x_ext * (1.0 + mod_ref[0, 4:5, :]) + mod_ref[0, 3:4, :]).astype(BF16)
    y_ext = _dot(h, w_ref[...])
    y = y_ext[halo:halo + tm]
    cos = cos_ref[...]
    sin = sin_ref[...]
    bd = bd_ref[...]

    def pairs(lo, n):
        return [y[:, lo + PAIR * p: lo + PAIR * (p + 1)] for p in range(n)]

    xbc = y_ext[:, _P_XBC:_P_XBC + SSD_CONV_CH]
    ext_sc[:halo] = jnp.where(pos == 0, 0.0, xbc[:halo])
    ext_sc[halo:halo + tm] = xbc[halo:halo + tm]
    ext_sc[halo + tm:] = jnp.where(pos == tiles_per_seq - 1, 0.0, xbc[halo + tm:])
    conv = cb_ref[...]
    for k in range(SSD_CONV):
        lo = halo - SSD_CONV // 2 + k
        conv = conv + cw_ref[k:k + 1, :] * ext_sc[lo:lo + tm, :]
    u_ref[...] = _silu(conv).astype(u_ref.dtype)

    def put(ref, parts, scale=None):
        for p, part in enumerate(parts):
            if scale is not None:
                part = part * scale
            ref[:, PAIR * p: PAIR * (p + 1)] = part.astype(ref.dtype)

    swa = pairs(_P_AQ, 4)
    put(aq_ref, [_rope_pair(t, cos, sin) for t in swa[:2]], Q_SCALE)
    put(ak_ref, [_rope_pair(swa[2], cos, sin)])
    put(av_ref, [swa[3]])
    put(z_ref, pairs(_P_Z, 2))
    put(dt_ref, pairs(_P_DT, 1))
    gqa = pairs(_P_CQ, 4)
    put(cq_ref, [_rope_pair(_rms_pair(t, qg_ref[...], bd), cos, sin) for t in gqa[:2]], gqa_q_scale)
    put(ck_ref, [_rope_pair(_rms_pair(gqa[2], kg_ref[...], bd), cos, sin)])
    put(cv_ref, [gqa[3]])
    na = pairs(_P_DQ, 6)
    put(dq_ref, na[:2], Q_SCALE)
    put(dk_ref, na[2:4])
    put(dv_ref, na[4:])


def _inproj(xt, mod, w, layer, cos, sin, qg, kg, blockdiag, conv_w, conv_b, *, tokens_per_mod, seq_len, tm,
            gqa_q_scale):
    t, d = xt.shape
    tiles_per_mod = tokens_per_mod // tm
    table_tiles = cos.shape[0] // tm
    widths = (256, 128, 128, 256, SSD_CONV_CH, DT_PAD, 256, 128, 128, 256, 256, 256)
    dtypes = (BF16,) * 5 + (F32,) + (BF16,) * 6
    per = tm // CONV_HALO
    last_halo = t // CONV_HALO - 1
    row = lambda i: (i, 0)
    const = lambda i: (0, 0)
    return pl.pallas_call(
        functools.partial(_inproj_kernel, gqa_q_scale=gqa_q_scale, tiles_per_seq=seq_len // tm),
        grid=(t // tm,),
        in_specs=[
            pl.BlockSpec((tm, d), row),
            pl.BlockSpec((CONV_HALO, d), lambda i: (jnp.maximum(i * per - 1, 0), 0)),
            pl.BlockSpec((CONV_HALO, d), lambda i: (jnp.minimum((i + 1) * per, last_halo), 0)),
            pl.BlockSpec((1, N_MOD, d), lambda i: (i // tiles_per_mod, 0, 0)),
            _layer_weight((d, _P_END), layer),
            pl.BlockSpec((tm, PAIR), lambda i: (i % table_tiles, 0)),
            pl.BlockSpec((tm, PAIR), lambda i: (i % table_tiles, 0)),
            pl.BlockSpec((1, PAIR), const),
            pl.BlockSpec((1, PAIR), const),
            pl.BlockSpec((PAIR, PAIR), const),
            pl.BlockSpec((8, SSD_CONV_CH), const),
            pl.BlockSpec((1, SSD_CONV_CH), const),
        ],
        out_specs=[pl.BlockSpec((tm, wd), row) for wd in widths],
        out_shape=[jax.ShapeDtypeStruct((t, wd), dt) for wd, dt in zip(widths, dtypes)],
        scratch_shapes=[pltpu.VMEM((tm + 2 * CONV_HALO, SSD_CONV_CH), F32)],
        compiler_params=_cparams("arbitrary"),
        name="inproj",
    )(xt, xt, xt, mod, w, cos, sin, qg, kg, blockdiag, conv_w, conv_b)


def _stack_heads(q_pairs):
    lane = lax.broadcasted_iota(jnp.int32, q_pairs[0].shape, 1)
    lo = lane < HEAD_DIM
    zero = jnp.zeros_like(q_pairs[0])
    blocks = []
    for qp in q_pairs:
        blocks.append(jnp.where(lo, qp, zero))
        blocks.append(jnp.where(lo, zero, qp))
    return jnp.concatenate(blocks, axis=0)


def _unstack_heads(o, n_pairs, tq):
    lane = lax.broadcasted_iota(jnp.int32, (tq, PAIR), 1)
    lo = lane < HEAD_DIM
    return [jnp.where(lo, o[2 * p * tq:(2 * p + 1) * tq], o[(2 * p + 1) * tq:(2 * p + 2) * tq])
            for p in range(n_pairs)]


def _sink_column(sink_ref, tq):
    blk = lax.broadcasted_iota(jnp.int32, (4 * tq, 1), 0) // tq
    col = jnp.full((4 * tq, 1), sink_ref[HEAD_PERM[3]], F32)
    for b in (2, 1, 0):
        col = jnp.where(blk == b, sink_ref[HEAD_PERM[b]], col)
    return col


def _swa_kernel(sink_ref, q_ref, k_ref, v_ref, kc_ref, vc_ref, o_ref, *, seq, nblk):
    blk = SWA_BLOCK
    span = 3 * blk
    sk = _sink_column(sink_ref, blk)
    delta = (lax.broadcasted_iota(jnp.int32, (blk, span), 1)
             - lax.broadcasted_iota(jnp.int32, (blk, span), 0))
    for j in range(nblk):
        n = pl.program_id(1) * nblk + j
        rows = slice(j * blk, (j + 1) * blk)
        start = pl.multiple_of(jnp.clip((n - 1) * blk, 0, seq - span), blk)
        kl = k_ref[0, pl.ds(start, span), :]
        vl = v_ref[0, pl.ds(start, span), :]
        q = q_ref[0, rows, :]
        qs = _stack_heads([q[:, :PAIR], q[:, PAIR:]])
        s_loc = _dot_nt(qs, kl)
        s_ctx = _dot_nt(qs, kc_ref[0])
        in_window = jnp.abs(delta + (start - n * blk)) <= SWA_WINDOW
        s_loc = jnp.where(in_window[None], s_loc.reshape(4, blk, span), NEG).reshape(4 * blk, span)
        m = jnp.maximum(jnp.maximum(jnp.max(s_loc, axis=-1, keepdims=True),
                                    jnp.max(s_ctx, axis=-1, keepdims=True)), sk)
        p_loc = jnp.exp(s_loc - m)
        p_ctx = jnp.exp(s_ctx - m)
        denom = (jnp.sum(p_loc, axis=-1, keepdims=True) + jnp.sum(p_ctx, axis=-1, keepdims=True)
                 + jnp.exp(sk - m))
        o = _dot(p_loc.astype(BF16), vl) + _dot(p_ctx.astype(BF16), vc_ref[0])
        o = o * (1.0 / denom)
        o_a, o_b = _unstack_heads(o, 2, blk)
        o_ref[0, rows, :PAIR] = o_a.astype(o_ref.dtype)
        o_ref[0, rows, PAIR:] = o_b.astype(o_ref.dtype)


def _swa(sink, q, k, v, kc, vc, *, nblk=8):
    b, s, _ = q.shape
    m = kc.shape[1]
    blk = SWA_BLOCK * nblk
    assert s % blk == 0 and s >= 3 * SWA_BLOCK
    return pl.pallas_call(
        functools.partial(_swa_kernel, seq=s, nblk=nblk),
        grid=(b, s // blk),
        in_specs=[
            pl.BlockSpec(memory_space=pltpu.SMEM),
            pl.BlockSpec((1, blk, GROUP_WIDTH), lambda i, n: (i, n, 0)),
            pl.BlockSpec((1, s, PAIR), lambda i, n: (i, 0, 0)),
            pl.BlockSpec((1, s, PAIR), lambda i, n: (i, 0, 0)),
            pl.BlockSpec((1, m, PAIR), lambda i, n: (i, 0, 0)),
            pl.BlockSpec((1, m, PAIR), lambda i, n: (i, 0, 0)),
        ],
        out_specs=pl.BlockSpec((1, blk, GROUP_WIDTH), lambda i, n: (i, n, 0)),
        out_shape=jax.ShapeDtypeStruct((b, s, GROUP_WIDTH), BF16),
        compiler_params=_cparams("arbitrary", "arbitrary"),
        name="swa_attn",
    )(sink, q, k, v, kc, vc)


def _dense_attn_kernel(*refs, kv_pairs, n_src, has_sink, tq, tk):
    refs = list(refs)
    sink_ref = refs.pop(0) if has_sink else None
    q_ref = refs.pop(0)
    srcs = [(refs[2 * i], refs[2 * i + 1]) for i in range(n_src)]
    o_ref = refs[2 * n_src]
    q = q_ref[0]
    if kv_pairs == 1:
        units = [([q[:, :PAIR], q[:, PAIR:]], 0)]
    else:
        units = [([q[:, :PAIR]], 0), ([q[:, PAIR:]], 1)]
    outs = []
    for q_pairs, kv in units:
        qs = _stack_heads(q_pairs)
        nrow = qs.shape[0]
        if has_sink:
            m = _sink_column(sink_ref, tq)
            l = jnp.ones((nrow, 1), F32)
        else:
            m = jnp.full((nrow, 1), NEG, F32)
            l = jnp.zeros((nrow, 1), F32)
        acc = jnp.zeros((nrow, PAIR), F32)
        for k_ref, v_ref in srcs:
            nk = k_ref.shape[1]
            step = min(tk, nk)
            for c in range(nk // step):
                kch = k_ref[0, c * step:(c + 1) * step, kv * PAIR:(kv + 1) * PAIR]
                vch = v_ref[0, c * step:(c + 1) * step, kv * PAIR:(kv + 1) * PAIR]
                s = _dot_nt(qs, kch)
                m_new = jnp.maximum(m, jnp.max(s, axis=-1, keepdims=True))
                a = jnp.exp(m - m_new)
                p = jnp.exp(s - m_new)
                l = a * l + jnp.sum(p, axis=-1, keepdims=True)
                acc = a * acc + _dot(p.astype(BF16), vch)
                m = m_new
        o = acc * (1.0 / l)
        outs += _unstack_heads(o, len(q_pairs), tq)
    o_ref[0, :, :PAIR] = outs[0].astype(o_ref.dtype)
    o_ref[0, :, PAIR:] = outs[1].astype(o_ref.dtype)


def _dense_attn(q, srcs, *, kv_pairs, sink=None, tq=128, tk=512):
    b, s, _ = q.shape
    kvw = kv_pairs * PAIR
    has_sink = sink is not None
    in_specs, args = [], []
    if has_sink:
        in_specs.append(pl.BlockSpec(memory_space=pltpu.SMEM))
        args.append(sink)
    in_specs.append(pl.BlockSpec((1, tq, GROUP_WIDTH), lambda i, n: (i, n, 0)))
    args.append(q)
    for k, v in srcs:
        nk = k.shape[1]
        assert nk % min(tk, nk) == 0
        in_specs += [pl.BlockSpec((1, nk, kvw), lambda i, n: (i, 0, 0))] * 2
        args += [k, v]
    return pl.pallas_call(
        functools.partial(_dense_attn_kernel, kv_pairs=kv_pairs, n_src=len(srcs),
                          has_sink=has_sink, tq=tq, tk=tk),
        grid=(b, s // tq),
        in_specs=in_specs,
        out_specs=pl.BlockSpec((1, tq, GROUP_WIDTH), lambda i, n: (i, n, 0)),
        out_shape=jax.ShapeDtypeStruct((b, s, GROUP_WIDTH), BF16),
        compiler_params=_cparams("arbitrary", "arbitrary"),
        name="dense_attn",
    )(*args)


GQA_EXTRA_ROWS = 16


def _gqa_kernel(q_ref, k_ref, v_ref, kc_ref, vc_ref, o_ref, vt_sc, *, tq, tk):
    n = pl.program_id(1)
    seq = k_ref.shape[1]
    m_ctx = kc_ref.shape[1]

    @pl.when(n == 0)
    def _():
        for c in range(seq // PAIR):
            rows = slice(c * PAIR, (c + 1) * PAIR)
            vt_sc[:PAIR, rows] = v_ref[0, rows, :].astype(F32).T.astype(BF16)
        for c in range(m_ctx // PAIR):
            rows = slice(c * PAIR, (c + 1) * PAIR)
            vt_sc[:PAIR, seq + c * PAIR: seq + (c + 1) * PAIR] = vc_ref[0, rows, :].astype(F32).T.astype(BF16)
        ones_row = lax.broadcasted_iota(jnp.int32, (GQA_EXTRA_ROWS, seq + m_ctx), 0) == 0
        vt_sc[PAIR:, :] = jnp.where(ones_row, 1.0, 0.0).astype(BF16)

    q = q_ref[0]
    qs = _stack_heads([q[:, :PAIR], q[:, PAIR:]])
    chunks = [(k_ref, c * tk, tk, c * tk) for c in range(seq // tk)]
    chunks += [(kc_ref, c * min(tk, m_ctx), min(tk, m_ctx), seq + c * min(tk, m_ctx))
               for c in range(m_ctx // min(tk, m_ctx))]
    nrow = 4 * tq
    m = jnp.full((1, nrow), NEG, F32)
    acc = jnp.zeros((PAIR + GQA_EXTRA_ROWS, nrow), F32)
    for ref, lo, size, col in chunks:
        s_t = _dot_nt(ref[0, lo:lo + size, :], qs)
        m_new = jnp.maximum(m, jnp.max(s_t, axis=0, keepdims=True))
        p_t = jnp.exp2((s_t - m_new).astype(BF16))
        acc = jnp.exp2(m - m_new) * acc + _dot(vt_sc[:, col:col + size], p_t)
        m = m_new
    o_t = acc[:PAIR] * (1.0 / acc[PAIR:PAIR + 1])
    lo_rows = lax.broadcasted_iota(jnp.int32, (PAIR, tq), 0) < HEAD_DIM
    for p in range(2):
        pair_t = jnp.where(lo_rows, o_t[:, 2 * p * tq:(2 * p + 1) * tq],
                           o_t[:, (2 * p + 1) * tq:(2 * p + 2) * tq])
        o_ref[0, :, p * PAIR:(p + 1) * PAIR] = pair_t.T.astype(o_ref.dtype)


def _gqa(q, k, v, kc, vc, *, tq=1024, tk=1024):
    b, s, _ = q.shape
    m = kc.shape[1]
    tq = _token_tile(s, tq)
    assert s % tk == 0 and s % PAIR == 0 and m % PAIR == 0 and m % min(tk, m) == 0
    full = lambda i, n: (i, 0, 0)
    return pl.pallas_call(
        functools.partial(_gqa_kernel, tq=tq, tk=tk),
        grid=(b, s // tq),
        in_specs=[
            pl.BlockSpec((1, tq, GROUP_WIDTH), lambda i, n: (i, n, 0)),
            pl.BlockSpec((1, s, PAIR), full),
            pl.BlockSpec((1, s, PAIR), full),
            pl.BlockSpec((1, m, PAIR), full),
            pl.BlockSpec((1, m, PAIR), full),
        ],
        out_specs=pl.BlockSpec((1, tq, GROUP_WIDTH), lambda i, n: (i, n, 0)),
        out_shape=jax.ShapeDtypeStruct((b, s, GROUP_WIDTH), BF16),
        scratch_shapes=[pltpu.VMEM((PAIR + GQA_EXTRA_ROWS, s + m), BF16)],
        compiler_params=_cparams("arbitrary", "arbitrary"),
        name="gqa_attn",
    )(q, k, v, kc, vc)


NA_RPB_ROWS = 2 * NA_ROWS


def _na_rpb_pairs(rpb):
    h = rpb.shape[0]
    t = jnp.pad(rpb.astype(F32), ((0, 0), (1, 1), (0, HEAD_DIM - rpb.shape[2])))
    pairs = jnp.concatenate([t[:, :-1], t[:, 1:]], axis=-1)
    assert pairs.shape == (h, NA_RPB_ROWS, PAIR)
    return jnp.roll(pairs, -(NA_COLS - 1), axis=-1)


def _na_build_bias(rpb_ref, bias_sc, kind, g, rows):
    w = GRID_W
    r0 = NA_QROWS * g
    start_row = jnp.clip(r0 - NA_ROWS // 2, 0, rows - NA_KROWS)
    qcol = lax.broadcasted_iota(jnp.int32, (w, PAIR), 0)
    lane = lax.broadcasted_iota(jnp.int32, (w, PAIR), 1)
    kcol = lane % w
    odd = (lane >= w).astype(jnp.int32)
    cs = jnp.clip(qcol - NA_COLS // 2, 0, w - NA_COLS)
    col_ok = (kcol >= cs) & (kcol < cs + NA_COLS)

    def body(t, carry):
        rr = t // (NA_KROWS // 2)
        a2 = t % (NA_KROWS // 2)
        r = r0 + rr
        rs = jnp.clip(r - NA_ROWS // 2, 0, rows - NA_ROWS)
        krow0 = start_row + 2 * a2
        krow = krow0 + odd
        ok = col_ok & (krow >= rs) & (krow < rs + NA_ROWS)
        e = jnp.clip(krow0 - r + NA_ROWS, 0, NA_RPB_ROWS - 1)
        for h in range(4):
            tile = jnp.broadcast_to(rpb_ref[h, pl.ds(e, 1), :], (w, PAIR))
            for bit in range(6):
                tile = jnp.where(((qcol >> bit) & 1) == 1, pltpu.roll(tile, 1 << bit, 1), tile)
            row0 = pl.multiple_of((h % 2) * NA_QROWS * w + rr * w, w)
            bias_sc[kind, h // 2, a2, pl.ds(row0, w), :] = jnp.where(ok, tile, NEG)
        return carry

    lax.fori_loop(0, NA_QROWS * (NA_KROWS // 2), body, 0)


def _na_kernel(q_ref, k_ref, v_ref, kc_ref, vc_ref, rpb_ref, o_ref, bias_sc, *, rows, ngrp):
    groups = rows // NA_QROWS
    tq = NA_QROWS * GRID_W
    span = NA_KROWS * GRID_W
    gs = [pl.program_id(1) * ngrp + j for j in range(ngrp)]
    kinds = [jnp.where(g == 0, 0, jnp.where(g == groups - 1, 2, 1)) for g in gs]
    for g, kind in zip(gs, kinds):
        @pl.when((pl.program_id(0) == 0) & ((g == 0) | (g == 1) | (g == groups - 1)))
        def _():
            _na_build_bias(rpb_ref, bias_sc, kind, g, rows)

    for j, (g, kind) in enumerate(zip(gs, kinds)):
        qrows = slice(j * tq, (j + 1) * tq)
        start = pl.multiple_of(jnp.clip(NA_QROWS * g - NA_ROWS // 2, 0, rows - NA_KROWS) * GRID_W, GRID_W)
        for p in range(2):
            lanes = slice(p * PAIR, (p + 1) * PAIR)
            qs = _stack_heads([q_ref[0, qrows, lanes]])
            kl = k_ref[0, pl.ds(start, span), lanes]
            vl = v_ref[0, pl.ds(start, span), lanes]
            bias = jnp.concatenate([bias_sc[kind, p, a2] for a2 in range(NA_KROWS // 2)], axis=1)
            s_nb = _dot_nt(qs, kl) + bias
            s_ctx = _dot_nt(qs, kc_ref[0, :, lanes])
            m = jnp.maximum(jnp.max(s_nb, axis=-1, keepdims=True), jnp.max(s_ctx, axis=-1, keepdims=True))
            p_nb = jnp.exp(s_nb - m)
            p_ctx = jnp.exp(s_ctx - m)
            denom = jnp.sum(p_nb, axis=-1, keepdims=True) + jnp.sum(p_ctx, axis=-1, keepdims=True)
            o = _dot(p_nb.astype(BF16), vl) + _dot(p_ctx.astype(BF16), vc_ref[0, :, lanes])
            o = o * (1.0 / denom)
            o_ref[0, qrows, lanes] = _unstack_heads(o, 1, tq)[0].astype(o_ref.dtype)


def _na(q, k, v, kc, vc, rpb_pairs, *, ngrp=8):
    b, s, _ = q.shape
    m = kc.shape[1]
    rows = s // GRID_W
    groups = rows // NA_QROWS
    tq = NA_QROWS * GRID_W
    assert rows % NA_QROWS == 0 and groups >= 4 and groups % ngrp == 0
    return pl.pallas_call(
        functools.partial(_na_kernel, rows=rows, ngrp=ngrp),
        grid=(b, groups // ngrp),
        in_specs=[
            pl.BlockSpec((1, ngrp * tq, GROUP_WIDTH), lambda i, g: (i, g, 0)),
            pl.BlockSpec((1, s, GROUP_WIDTH), lambda i, g: (i, 0, 0)),
            pl.BlockSpec((1, s, GROUP_WIDTH), lambda i, g: (i, 0, 0)),
            pl.BlockSpec((1, m, GROUP_WIDTH), lambda i, g: (i, 0, 0)),
            pl.BlockSpec((1, m, GROUP_WIDTH), lambda i, g: (i, 0, 0)),
            pl.BlockSpec((4, NA_RPB_ROWS, PAIR), lambda i, g: (0, 0, 0)),
        ],
        out_specs=pl.BlockSpec((1, ngrp * tq, GROUP_WIDTH), lambda i, g: (i, g, 0)),
        out_shape=jax.ShapeDtypeStruct((b, s, GROUP_WIDTH), BF16),
        scratch_shapes=[pltpu.VMEM((3, 2, NA_KROWS // 2, 2 * tq, PAIR), F32)],
        compiler_params=_cparams("arbitrary", "arbitrary"),
        name="na_attn",
    )(q, k, v, kc, vc, rpb_pairs)


def _ssd_kernel(uf_ref, ub_ref, dtf_ref, dtb_ref, dttf_ref, dttb_ref, dtbias_ref, alog_ref,
                dtbias_col_ref, alog_col_ref, dskip_ref, h0f_ref, h0b_ref,
                yf_ref, yb_ref, hf_ref, hb_ref, hf_sc, hb_sc, *, steps, group):
    i = pl.program_id(1)
    q = SSD_CHUNK

    @pl.when(i == 0)
    def _():
        hf_sc[...] = h0f_ref[0]
        hb_sc[...] = h0b_ref[0]

    ii = lax.broadcasted_iota(jnp.int32, (q, q), 0)
    jj = lax.broadcasted_iota(jnp.int32, (q, q), 1)
    lo_lanes = jj < HEAD_DIM
    a_coef = -jnp.exp(alog_ref[...])
    a_coef_col = -jnp.exp(alog_col_ref[...])

    def chunk(u, dt_raw, dt_raw_t, col0, reverse, h):
        causal = (jj >= ii) if reverse else (jj <= ii)
        dt = _softplus(dt_raw + dtbias_ref[...])
        dt_t = _softplus(dt_raw_t + dtbias_col_ref[...])
        tri = jnp.where(causal, 1.0, 0.0).astype(BF16)
        tri_t = jnp.where((ii >= jj) if reverse else (ii <= jj), 1.0, 0.0).astype(BF16)
        a_hi, a_mid, a_lo = _split3(dt * a_coef)
        cum = _dot(tri, a_hi) + _dot(tri, a_mid) + _dot(tri, a_lo)
        a_hi, a_mid, a_lo = _split3(dt_t * a_coef_col)
        cum_t = _dot(a_hi, tri_t) + _dot(a_mid, tri_t) + _dot(a_lo, tri_t)
        end = cum[0:1, :] if reverse else cum[q - 1:q, :]
        xs = u[:, :2 * PAIR]
        ys, h_out = [], []
        for k in range(2):
            bk = u[:, 2 * PAIR + k * SSD_STATE: 2 * PAIR + (k + 1) * SSD_STATE]
            ck = u[:, 2 * PAIR + 2 * SSD_STATE + k * SSD_STATE: 2 * PAIR + 2 * SSD_STATE + (k + 1) * SSD_STATE]
            xk_b = xs[:, k * PAIR:(k + 1) * PAIR]
            xk = xk_b.astype(F32)
            ck_b = ck
            cb = _dot_nt(ck_b, bk)
            c0 = col0 + 2 * k
            cols = []
            for r in range(2):
                c = c0 + r
                seg = cum[:, c:c + 1] - cum_t[c:c + 1, :]
                lmat = jnp.exp(jnp.where(causal, seg, NEG))
                att = (cb * lmat * dt_t[c:c + 1, :]).astype(BF16)
                cols.append(_dot(att, xk_b))
            y_intra = jnp.where(lo_lanes, cols[0], cols[1])
            e_in = jnp.where(lo_lanes, jnp.exp(cum[:, c0:c0 + 1]), jnp.exp(cum[:, c0 + 1:c0 + 2]))
            h_t = h[k]
            y_state = _dot(ck_b, h_t.astype(BF16)) * e_in
            w0 = jnp.exp(end[:, c0:c0 + 1] - cum[:, c0:c0 + 1]) * dt[:, c0:c0 + 1]
            w1 = jnp.exp(end[:, c0 + 1:c0 + 2] - cum[:, c0 + 1:c0 + 2]) * dt[:, c0 + 1:c0 + 2]
            xw = (xk * jnp.where(lo_lanes, w0, w1)).astype(BF16)
            st = _dot(bk.astype(F32).T.astype(BF16), xw)
            decay = jnp.where(lo_lanes[0:1], jnp.exp(end[:, c0:c0 + 1]), jnp.exp(end[:, c0 + 1:c0 + 2]))
            h_out.append(h_t * decay + st)
            ys.append(y_intra + y_state)
        return ys, xs, h_out

    dskip = dskip_ref[...]
    h = [hf_sc[0], hf_sc[1]]
    for j in range(group):
        rows = slice(j * q, (j + 1) * q)
        ys, xs, h = chunk(uf_ref[0, rows, :], dtf_ref[0, rows, :], dttf_ref[0, :, rows], 0, False, h)
        for k in range(2):
            lanes = slice(k * PAIR, (k + 1) * PAIR)
            yf_ref[0, rows, lanes] = ys[k] + dskip[:, lanes] * xs[:, lanes].astype(F32)
    hf_sc[0], hf_sc[1] = h

    h = [hb_sc[0], hb_sc[1]]
    for j in reversed(range(group)):
        rows = slice(j * q, (j + 1) * q)
        ys, _, h = chunk(ub_ref[0, rows, :], dtb_ref[0, rows, :], dttb_ref[0, :, rows], 4, True, h)
        for k in range(2):
            yb_ref[0, rows, k * PAIR:(k + 1) * PAIR] = ys[k]
    hb_sc[0], hb_sc[1] = h

    @pl.when(i == steps - 1)
    def _():
        hf_ref[0] = hf_sc[...]
        hb_ref[0] = hb_sc[...]


def _ssd(u, dt, dt_bias, a_log, dskip, h0f, h0b, *, group=8):
    b, length, ch = u.shape
    nc = length // SSD_CHUNK
    while nc % group:
        group //= 2
    steps = nc // group
    q = group * SSD_CHUNK
    dt_t = jnp.swapaxes(dt[:, :, :DT_ROWS], 1, 2)
    column = lambda v: jnp.broadcast_to(v[0, :DT_ROWS, None], (DT_ROWS, DT_PAD))
    fwd = lambda i, c: (i, c, 0)
    bwd = lambda i, c: (i, steps - 1 - c, 0)
    fwd_t = lambda i, c: (i, 0, c)
    bwd_t = lambda i, c: (i, 0, steps - 1 - c)
    const = lambda i, c: (0, 0)
    state = lambda i, c: (i, 0, 0, 0)
    state_shape = (b, 2, SSD_STATE, PAIR)
    return pl.pallas_call(
        functools.partial(_ssd_kernel, steps=steps, group=group),
        grid=(b, steps),
        in_specs=[
            pl.BlockSpec((1, q, ch), fwd),
            pl.BlockSpec((1, q, ch), bwd),
            pl.BlockSpec((1, q, DT_PAD), fwd),
            pl.BlockSpec((1, q, DT_PAD), bwd),
            pl.BlockSpec((1, DT_ROWS, q), fwd_t),
            pl.BlockSpec((1, DT_ROWS, q), bwd_t),
            pl.BlockSpec((1, DT_PAD), const),
            pl.BlockSpec((1, DT_PAD), const),
            pl.BlockSpec((DT_ROWS, DT_PAD), const),
            pl.BlockSpec((DT_ROWS, DT_PAD), const),
            pl.BlockSpec((1, GROUP_WIDTH), const),
            pl.BlockSpec((1, 2, SSD_STATE, PAIR), state),
            pl.BlockSpec((1, 2, SSD_STATE, PAIR), state),
        ],
        out_specs=[
            pl.BlockSpec((1, q, GROUP_WIDTH), fwd),
            pl.BlockSpec((1, q, GROUP_WIDTH), bwd),
            pl.BlockSpec((1, 2, SSD_STATE, PAIR), state),
            pl.BlockSpec((1, 2, SSD_STATE, PAIR), state),
        ],
        out_shape=[
            jax.ShapeDtypeStruct((b, length, GROUP_WIDTH), F32),
            jax.ShapeDtypeStruct((b, length, GROUP_WIDTH), F32),
            jax.ShapeDtypeStruct(state_shape, F32),
            jax.ShapeDtypeStruct(state_shape, F32),
        ],
        scratch_shapes=[pltpu.VMEM((2, SSD_STATE, PAIR), F32), pltpu.VMEM((2, SSD_STATE, PAIR), F32)],
        compiler_params=_cparams("arbitrary", "arbitrary"),
        name="ssd_scan",
    )(u, u, dt, dt, dt_t, dt_t, dt_bias, a_log, column(dt_bias), column(a_log), dskip, h0f, h0b)


def _head_blocks(start, perm):
    return [(start + h * HEAD_DIM, start + (h + 1) * HEAD_DIM) for h in perm]


def _take_blocks(a, axis, blocks):
    return jnp.concatenate([lax.slice_in_dim(a, lo, hi, axis=axis) for lo, hi in blocks], axis=axis)


_INPROJ_BLOCKS = (_head_blocks(0, HEAD_PERM) + [(256, 1536)] + _head_blocks(1544, HEAD_PERM)
                  + [(1800, 2824), (1536, 1544)])
_OUTPROJ_BLOCKS = (_head_blocks(0, HEAD_PERM) + [(256, 512)] + _head_blocks(512, HEAD_PERM)
                   + [(768, 1024)])


def _rope_tables(seq):
    t = np.arange(seq)
    quarter = HEAD_DIM // 4
    inv = ROPE_BASE ** (-jnp.arange(quarter, dtype=F32) / quarter)
    a_row = jnp.asarray(t // GRID_W, F32)[:, None] * inv
    a_col = jnp.asarray(t % GRID_W, F32)[:, None] * inv
    cos = jnp.concatenate([jnp.cos(a_row), jnp.cos(a_row), jnp.cos(a_col), jnp.cos(a_col)], axis=1)
    sin = jnp.concatenate([-jnp.sin(a_row), jnp.sin(a_row), -jnp.sin(a_col), jnp.sin(a_col)], axis=1)
    return jnp.tile(cos, (1, 2)), jnp.tile(sin, (1, 2))


def _pad_lanes(v, width):
    v = v.reshape(1, -1).astype(F32)
    return jnp.pad(v, ((0, 0), (0, width - v.shape[1])))


def _token_tile(n, cap):
    t = cap
    while n % t:
        t //= 2
    return t


def kernel(x, c, ctx, c_ctx, ada_w, ada_b, ln_g, ln_b, ffn1_w_in, ffn1_w_out, mix_w_in, mix_w_out,
           mix_norm_g, swa_sink, ssd_conv_w, ssd_conv_b, ssd_dt_bias, ssd_A_log, ssd_D,
           gqa_q_norm, gqa_k_norm, na_rpb, ffn2_w_in, ffn2_w_out):
    bsz, seq, d = x.shape
    m_ctx = ctx.shape[1]
    depth = ada_w.shape[0]
    alpha = float((2 * depth) ** 0.25)
    mod_rows = 16 * ((bsz + 1 + 15) // 16)
    cc = jnp.concatenate([c, c_ctx[None], jnp.zeros((mod_rows - bsz - 1, d), F32)], axis=0)
    mods = _ada(cc, ada_w, ada_b)

    cos_x, sin_x = _rope_tables(seq)
    tm_x = _token_tile(seq, 1024)
    tm_p = _token_tile(seq, 1024)
    tm_c = _token_tile(m_ctx, 256)
    cos_c = jnp.ones((tm_c, PAIR), F32)
    sin_c = jnp.zeros((tm_c, PAIR), F32)
    blockdiag = jnp.asarray(np.kron(np.eye(2), np.ones((HEAD_DIM, HEAD_DIM))), BF16)
    w1_in, w1_out = ffn1_w_in.astype(BF16), ffn1_w_out.astype(BF16)
    w2_in, w2_out = ffn2_w_in.astype(BF16), ffn2_w_out.astype(BF16)
    w_mix = _take_blocks(mix_w_in, 2, _INPROJ_BLOCKS).astype(BF16)
    w_mix = jnp.pad(w_mix, ((0, 0), (0, 0), (0, _P_END - w_mix.shape[2])))
    w_o = _take_blocks(mix_w_out, 1, _OUTPROJ_BLOCKS).astype(BF16)
    norm_g_all = _take_blocks(mix_norm_g, 1, _OUTPROJ_BLOCKS)

    xt = x.reshape(bsz * seq, d)
    ct = ctx.reshape(bsz * m_ctx, d)
    zero_state = jnp.zeros((bsz, 2, SSD_STATE, PAIR), F32)

    for l in range(depth):
        last = l == depth - 1
        mod_x = mods[l, :bsz].reshape(bsz, N_MOD, d)
        mod_c = mods[l, bsz:bsz + 1].reshape(1, N_MOD, d)
        norm_g = norm_g_all[l].reshape(1, d)
        qg = jnp.tile(gqa_q_norm[l], 2).reshape(1, PAIR)
        kg = jnp.tile(gqa_k_norm[l], 2).reshape(1, PAIR)
        conv_w = jnp.pad(ssd_conv_w[l].reshape(SSD_CONV, SSD_CONV_CH), ((0, 8 - SSD_CONV), (0, 0)))
        conv_b = ssd_conv_b[l].reshape(1, SSD_CONV_CH)
        dt_bias = _pad_lanes(ssd_dt_bias[l], DT_PAD)
        a_log = _pad_lanes(ssd_A_log[l], DT_PAD)
        dskip = jnp.repeat(ssd_D[l], HEAD_DIM).reshape(1, GROUP_WIDTH)
        sink = swa_sink[l].astype(F32)
        rpb_pairs = _na_rpb_pairs(na_rpb[l])

        ffn_x = functools.partial(_ffn, tokens_per_mod=seq, alpha=alpha, tm=tm_x)
        ffn_c = functools.partial(_ffn, tokens_per_mod=bsz * m_ctx, alpha=alpha, tm=tm_c)

        xt = ffn_x(xt, mod_x, (0, 1, 2), w1_in, w1_out, l, ln_g[l, 0], ln_b[l, 0])
        ct = ffn_c(ct, mod_c, (0, 1, 2), w1_in, w1_out, l, ln_g[l, 0], ln_b[l, 0])

        px = _inproj(xt, mod_x, w_mix, l, cos_x, sin_x, qg, kg, blockdiag, conv_w, conv_b,
                     tokens_per_mod=seq, seq_len=seq, tm=tm_p, gqa_q_scale=Q_SCALE * LOG2E)
        pc = _inproj(ct, mod_c, w_mix, l, cos_c, sin_c, qg, kg, blockdiag, conv_w, conv_b,
                     tokens_per_mod=bsz * m_ctx, seq_len=m_ctx, tm=tm_c, gqa_q_scale=Q_SCALE)
        aq, ak, av, bz, bu, bdt, cq, ck, cv, dq, dk, dv = [
            t.reshape(bsz, seq, t.shape[-1]) for t in px]
        aq_c, ak_c, av_c, bz_c, bu_c, bdt_c, cq_c, ck_c, cv_c, dq_c, dk_c, dv_c = [
            t.reshape(bsz, m_ctx, t.shape[-1]) for t in pc]

        ssd = functools.partial(_ssd, dt_bias=dt_bias, a_log=a_log, dskip=dskip)
        yf_c, yb_c, hf_c, hb_c = ssd(bu_c, bdt_c, h0f=zero_state, h0b=zero_state)
        yf, yb, _, _ = ssd(bu, bdt, h0f=hf_c, h0b=hb_c)

        oa = _swa(sink, aq, ak, av, ak_c, av_c)
        oc = _gqa(cq, ck, cv, ck_c, cv_c)
        od = _na(dq, dk, dv, dk_c, dv_c, rpb_pairs)

        flat = lambda t: t.reshape(-1, t.shape[-1])
        xt = _mixer_ffn(xt, mod_x, flat(oa), flat(yf), flat(yb), flat(bz), flat(oc), flat(od), norm_g, w_o,
                        ln_g[l, 1], ln_b[l, 1], w2_in, w2_out, l, ln_g[l, 2], ln_b[l, 2],
                        tokens_per_mod=seq, alpha=alpha, tm=tm_x)

        if not last:
            tq_c = _token_tile(m_ctx, 128)
            oa_c = _dense_attn(aq_c, [(ak_c, av_c)], kv_pairs=1, sink=sink, tq=tq_c)
            oc_c = _dense_attn(cq_c, [(ck_c, cv_c)], kv_pairs=1, tq=tq_c)
            od_c = _dense_attn(dq_c, [(dk_c, dv_c)], kv_pairs=2, tq=tq_c)
            ct = _mixer_ffn(ct, mod_c, flat(oa_c), flat(yf_c), flat(yb_c), flat(bz_c), flat(oc_c),
                            flat(od_c), norm_g, w_o, ln_g[l, 1], ln_b[l, 1], w2_in, w2_out, l,
                            ln_g[l, 2], ln_b[l, 2], tokens_per_mod=bsz * m_ctx, alpha=alpha, tm=tm_c)

    return xt.reshape(bsz, seq, d)
```

```python
import functools

import numpy as np
import jax
import jax.numpy as jnp
from jax import lax
from jax.experimental import pallas as pl
from jax.experimental.pallas import tpu as pltpu

F32 = jnp.float32
BF16 = jnp.bfloat16

HEAD_DIM = 64
PAIR = 2 * HEAD_DIM
GROUP_WIDTH = 256
GRID_W = 64
N_MOD = 9
SWA_WINDOW = 128
SWA_BLOCK = 128
SSD_CHUNK = 128
SSD_STATE = 128
SSD_CONV = 5
CONV_HALO = 8
FFN_SUBTILE = 256
SSD_CONV_CH = 768
NA_ROWS = 8
NA_COLS = 16
NA_QROWS = 4
NA_KROWS = 12
ROPE_BASE = 10000.0
EPS = 1e-5
NEG = -1e30
Q_SCALE = HEAD_DIM ** -0.5
LOG2E = 1.4426950408889634
DT_PAD = 128
DT_ROWS = 16
HEAD_PERM = (0, 2, 1, 3)

VMEM_LIMIT_BYTES = 56 * 1024 * 1024


def _cparams(*sem):
    return pltpu.CompilerParams(dimension_semantics=sem, vmem_limit_bytes=VMEM_LIMIT_BYTES)


def _dot(a, b):
    return jnp.dot(a, b, preferred_element_type=F32)


def _dot_nt(a, b):
    return lax.dot_general(a, b, (((1,), (1,)), ((), ())), preferred_element_type=F32)


def _sigmoid(x):
    return 1.0 / (1.0 + jnp.exp(-x))


def _silu(x):
    return x * _sigmoid(x)


def _softplus(x):
    return jnp.maximum(x, 0.0) + jnp.log(1.0 + jnp.exp(-jnp.abs(x)))


def _split3(a):
    hi = a.astype(BF16)
    r1 = a - hi.astype(F32)
    mid = r1.astype(BF16)
    lo = (r1 - mid.astype(F32)).astype(BF16)
    return hi, mid, lo


def _layer_norm(y, g, b):
    mu = jnp.mean(y, axis=-1, keepdims=True)
    d = y - mu
    var = jnp.mean(d * d, axis=-1, keepdims=True)
    return d * lax.rsqrt(var + EPS) * g + b


def _ada_kernel(c_ref, w_ref, b_ref, o_ref):
    s = _silu(c_ref[...])
    s_hi = s.astype(BF16)
    s_lo = (s - s_hi.astype(F32)).astype(BF16)
    w = w_ref[0]
    w_hi = w.astype(BF16)
    w_lo = (w - w_hi.astype(F32)).astype(BF16)
    o_ref[0] = _dot(s_hi, w_hi) + _dot(s_lo, w_hi) + _dot(s_hi, w_lo) + b_ref[0]


def _ada(cc, ada_w, ada_b):
    depth, d, n = ada_w.shape
    rows = cc.shape[0]
    tn = 1024
    return pl.pallas_call(
        _ada_kernel,
        grid=(depth, n // tn),
        in_specs=[
            pl.BlockSpec((rows, d), lambda l, j: (0, 0)),
            pl.BlockSpec((1, d, tn), lambda l, j: (l, 0, j)),
            pl.BlockSpec((1, 1, tn), lambda l, j: (l, 0, j)),
        ],
        out_specs=pl.BlockSpec((1, rows, tn), lambda l, j: (l, 0, j)),
        out_shape=jax.ShapeDtypeStruct((depth, rows, n), F32),
        compiler_params=_cparams("arbitrary", "arbitrary"),
        name="ada_mod",
    )(cc, ada_w, ada_b.reshape(depth, 1, n))


def _ffn_rows(x, mod_ref, rows, wa_ref, wu_ref, wo_ref, g, b, alpha):
    r_shift, r_scale, r_gate = rows
    h = x * (1.0 + mod_ref[0, r_scale:r_scale + 1, :]) + mod_ref[0, r_shift:r_shift + 1, :]
    h = h.astype(BF16)
    a = _dot(h, wa_ref[...])
    u = _dot(h, wu_ref[...])
    gated = (_silu(a) * u).astype(BF16)
    f = _dot(gated, wo_ref[...])
    y = alpha * x + (0.5 * mod_ref[0, r_gate:r_gate + 1, :]) * f
    return _layer_norm(y, g, b)


def _mixer_rows(x, mod_ref, parts, ng_ref, w_ref, g, b, alpha):
    acc = None
    for gi, y in enumerate(parts):
        rows = slice(gi * GROUP_WIDTH, (gi + 1) * GROUP_WIDTH)
        ms = jnp.mean(y * y, axis=-1, keepdims=True)
        yn = (y * lax.rsqrt(ms + EPS) * ng_ref[:, rows]).astype(BF16)
        term = _dot(yn, w_ref[rows, :])
        acc = term if acc is None else acc + term
    return _layer_norm(alpha * x + mod_ref[0, 5:6, :] * acc, g, b)


def _subtiles(tm):
    sub = FFN_SUBTILE if tm % FFN_SUBTILE == 0 else tm
    return [slice(r0, r0 + sub) for r0 in range(0, tm, sub)]


def _ffn_kernel(x_ref, mod_ref, wa_ref, wu_ref, wo_ref, g_ref, b_ref, o_ref, *, rows, alpha):
    for r in _subtiles(x_ref.shape[0]):
        o_ref[r, :] = _ffn_rows(x_ref[r, :], mod_ref, rows, wa_ref, wu_ref, wo_ref,
                                g_ref[...], b_ref[...], alpha)


def _mixer_ffn_kernel(x_ref, mod_ref, oa_ref, yf_ref, yb_ref, z_ref, oc_ref, od_ref, ng_ref, wm_ref,
                      g1_ref, b1_ref, wa_ref, wu_ref, wo_ref, g2_ref, b2_ref, o_ref, *, alpha):
    for r in _subtiles(x_ref.shape[0]):
        ob = (yf_ref[r, :] + yb_ref[r, :]) * _silu(z_ref[r, :].astype(F32))
        parts = (oa_ref[r, :].astype(F32), ob, oc_ref[r, :].astype(F32), od_ref[r, :].astype(F32))
        x_mid = _mixer_rows(x_ref[r, :], mod_ref, parts, ng_ref, wm_ref, g1_ref[...], b1_ref[...], alpha)
        o_ref[r, :] = _ffn_rows(x_mid, mod_ref, (6, 7, 8), wa_ref, wu_ref, wo_ref,
                                g2_ref[...], b2_ref[...], alpha)


def _layer_weight(shape, layer, col_block=0):
    return pl.BlockSpec((None,) + shape, lambda i: (layer, 0, col_block), pipeline_mode=pl.Buffered(1))


def _ffn(xt, mod, rows, w_in, w_out, layer, ln_g, ln_b, *, tokens_per_mod, alpha, tm):
    t, d = xt.shape
    dff = w_out.shape[1]
    tiles_per_mod = tokens_per_mod // tm
    return pl.pallas_call(
        functools.partial(_ffn_kernel, rows=rows, alpha=alpha),
        grid=(t // tm,),
        in_specs=[
            pl.BlockSpec((tm, d), lambda i: (i, 0)),
            pl.BlockSpec((1, N_MOD, d), lambda i: (i // tiles_per_mod, 0, 0)),
            _layer_weight((d, dff), layer, 0),
            _layer_weight((d, dff), layer, 1),
            _layer_weight((dff, d), layer),
            pl.BlockSpec((1, d), lambda i: (0, 0)),
            pl.BlockSpec((1, d), lambda i: (0, 0)),
        ],
        out_specs=pl.BlockSpec((tm, d), lambda i: (i, 0)),
        out_shape=jax.ShapeDtypeStruct((t, d), F32),
        compiler_params=_cparams("arbitrary"),
        name="ffn",
    )(xt, mod, w_in, w_in, w_out, ln_g.reshape(1, d), ln_b.reshape(1, d))


def _mixer_ffn(xt, mod, oa, yf, yb, z, oc, od, norm_g, w_mix_out, ln1_g, ln1_b, w_in, w_out, layer,
               ln2_g, ln2_b, *, tokens_per_mod, alpha, tm):
    t, d = xt.shape
    dff = w_out.shape[1]
    tiles_per_mod = tokens_per_mod // tm
    row = lambda i: (i, 0)
    const = lambda i: (0, 0)
    grp = pl.BlockSpec((tm, GROUP_WIDTH), row)
    vec = pl.BlockSpec((1, d), const)
    return pl.pallas_call(
        functools.partial(_mixer_ffn_kernel, alpha=alpha),
        grid=(t // tm,),
        in_specs=[
            pl.BlockSpec((tm, d), row),
            pl.BlockSpec((1, N_MOD, d), lambda i: (i // tiles_per_mod, 0, 0)),
            grp, grp, grp, grp, grp, grp,
            vec,
            _layer_weight((d, d), layer),
            vec, vec,
            _layer_weight((d, dff), layer, 0),
            _layer_weight((d, dff), layer, 1),
            _layer_weight((dff, d), layer),
            vec, vec,
        ],
        out_specs=pl.BlockSpec((tm, d), row),
        out_shape=jax.ShapeDtypeStruct((t, d), F32),
        compiler_params=_cparams("arbitrary"),
        name="mixer_ffn",
    )(xt, mod, oa, yf, yb, z, oc, od, norm_g, w_mix_out, ln1_g.reshape(1, d), ln1_b.reshape(1, d),
      w_in, w_in, w_out, ln2_g.reshape(1, d), ln2_b.reshape(1, d))


_P_AQ, _P_AK, _P_AV, _P_Z, _P_XBC = 0, 256, 384, 512, 768
_P_CQ, _P_CK, _P_CV, _P_DQ, _P_DK, _P_DV, _P_DT, _P_END = 1536, 1792, 1920, 2048, 2304, 2560, 2816, 2944


def _rope_pair(t, cos, sin):
    lane = lax.broadcasted_iota(jnp.int32, t.shape, 1)
    up = pltpu.roll(t, HEAD_DIM // 4, 1)
    dn = pltpu.roll(t, PAIR - HEAD_DIM // 4, 1)
    partner = jnp.where((lane % 32) < 16, dn, up)
    return t * cos + partner * sin


def _rms_pair(t, g, blockdiag):
    sq = t * t
    hi = sq.astype(BF16)
    lo = (sq - hi.astype(F32)).astype(BF16)
    ss = _dot(hi, blockdiag) + _dot(lo, blockdiag)
    return t * lax.rsqrt(ss * (1.0 / HEAD_DIM) + EPS) * g


def _inproj_kernel(x_ref, xp_ref, xn_ref, mod_ref, w_ref, cos_ref, sin_ref, qg_ref, kg_ref, bd_ref,
                   cw_ref, cb_ref,
                   aq_ref, ak_ref, av_ref, z_ref, u_ref, dt_ref,
                   cq_ref, ck_ref, cv_ref, dq_ref, dk_ref, dv_ref, ext_sc, *, gqa_q_scale, tiles_per_seq):
    tm = x_ref.shape[0]
    halo = xp_ref.shape[0]
    pos = pl.program_id(0) % tiles_per_seq
    x_ext = jnp.concatenate([xp_ref[...], x_ref[...], xn_ref[...]], axis=0)
    h = (x_ext * (1.0 + mod_ref[0, 4:5, :]) + mod_ref[0, 3:4, :]).astype(BF16)
    y_ext = _dot(h, w_ref[...])
    y = y_ext[halo:halo + tm]
    cos = cos_ref[...]
    sin = sin_ref[...]
    bd = bd_ref[...]

    def pairs(lo, n):
        return [y[:, lo + PAIR * p: lo + PAIR * (p + 1)] for p in range(n)]

    xbc = y_ext[:, _P_XBC:_P_XBC + SSD_CONV_CH]
    ext_sc[:halo] = jnp.where(pos == 0, 0.0, xbc[:halo])
    ext_sc[halo:halo + tm] = xbc[halo:halo + tm]
    ext_sc[halo + tm:] = jnp.where(pos == tiles_per_seq - 1, 0.0, xbc[halo + tm:])
    conv = cb_ref[...]
    for k in range(SSD_CONV):
        lo = halo - SSD_CONV // 2 + k
        conv = conv + cw_ref[k:k + 1, :] * ext_sc[lo:lo + tm, :]
    u_ref[...] = _silu(conv).astype(u_ref.dtype)

    def put(ref, parts, scale=None):
        for p, part in enumerate(parts):
            if scale is not None:
                part = part * scale
            ref[:, PAIR * p: PAIR * (p + 1)] = part.astype(ref.dtype)

    swa = pairs(_P_AQ, 4)
    put(aq_ref, [_rope_pair(t, cos, sin) for t in swa[:2]], gqa_q_scale)
    put(ak_ref, [_rope_pair(swa[2], cos, sin)])
    put(av_ref, [swa[3]])
    put(z_ref, pairs(_P_Z, 2))
    put(dt_ref, pairs(_P_DT, 1))
    gqa = pairs(_P_CQ, 4)
    put(cq_ref, [_rope_pair(_rms_pair(t, qg_ref[...], bd), cos, sin) for t in gqa[:2]], gqa_q_scale)
    put(ck_ref, [_rope_pair(_rms_pair(gqa[2], kg_ref[...], bd), cos, sin)])
    put(cv_ref, [gqa[3]])
    na = pairs(_P_DQ, 6)
    put(dq_ref, na[:2], gqa_q_scale)
    put(dk_ref, na[2:4])
    put(dv_ref, na[4:])


def _inproj(xt, mod, w, layer, cos, sin, qg, kg, blockdiag, conv_w, conv_b, *, tokens_per_mod, seq_len, tm,
            gqa_q_scale):
    t, d = xt.shape
    tiles_per_mod = tokens_per_mod // tm
    table_tiles = cos.shape[0] // tm
    widths = (256, 128, 128, 256, SSD_CONV_CH, DT_PAD, 256, 128, 128, 256, 256, 256)
    dtypes = (BF16,) * 5 + (F32,) + (BF16,) * 6
    per = tm // CONV_HALO
    last_halo = t // CONV_HALO - 1
    row = lambda i: (i, 0)
    const = lambda i: (0, 0)
    return pl.pallas_call(
        functools.partial(_inproj_kernel, gqa_q_scale=gqa_q_scale, tiles_per_seq=seq_len // tm),
        grid=(t // tm,),
        in_specs=[
            pl.BlockSpec((tm, d), row),
            pl.BlockSpec((CONV_HALO, d), lambda i: (jnp.maximum(i * per - 1, 0), 0)),
            pl.BlockSpec((CONV_HALO, d), lambda i: (jnp.minimum((i + 1) * per, last_halo), 0)),
            pl.BlockSpec((1, N_MOD, d), lambda i: (i // tiles_per_mod, 0, 0)),
            _layer_weight((d, _P_END), layer),
            pl.BlockSpec((tm, PAIR), lambda i: (i % table_tiles, 0)),
            pl.BlockSpec((tm, PAIR), lambda i: (i % table_tiles, 0)),
            pl.BlockSpec((1, PAIR), const),
            pl.BlockSpec((1, PAIR), const),
            pl.BlockSpec((PAIR, PAIR), const),
            pl.BlockSpec((8, SSD_CONV_CH), const),
            pl.BlockSpec((1, SSD_CONV_CH), const),
        ],
        out_specs=[pl.BlockSpec((tm, wd), row) for wd in widths],
        out_shape=[jax.ShapeDtypeStruct((t, wd), dt) for wd, dt in zip(widths, dtypes)],
        scratch_shapes=[pltpu.VMEM((tm + 2 * CONV_HALO, SSD_CONV_CH), F32)],
        compiler_params=_cparams("arbitrary"),
        name="inproj",
    )(xt, xt, xt, mod, w, cos, sin, qg, kg, blockdiag, conv_w, conv_b)


def _stack_heads(q_pairs):
    lane = lax.broadcasted_iota(jnp.int32, q_pairs[0].shape, 1)
    lo = lane < HEAD_DIM
    zero = jnp.zeros_like(q_pairs[0])
    blocks = []
    for qp in q_pairs:
        blocks.append(jnp.where(lo, qp, zero))
        blocks.append(jnp.where(lo, zero, qp))
    return jnp.concatenate(blocks, axis=0)


def _unstack_heads(o, n_pairs, tq):
    lane = lax.broadcasted_iota(jnp.int32, (tq, PAIR), 1)
    lo = lane < HEAD_DIM
    return [jnp.where(lo, o[2 * p * tq:(2 * p + 1) * tq], o[(2 * p + 1) * tq:(2 * p + 2) * tq])
            for p in range(n_pairs)]


def _sink_column(sink_ref, tq):
    blk = lax.broadcasted_iota(jnp.int32, (4 * tq, 1), 0) // tq
    col = jnp.full((4 * tq, 1), sink_ref[HEAD_PERM[3]], F32)
    for b in (2, 1, 0):
        col = jnp.where(blk == b, sink_ref[HEAD_PERM[b]], col)
    return col


def _swa_kernel(sink_ref, q_ref, k_ref, v_ref, kc_ref, vc_ref, o_ref, *, seq, nblk):
    blk = SWA_BLOCK
    span = 3 * blk
    sk = _sink_column(sink_ref, blk) * LOG2E
    for j in range(nblk):
        n = pl.program_id(1) * nblk + j
        rows = slice(j * blk, (j + 1) * blk)
        start = pl.multiple_of(jnp.clip((n - 1) * blk, 0, seq - span), blk)
        kl = k_ref[0, pl.ds(start, span), :]
        vl = v_ref[0, pl.ds(start, span), :]
        q = q_ref[0, rows, :]
        qs = _stack_heads([q[:, :PAIR], q[:, PAIR:]])
        s_loc = _dot_nt(qs, kl)
        s_ctx = _dot_nt(qs, kc_ref[0])
        qpos = n * blk + lax.broadcasted_iota(jnp.int32, s_loc.shape, 0) % blk
        kpos = start + lax.broadcasted_iota(jnp.int32, s_loc.shape, 1)
        s_loc = jnp.where(jnp.abs(kpos - qpos) <= SWA_WINDOW, s_loc, NEG)
        m = jnp.maximum(jnp.maximum(jnp.max(s_loc, axis=-1, keepdims=True),
                                    jnp.max(s_ctx, axis=-1, keepdims=True)), sk)
        p_loc = jnp.exp2(s_loc - m)
        p_ctx = jnp.exp2(s_ctx - m)
        denom = (jnp.sum(p_loc, axis=-1, keepdims=True) + jnp.sum(p_ctx, axis=-1, keepdims=True)
                 + jnp.exp2(sk - m))
        o = _dot(p_loc.astype(BF16), vl) + _dot(p_ctx.astype(BF16), vc_ref[0])
        o = o * (1.0 / denom)
        o_a, o_b = _unstack_heads(o, 2, blk)
        o_ref[0, rows, :PAIR] = o_a.astype(o_ref.dtype)
        o_ref[0, rows, PAIR:] = o_b.astype(o_ref.dtype)


def _swa(sink, q, k, v, kc, vc, *, nblk=8):
    b, s, _ = q.shape
    m = kc.shape[1]
    blk = SWA_BLOCK * nblk
    assert s % blk == 0 and s >= 3 * SWA_BLOCK
    return pl.pallas_call(
        functools.partial(_swa_kernel, seq=s, nblk=nblk),
        grid=(b, s // blk),
        in_specs=[
            pl.BlockSpec(memory_space=pltpu.SMEM),
            pl.BlockSpec((1, blk, GROUP_WIDTH), lambda i, n: (i, n, 0)),
            pl.BlockSpec((1, s, PAIR), lambda i, n: (i, 0, 0)),
            pl.BlockSpec((1, s, PAIR), lambda i, n: (i, 0, 0)),
            pl.BlockSpec((1, m, PAIR), lambda i, n: (i, 0, 0)),
            pl.BlockSpec((1, m, PAIR), lambda i, n: (i, 0, 0)),
        ],
        out_specs=pl.BlockSpec((1, blk, GROUP_WIDTH), lambda i, n: (i, n, 0)),
        out_shape=jax.ShapeDtypeStruct((b, s, GROUP_WIDTH), BF16),
        compiler_params=_cparams("arbitrary", "arbitrary"),
        name="swa_attn",
    )(sink, q, k, v, kc, vc)


def _dense_attn_kernel(*refs, kv_pairs, n_src, has_sink, tq, tk):
    refs = list(refs)
    sink_ref = refs.pop(0) if has_sink else None
    q_ref = refs.pop(0)
    srcs = [(refs[2 * i], refs[2 * i + 1]) for i in range(n_src)]
    o_ref = refs[2 * n_src]
    q = q_ref[0]
    if kv_pairs == 1:
        units = [([q[:, :PAIR], q[:, PAIR:]], 0)]
    else:
        units = [([q[:, :PAIR]], 0), ([q[:, PAIR:]], 1)]
    outs = []
    for q_pairs, kv in units:
        qs = _stack_heads(q_pairs)
        nrow = qs.shape[0]
        if has_sink:
            m = _sink_column(sink_ref, tq)
            l = jnp.ones((nrow, 1), F32)
        else:
            m = jnp.full((nrow, 1), NEG, F32)
            l = jnp.zeros((nrow, 1), F32)
        acc = jnp.zeros((nrow, PAIR), F32)
        for k_ref, v_ref in srcs:
            nk = k_ref.shape[1]
            step = min(tk, nk)
            for c in range(nk // step):
                kch = k_ref[0, c * step:(c + 1) * step, kv * PAIR:(kv + 1) * PAIR]
                vch = v_ref[0, c * step:(c + 1) * step, kv * PAIR:(kv + 1) * PAIR]
                s = _dot_nt(qs, kch)
                m_new = jnp.maximum(m, jnp.max(s, axis=-1, keepdims=True))
                a = jnp.exp(m - m_new)
                p = jnp.exp(s - m_new)
                l = a * l + jnp.sum(p, axis=-1, keepdims=True)
                acc = a * acc + _dot(p.astype(BF16), vch)
                m = m_new
        o = acc * (1.0 / l)
        outs += _unstack_heads(o, len(q_pairs), tq)
    o_ref[0, :, :PAIR] = outs[0].astype(o_ref.dtype)
    o_ref[0, :, PAIR:] = outs[1].astype(o_ref.dtype)


def _dense_attn(q, srcs, *, kv_pairs, sink=None, tq=128, tk=512):
    b, s, _ = q.shape
    kvw = kv_pairs * PAIR
    has_sink = sink is not None
    in_specs, args = [], []
    if has_sink:
        in_specs.append(pl.BlockSpec(memory_space=pltpu.SMEM))
        args.append(sink)
    in_specs.append(pl.BlockSpec((1, tq, GROUP_WIDTH), lambda i, n: (i, n, 0)))
    args.append(q)
    for k, v in srcs:
        nk = k.shape[1]
        assert nk % min(tk, nk) == 0
        in_specs += [pl.BlockSpec((1, nk, kvw), lambda i, n: (i, 0, 0))] * 2
        args += [k, v]
    return pl.pallas_call(
        functools.partial(_dense_attn_kernel, kv_pairs=kv_pairs, n_src=len(srcs),
                          has_sink=has_sink, tq=tq, tk=tk),
        grid=(b, s // tq),
        in_specs=in_specs,
        out_specs=pl.BlockSpec((1, tq, GROUP_WIDTH), lambda i, n: (i, n, 0)),
        out_shape=jax.ShapeDtypeStruct((b, s, GROUP_WIDTH), BF16),
        compiler_params=_cparams("arbitrary", "arbitrary"),
        name="dense_attn",
    )(*args)


GQA_EXTRA_ROWS = 16


def _gqa_kernel(q_ref, k_ref, v_ref, kc_ref, vc_ref, o_ref, vt_sc, *, tq, tk):
    n = pl.program_id(1)
    seq = k_ref.shape[1]
    m_ctx = kc_ref.shape[1]

    @pl.when(n == 0)
    def _():
        for c in range(seq // PAIR):
            rows = slice(c * PAIR, (c + 1) * PAIR)
            vt_sc[:PAIR, rows] = v_ref[0, rows, :].astype(F32).T.astype(BF16)
        for c in range(m_ctx // PAIR):
            rows = slice(c * PAIR, (c + 1) * PAIR)
            vt_sc[:PAIR, seq + c * PAIR: seq + (c + 1) * PAIR] = vc_ref[0, rows, :].astype(F32).T.astype(BF16)
        ones_row = lax.broadcasted_iota(jnp.int32, (GQA_EXTRA_ROWS, seq + m_ctx), 0) == 0
        vt_sc[PAIR:, :] = jnp.where(ones_row, 1.0, 0.0).astype(BF16)

    q = q_ref[0]
    qs = _stack_heads([q[:, :PAIR], q[:, PAIR:]])
    chunks = [(k_ref, c * tk, tk, c * tk) for c in range(seq // tk)]
    chunks += [(kc_ref, c * min(tk, m_ctx), min(tk, m_ctx), seq + c * min(tk, m_ctx))
               for c in range(m_ctx // min(tk, m_ctx))]
    nrow = 4 * tq
    m = jnp.full((1, nrow), NEG, F32)
    acc = jnp.zeros((PAIR + GQA_EXTRA_ROWS, nrow), F32)
    for ref, lo, size, col in chunks:
        s_t = _dot_nt(ref[0, lo:lo + size, :], qs)
        m_new = jnp.maximum(m, jnp.max(s_t, axis=0, keepdims=True))
        p_t = jnp.exp2((s_t - m_new).astype(BF16))
        acc = jnp.exp2(m - m_new) * acc + _dot(vt_sc[:, col:col + size], p_t)
        m = m_new
    o_t = acc[:PAIR] * (1.0 / acc[PAIR:PAIR + 1])
    lo_rows = lax.broadcasted_iota(jnp.int32, (PAIR, tq), 0) < HEAD_DIM
    for p in range(2):
        pair_t = jnp.where(lo_rows, o_t[:, 2 * p * tq:(2 * p + 1) * tq],
                           o_t[:, (2 * p + 1) * tq:(2 * p + 2) * tq])
        o_ref[0, :, p * PAIR:(p + 1) * PAIR] = pair_t.T.astype(o_ref.dtype)


def _gqa(q, k, v, kc, vc, *, tq=1024, tk=1024):
    b, s, _ = q.shape
    m = kc.shape[1]
    tq = _token_tile(s, tq)
    assert s % tk == 0 and s % PAIR == 0 and m % PAIR == 0 and m % min(tk, m) == 0
    full = lambda i, n: (i, 0, 0)
    return pl.pallas_call(
        functools.partial(_gqa_kernel, tq=tq, tk=tk),
        grid=(b, s // tq),
        in_specs=[
            pl.BlockSpec((1, tq, GROUP_WIDTH), lambda i, n: (i, n, 0)),
            pl.BlockSpec((1, s, PAIR), full),
            pl.BlockSpec((1, s, PAIR), full),
            pl.BlockSpec((1, m, PAIR), full),
            pl.BlockSpec((1, m, PAIR), full),
        ],
        out_specs=pl.BlockSpec((1, tq, GROUP_WIDTH), lambda i, n: (i, n, 0)),
        out_shape=jax.ShapeDtypeStruct((b, s, GROUP_WIDTH), BF16),
        scratch_shapes=[pltpu.VMEM((PAIR + GQA_EXTRA_ROWS, s + m), BF16)],
        compiler_params=_cparams("arbitrary", "arbitrary"),
        name="gqa_attn",
    )(q, k, v, kc, vc)


NA_RPB_ROWS = 2 * NA_ROWS


def _na_rpb_pairs(rpb):
    h = rpb.shape[0]
    t = jnp.pad(rpb.astype(F32), ((0, 0), (1, 1), (0, HEAD_DIM - rpb.shape[2])))
    pairs = jnp.concatenate([t[:, :-1], t[:, 1:]], axis=-1)
    assert pairs.shape == (h, NA_RPB_ROWS, PAIR)
    return jnp.roll(pairs, -(NA_COLS - 1), axis=-1)


def _na_build_bias(rpb_ref, bias_sc, kind, g, rows):
    w = GRID_W
    r0 = NA_QROWS * g
    start_row = jnp.clip(r0 - NA_ROWS // 2, 0, rows - NA_KROWS)
    qcol = lax.broadcasted_iota(jnp.int32, (w, PAIR), 0)
    lane = lax.broadcasted_iota(jnp.int32, (w, PAIR), 1)
    kcol = lane % w
    odd = (lane >= w).astype(jnp.int32)
    cs = jnp.clip(qcol - NA_COLS // 2, 0, w - NA_COLS)
    col_ok = (kcol >= cs) & (kcol < cs + NA_COLS)

    def body(t, carry):
        rr = t // (NA_KROWS // 2)
        a2 = t % (NA_KROWS // 2)
        r = r0 + rr
        rs = jnp.clip(r - NA_ROWS // 2, 0, rows - NA_ROWS)
        krow0 = start_row + 2 * a2
        krow = krow0 + odd
        ok = col_ok & (krow >= rs) & (krow < rs + NA_ROWS)
        e = jnp.clip(krow0 - r + NA_ROWS, 0, NA_RPB_ROWS - 1)
        for h in range(4):
            tile = jnp.broadcast_to(rpb_ref[h, pl.ds(e, 1), :], (w, PAIR))
            for bit in range(6):
                tile = jnp.where(((qcol >> bit) & 1) == 1, pltpu.roll(tile, 1 << bit, 1), tile)
            row0 = pl.multiple_of((h % 2) * NA_QROWS * w + rr * w, w)
            bias_sc[kind, h // 2, a2, pl.ds(row0, w), :] = jnp.where(ok, tile, NEG)
        return carry

    lax.fori_loop(0, NA_QROWS * (NA_KROWS // 2), body, 0)


def _na_kernel(q_ref, k_ref, v_ref, kc_ref, vc_ref, rpb_ref, o_ref, bias_sc, *, rows, ngrp):
    groups = rows // NA_QROWS
    tq = NA_QROWS * GRID_W
    span = NA_KROWS * GRID_W
    gs = [pl.program_id(1) * ngrp + j for j in range(ngrp)]
    kinds = [jnp.where(g == 0, 0, jnp.where(g == groups - 1, 2, 1)) for g in gs]
    for g, kind in zip(gs, kinds):
        @pl.when((pl.program_id(0) == 0) & ((g == 0) | (g == 1) | (g == groups - 1)))
        def _():
            _na_build_bias(rpb_ref, bias_sc, kind, g, rows)

    for j, (g, kind) in enumerate(zip(gs, kinds)):
        qrows = slice(j * tq, (j + 1) * tq)
        start = pl.multiple_of(jnp.clip(NA_QROWS * g - NA_ROWS // 2, 0, rows - NA_KROWS) * GRID_W, GRID_W)
        for p in range(2):
            lanes = slice(p * PAIR, (p + 1) * PAIR)
            qs = _stack_heads([q_ref[0, qrows, lanes]])
            kl = k_ref[0, pl.ds(start, span), lanes]
            vl = v_ref[0, pl.ds(start, span), lanes]
            bias = jnp.concatenate([bias_sc[kind, p, a2] for a2 in range(NA_KROWS // 2)], axis=1)
            s_nb = _dot_nt(qs, kl) + bias
            s_ctx = _dot_nt(qs, kc_ref[0, :, lanes])
            m = jnp.maximum(jnp.max(s_nb, axis=-1, keepdims=True), jnp.max(s_ctx, axis=-1, keepdims=True))
            p_nb = jnp.exp2(s_nb - m)
            p_ctx = jnp.exp2(s_ctx - m)
            denom = jnp.sum(p_nb, axis=-1, keepdims=True) + jnp.sum(p_ctx, axis=-1, keepdims=True)
            o = _dot(p_nb.astype(BF16), vl) + _dot(p_ctx.astype(BF16), vc_ref[0, :, lanes])
            o = o * (1.0 / denom)
            o_ref[0, qrows, lanes] = _unstack_heads(o, 1, tq)[0].astype(o_ref.dtype)


def _na(q, k, v, kc, vc, rpb_pairs, *, ngrp=4):
    b, s, _ = q.shape
    m = kc.shape[1]
    rows = s // GRID_W
    groups = rows // NA_QROWS
    tq = NA_QROWS * GRID_W
    assert rows % NA_QROWS == 0 and groups >= 4 and groups % ngrp == 0
    return pl.pallas_call(
        functools.partial(_na_kernel, rows=rows, ngrp=ngrp),
        grid=(b, groups // ngrp),
        in_specs=[
            pl.BlockSpec((1, ngrp * tq, GROUP_WIDTH), lambda i, g: (i, g, 0)),
            pl.BlockSpec((1, s, GROUP_WIDTH), lambda i, g: (i, 0, 0)),
            pl.BlockSpec((1, s, GROUP_WIDTH), lambda i, g: (i, 0, 0)),
            pl.BlockSpec((1, m, GROUP_WIDTH), lambda i, g: (i, 0, 0)),
            pl.BlockSpec((1, m, GROUP_WIDTH), lambda i, g: (i, 0, 0)),
            pl.BlockSpec((4, NA_RPB_ROWS, PAIR), lambda i, g: (0, 0, 0)),
        ],
        out_specs=pl.BlockSpec((1, ngrp * tq, GROUP_WIDTH), lambda i, g: (i, g, 0)),
        out_shape=jax.ShapeDtypeStruct((b, s, GROUP_WIDTH), BF16),
        scratch_shapes=[pltpu.VMEM((3, 2, NA_KROWS // 2, 2 * tq, PAIR), F32)],
        compiler_params=_cparams("arbitrary", "arbitrary"),
        name="na_attn",
    )(q, k, v, kc, vc, rpb_pairs)


def _ssd_kernel(uf_ref, ub_ref, dtf_ref, dtb_ref, dttf_ref, dttb_ref, dtbias_ref, alog_ref,
                dtbias_col_ref, alog_col_ref, dskip_ref, h0f_ref, h0b_ref,
                yf_ref, yb_ref, hf_ref, hb_ref, hf_sc, hb_sc, *, steps, group):
    i = pl.program_id(1)
    q = SSD_CHUNK

    @pl.when(i == 0)
    def _():
        hf_sc[...] = h0f_ref[0]
        hb_sc[...] = h0b_ref[0]

    ii = lax.broadcasted_iota(jnp.int32, (q, q), 0)
    jj = lax.broadcasted_iota(jnp.int32, (q, q), 1)
    lo_lanes = jj < HEAD_DIM
    a_coef = -jnp.exp(alog_ref[...])
    a_coef_col = -jnp.exp(alog_col_ref[...])

    def chunk(u, dt_raw, dt_raw_t, col0, reverse, h):
        causal = (jj >= ii) if reverse else (jj <= ii)
        dt = _softplus(dt_raw + dtbias_ref[...])
        dt_t = _softplus(dt_raw_t + dtbias_col_ref[...])
        tri = jnp.where(causal, 1.0, 0.0).astype(BF16)
        tri_t = jnp.where((ii >= jj) if reverse else (ii <= jj), 1.0, 0.0).astype(BF16)
        a_hi, a_mid, a_lo = _split3(dt * a_coef)
        cum = _dot(tri, a_hi) + _dot(tri, a_mid) + _dot(tri, a_lo)
        a_hi, a_mid, a_lo = _split3(dt_t * a_coef_col)
        cum_t = _dot(a_hi, tri_t) + _dot(a_mid, tri_t) + _dot(a_lo, tri_t)
        end = cum[0:1, :] if reverse else cum[q - 1:q, :]
        xs = u[:, :2 * PAIR]
        ys, h_out = [], []
        for k in range(2):
            bk = u[:, 2 * PAIR + k * SSD_STATE: 2 * PAIR + (k + 1) * SSD_STATE]
            ck = u[:, 2 * PAIR + 2 * SSD_STATE + k * SSD_STATE: 2 * PAIR + 2 * SSD_STATE + (k + 1) * SSD_STATE]
            xk_b = xs[:, k * PAIR:(k + 1) * PAIR]
            xk = xk_b.astype(F32)
            ck_b = ck
            cb = _dot_nt(ck_b, bk)
            c0 = col0 + 2 * k
            cols = []
            for r in range(2):
                c = c0 + r
                seg = cum[:, c:c + 1] - cum_t[c:c + 1, :]
                lmat = jnp.exp(jnp.where(causal, seg, NEG))
                att = (cb * lmat * dt_t[c:c + 1, :]).astype(BF16)
                cols.append(_dot(att, xk_b))
            y_intra = jnp.where(lo_lanes, cols[0], cols[1])
            e_in = jnp.where(lo_lanes, jnp.exp(cum[:, c0:c0 + 1]), jnp.exp(cum[:, c0 + 1:c0 + 2]))
            h_t = h[k]
            y_state = _dot(ck_b, h_t.astype(BF16)) * e_in
            w0 = jnp.exp(end[:, c0:c0 + 1] - cum[:, c0:c0 + 1]) * dt[:, c0:c0 + 1]
            w1 = jnp.exp(end[:, c0 + 1:c0 + 2] - cum[:, c0 + 1:c0 + 2]) * dt[:, c0 + 1:c0 + 2]
            xw = (xk * jnp.where(lo_lanes, w0, w1)).astype(BF16)
            st = _dot(bk.astype(F32).T.astype(BF16), xw)
            decay = jnp.where(lo_lanes[0:1], jnp.exp(end[:, c0:c0 + 1]), jnp.exp(end[:, c0 + 1:c0 + 2]))
            h_out.append(h_t * decay + st)
            ys.append(y_intra + y_state)
        return ys, xs, h_out

    dskip = dskip_ref[...]
    h = [hf_sc[0], hf_sc[1]]
    for j in range(group):
        rows = slice(j * q, (j + 1) * q)
        ys, xs, h = chunk(uf_ref[0, rows, :], dtf_ref[0, rows, :], dttf_ref[0, :, rows], 0, False, h)
        for k in range(2):
            lanes = slice(k * PAIR, (k + 1) * PAIR)
            yf_ref[0, rows, lanes] = ys[k] + dskip[:, lanes] * xs[:, lanes].astype(F32)
    hf_sc[0], hf_sc[1] = h

    h = [hb_sc[0], hb_sc[1]]
    for j in reversed(range(group)):
        rows = slice(j * q, (j + 1) * q)
        ys, _, h = chunk(ub_ref[0, rows, :], dtb_ref[0, rows, :], dttb_ref[0, :, rows], 4, True, h)
        for k in range(2):
            yb_ref[0, rows, k * PAIR:(k + 1) * PAIR] = ys[k]
    hb_sc[0], hb_sc[1] = h

    @pl.when(i == steps - 1)
    def _():
        hf_ref[0] = hf_sc[...]
        hb_ref[0] = hb_sc[...]


def _ssd(u, dt, dt_bias, a_log, dskip, h0f, h0b, *, group=8):
    b, length, ch = u.shape
    nc = length // SSD_CHUNK
    while nc % group:
        group //= 2
    steps = nc // group
    q = group * SSD_CHUNK
    dt_t = jnp.swapaxes(dt[:, :, :DT_ROWS], 1, 2)
    column = lambda v: jnp.broadcast_to(v[0, :DT_ROWS, None], (DT_ROWS, DT_PAD))
    fwd = lambda i, c: (i, c, 0)
    bwd = lambda i, c: (i, steps - 1 - c, 0)
    fwd_t = lambda i, c: (i, 0, c)
    bwd_t = lambda i, c: (i, 0, steps - 1 - c)
    const = lambda i, c: (0, 0)
    state = lambda i, c: (i, 0, 0, 0)
    state_shape = (b, 2, SSD_STATE, PAIR)
    return pl.pallas_call(
        functools.partial(_ssd_kernel, steps=steps, group=group),
        grid=(b, steps),
        in_specs=[
            pl.BlockSpec((1, q, ch), fwd),
            pl.BlockSpec((1, q, ch), bwd),
            pl.BlockSpec((1, q, DT_PAD), fwd),
            pl.BlockSpec((1, q, DT_PAD), bwd),
            pl.BlockSpec((1, DT_ROWS, q), fwd_t),
            pl.BlockSpec((1, DT_ROWS, q), bwd_t),
            pl.BlockSpec((1, DT_PAD), const),
            pl.BlockSpec((1, DT_PAD), const),
            pl.BlockSpec((DT_ROWS, DT_PAD), const),
            pl.BlockSpec((DT_ROWS, DT_PAD), const),
            pl.BlockSpec((1, GROUP_WIDTH), const),
            pl.BlockSpec((1, 2, SSD_STATE, PAIR), state),
            pl.BlockSpec((1, 2, SSD_STATE, PAIR), state),
        ],
        out_specs=[
            pl.BlockSpec((1, q, GROUP_WIDTH), fwd),
            pl.BlockSpec((1, q, GROUP_WIDTH), bwd),
            pl.BlockSpec((1, 2, SSD_STATE, PAIR), state),
            pl.BlockSpec((1, 2, SSD_STATE, PAIR), state),
        ],
        out_shape=[
            jax.ShapeDtypeStruct((b, length, GROUP_WIDTH), F32),
            jax.ShapeDtypeStruct((b, length, GROUP_WIDTH), F32),
            jax.ShapeDtypeStruct(state_shape, F32),
            jax.ShapeDtypeStruct(state_shape, F32),
        ],
        scratch_shapes=[pltpu.VMEM((2, SSD_STATE, PAIR), F32), pltpu.VMEM((2, SSD_STATE, PAIR), F32)],
        compiler_params=_cparams("arbitrary", "arbitrary"),
        name="ssd_scan",
    )(u, u, dt, dt, dt_t, dt_t, dt_bias, a_log, column(dt_bias), column(a_log), dskip, h0f, h0b)


def _head_blocks(start, perm):
    return [(start + h * HEAD_DIM, start + (h + 1) * HEAD_DIM) for h in perm]


def _take_blocks(a, axis, blocks):
    return jnp.concatenate([lax.slice_in_dim(a, lo, hi, axis=axis) for lo, hi in blocks], axis=axis)


_INPROJ_BLOCKS = (_head_blocks(0, HEAD_PERM) + [(256, 1536)] + _head_blocks(1544, HEAD_PERM)
                  + [(1800, 2824), (1536, 1544)])
_OUTPROJ_BLOCKS = (_head_blocks(0, HEAD_PERM) + [(256, 512)] + _head_blocks(512, HEAD_PERM)
                   + [(768, 1024)])


def _rope_tables(seq):
    t = np.arange(seq)
    quarter = HEAD_DIM // 4
    inv = ROPE_BASE ** (-jnp.arange(quarter, dtype=F32) / quarter)
    a_row = jnp.asarray(t // GRID_W, F32)[:, None] * inv
    a_col = jnp.asarray(t % GRID_W, F32)[:, None] * inv
    cos = jnp.concatenate([jnp.cos(a_row), jnp.cos(a_row), jnp.cos(a_col), jnp.cos(a_col)], axis=1)
    sin = jnp.concatenate([-jnp.sin(a_row), jnp.sin(a_row), -jnp.sin(a_col), jnp.sin(a_col)], axis=1)
    return jnp.tile(cos, (1, 2)), jnp.tile(sin, (1, 2))


def _pad_lanes(v, width):
    v = v.reshape(1, -1).astype(F32)
    return jnp.pad(v, ((0, 0), (0, width - v.shape[1])))


def _token_tile(n, cap):
    t = cap
    while n % t:
        t //= 2
    return t


def kernel(x, c, ctx, c_ctx, ada_w, ada_b, ln_g, ln_b, ffn1_w_in, ffn1_w_out, mix_w_in, mix_w_out,
           mix_norm_g, swa_sink, ssd_conv_w, ssd_conv_b, ssd_dt_bias, ssd_A_log, ssd_D,
           gqa_q_norm, gqa_k_norm, na_rpb, ffn2_w_in, ffn2_w_out):
    bsz, seq, d = x.shape
    m_ctx = ctx.shape[1]
    depth = ada_w.shape[0]
    alpha = float((2 * depth) ** 0.25)
    mod_rows = 16 * ((bsz + 1 + 15) // 16)
    cc = jnp.concatenate([c, c_ctx[None], jnp.zeros((mod_rows - bsz - 1, d), F32)], axis=0)
    mods = _ada(cc, ada_w, ada_b)

    cos_x, sin_x = _rope_tables(seq)
    tm_x = _token_tile(seq, 1024)
    tm_p = _token_tile(seq, 1024)
    tm_c = _token_tile(m_ctx, 256)
    cos_c = jnp.ones((tm_c, PAIR), F32)
    sin_c = jnp.zeros((tm_c, PAIR), F32)
    blockdiag = jnp.asarray(np.kron(np.eye(2), np.ones((HEAD_DIM, HEAD_DIM))), BF16)
    w1_in, w1_out = ffn1_w_in.astype(BF16), ffn1_w_out.astype(BF16)
    w2_in, w2_out = ffn2_w_in.astype(BF16), ffn2_w_out.astype(BF16)
    w_mix = _take_blocks(mix_w_in, 2, _INPROJ_BLOCKS).astype(BF16)
    w_mix = jnp.pad(w_mix, ((0, 0), (0, 0), (0, _P_END - w_mix.shape[2])))
    w_o = _take_blocks(mix_w_out, 1, _OUTPROJ_BLOCKS).astype(BF16)
    norm_g_all = _take_blocks(mix_norm_g, 1, _OUTPROJ_BLOCKS)

    xt = x.reshape(bsz * seq, d)
    ct = ctx.reshape(bsz * m_ctx, d)
    zero_state = jnp.zeros((bsz, 2, SSD_STATE, PAIR), F32)

    for l in range(depth):
        last = l == depth - 1
        mod_x = mods[l, :bsz].reshape(bsz, N_MOD, d)
        mod_c = mods[l, bsz:bsz + 1].reshape(1, N_MOD, d)
        norm_g = norm_g_all[l].reshape(1, d)
        qg = jnp.tile(gqa_q_norm[l], 2).reshape(1, PAIR)
        kg = jnp.tile(gqa_k_norm[l], 2).reshape(1, PAIR)
        conv_w = jnp.pad(ssd_conv_w[l].reshape(SSD_CONV, SSD_CONV_CH), ((0, 8 - SSD_CONV), (0, 0)))
        conv_b = ssd_conv_b[l].reshape(1, SSD_CONV_CH)
        dt_bias = _pad_lanes(ssd_dt_bias[l], DT_PAD)
        a_log = _pad_lanes(ssd_A_log[l], DT_PAD)
        dskip = jnp.repeat(ssd_D[l], HEAD_DIM).reshape(1, GROUP_WIDTH)
        sink = swa_sink[l].astype(F32)
        rpb_pairs = _na_rpb_pairs(na_rpb[l]) * LOG2E

        ffn_x = functools.partial(_ffn, tokens_per_mod=seq, alpha=alpha, tm=tm_x)
        ffn_c = functools.partial(_ffn, tokens_per_mod=bsz * m_ctx, alpha=alpha, tm=tm_c)

        xt = ffn_x(xt, mod_x, (0, 1, 2), w1_in, w1_out, l, ln_g[l, 0], ln_b[l, 0])
        ct = ffn_c(ct, mod_c, (0, 1, 2), w1_in, w1_out, l, ln_g[l, 0], ln_b[l, 0])

        px = _inproj(xt, mod_x, w_mix, l, cos_x, sin_x, qg, kg, blockdiag, conv_w, conv_b,
                     tokens_per_mod=seq, seq_len=seq, tm=tm_p, gqa_q_scale=Q_SCALE * LOG2E)
        pc = _inproj(ct, mod_c, w_mix, l, cos_c, sin_c, qg, kg, blockdiag, conv_w, conv_b,
                     tokens_per_mod=bsz * m_ctx, seq_len=m_ctx, tm=tm_c, gqa_q_scale=Q_SCALE)
        aq, ak, av, bz, bu, bdt, cq, ck, cv, dq, dk, dv = [
            t.reshape(bsz, seq, t.shape[-1]) for t in px]
        aq_c, ak_c, av_c, bz_c, bu_c, bdt_c, cq_c, ck_c, cv_c, dq_c, dk_c, dv_c = [
            t.reshape(bsz, m_ctx, t.shape[-1]) for t in pc]

        ssd = functools.partial(_ssd, dt_bias=dt_bias, a_log=a_log, dskip=dskip)
        yf_c, yb_c, hf_c, hb_c = ssd(bu_c, bdt_c, h0f=zero_state, h0b=zero_state)
        yf, yb, _, _ = ssd(bu, bdt, h0f=hf_c, h0b=hb_c)

        oa = _swa(sink, aq, ak, av, ak_c, av_c)
        oc = _gqa(cq, ck, cv, ck_c, cv_c)
        od = _na(dq, dk, dv, dk_c, dv_c, rpb_pairs)

        flat = lambda t: t.reshape(-1, t.shape[-1])
        xt = _mixer_ffn(xt, mod_x, flat(oa), flat(yf), flat(yb), flat(bz), flat(oc), flat(od), norm_g, w_o,
                        ln_g[l, 1], ln_b[l, 1], w2_in, w2_out, l, ln_g[l, 2], ln_b[l, 2],
                        tokens_per_mod=seq, alpha=alpha, tm=tm_x)

        if not last:
            tq_c = _token_tile(m_ctx, 128)
            oa_c = _dense_attn(aq_c, [(ak_c, av_c)], kv_pairs=1, sink=sink, tq=tq_c)
            oc_c = _dense_attn(cq_c, [(ck_c, cv_c)], kv_pairs=1, tq=tq_c)
            od_c = _dense_attn(dq_c, [(dk_c, dv_c)], kv_pairs=2, tq=tq_c)
            ct = _mixer_ffn(ct, mod_c, flat(oa_c), flat(yf_c), flat(yb_c), flat(bz_c), flat(oc_c),
                            flat(od_c), norm_g, w_o, ln_g[l, 1], ln_b[l, 1], w2_in, w2_out, l,
                            ln_g[l, 2], ln_b[l, 2], tokens_per_mod=bsz * m_ctx, alpha=alpha, tm=tm_c)

    return xt.reshape(bsz, seq, d)
```
